```python
import math
import jax
import jax.numpy as jnp
from jax import lax
import numpy as np

D_MODEL = 1024
BATCH = 8
SEQ = 2048
DEPTH = 2
DEC_BATCH = 128
DEC_SEQ = 8
PAST_LEN = 2048
PAGE_SIZE = 128

N_EVEN = (DEPTH + 1) // 2
N_ODD = DEPTH // 2

H_A = 4
DK_A = 128
DV_A = 128
CONV_A = 4
CHUNK_A = 64
QKV_A = H_A * (2 * DK_A + DV_A)
D_B = 512
CONV_B = 3
IN_EVEN = QKV_A + H_A * DV_A + 2 * H_A + 3 * D_B
CAT_EVEN = H_A * DV_A + D_B
H_C = 16
HD_C = 64
BRANCHES = ((128, 1), (512, 4), (2048, 16))
WINDOW_MAX = 2048
BAND_BLK = 128
N_GROUPS = 4
E_PER_GROUP = 8
TOP_K_IN_GROUP = 2
D_EXPERT = 256
EPS = 1e-6

kernel_name = 'hybrid_deltanet_shortconv_dilatedattn_hmoe_step'


def rms_norm(x, gain):
    xf = x.astype(jnp.float32)
    y = xf * lax.rsqrt(jnp.mean(xf * xf, axis=-1, keepdims=True) + EPS)
    return (y * gain.astype(jnp.float32)).astype(x.dtype)


def l2norm(x):
    return x * lax.rsqrt(jnp.sum(x * x, axis=-1, keepdims=True) + EPS)


def causal_dwconv(x_ext, w):
    k_w = w.shape[0]
    length = x_ext.shape[1] - k_w + 1
    return sum(x_ext[:, j:j + length] * w[j] for j in range(k_w))


def gated_delta_chunked(q, k, v, g, beta, s0):
    n, length, h, _ = q.shape
    dv = v.shape[-1]
    c = CHUNK_A
    nc = -(-length // c)
    pad = nc * c - length

    def blocks(x):
        x = jnp.pad(x, [(0, 0), (0, pad)] + [(0, 0)] * (x.ndim - 2))
        x = x.reshape((n, nc, c) + x.shape[2:])
        return jnp.moveaxis(x, 3, 1)

    qb, kb, vb, gb, bb = (blocks(t) for t in (q, k, v, g, beta))
    gcum = jnp.cumsum(gb, axis=-1)
    pos = jnp.arange(c)
    lower_incl = pos[:, None] >= pos[None, :]
    strict = pos[:, None] > pos[None, :]
    gamma = jnp.exp(jnp.where(lower_incl, gcum[..., :, None] - gcum[..., None, :], -jnp.inf))
    kk = jnp.einsum('nhcid,nhcjd->nhcij', kb, kb)
    a_mat = jnp.where(strict, bb[..., :, None] * kk * gamma, 0.0)
    t_mat = a_mat + jnp.eye(c, dtype=a_mat.dtype)
    rhs = jnp.concatenate([vb * bb[..., None], kb * (bb * jnp.exp(gcum))[..., None]], axis=-1)
    sol = lax.linalg.triangular_solve(t_mat, rhs, left_side=True, lower=True, unit_diagonal=True)
    u_base, w_mat = sol[..., :dv], sol[..., dv:]
    qk = jnp.einsum('nhcid,nhcjd->nhcij', qb, kb) * gamma
    q_dec = qb * jnp.exp(gcum)[..., None]
    k_dec = kb * jnp.exp(gcum[..., -1:] - gcum)[..., None]
    decay = jnp.exp(gcum[..., -1])

    def step(s, xs):
        u_b, w_c, qk_c, qd_c, kd_c, dec_c = xs
        u = u_b - jnp.einsum('nhck,nhkv->nhcv', w_c, s)
        o = jnp.einsum('nhck,nhkv->nhcv', qd_c, s) + jnp.einsum('nhij,nhjv->nhiv', qk_c, u)
        s = s * dec_c[..., None, None] + jnp.einsum('nhck,nhcv->nhkv', kd_c, u)
        return s, o

    xs = tuple(jnp.moveaxis(t, 2, 0) for t in (u_base, w_mat, qk, q_dec, k_dec, decay))
    s_fin, o = lax.scan(step, s0, xs)
    o = jnp.transpose(o, (1, 0, 3, 2, 4)).reshape(n, nc * c, h, dv)[:, :length]
    return o, s_fin


def even_mixer(h, a_conv_hist, b_conv_hist, s0, w_in, conv_a_w, a_log, dt_bias, o_gain, conv_b_w, w_out):
    n, length, _ = h.shape
    f32 = jnp.float32
    sizes = (QKV_A, H_A * DV_A, H_A, H_A, D_B, D_B, D_B)
    offs = np.cumsum(sizes)[:-1].tolist()
    qkv_pre, z, a, b, b_gate, c_gate, x_in = jnp.split(h @ w_in, offs, axis=-1)
    qkv_ext = jnp.concatenate([a_conv_hist.astype(h.dtype), qkv_pre], axis=1)
    qkv = jax.nn.silu(causal_dwconv(qkv_ext, conv_a_w)).astype(f32)
    q, k, v = jnp.split(qkv, [H_A * DK_A, 2 * H_A * DK_A], axis=-1)
    q = l2norm(q.reshape(n, length, H_A, DK_A)) * DK_A ** -0.5
    k = l2norm(k.reshape(n, length, H_A, DK_A))
    v = v.reshape(n, length, H_A, DV_A)
    g = -jnp.exp(a_log.astype(f32)) * jax.nn.softplus(a.astype(f32) + dt_bias.astype(f32))
    beta = jax.nn.sigmoid(b.astype(f32))
    o, s_new = gated_delta_chunked(q, k, v, g, beta, s0.astype(f32))
    o = rms_norm(o, o_gain) * jax.nn.silu(z.astype(f32).reshape(n, length, H_A, DV_A))
    out_a = o.reshape(n, length, H_A * DV_A).astype(h.dtype)
    bx = c_gate * x_in
    bx_ext = jnp.concatenate([b_conv_hist.astype(h.dtype), bx], axis=1)
    out_b = b_gate * causal_dwconv(bx_ext, conv_b_w)
    y = jnp.concatenate([out_a, out_b], axis=-1) @ w_out
    return y.astype(h.dtype), s_new, qkv_ext[:, -(CONV_A - 1):], bx_ext[:, -(CONV_B - 1):]


def band_attention(q, k, v, span):
    n, length, h, d = q.shape
    blk = BAND_BLK
    nb = -(-length // blk)
    pad = nb * blk - length
    padw = ((0, 0), (0, pad), (0, 0), (0, 0))
    qb = jnp.pad(q, padw).reshape(n, nb, blk, h, d)
    kb = jnp.pad(k, padw).reshape(n, nb, blk, h, d)
    vb = jnp.pad(v, padw).reshape(n, nb, blk, h, d)

    def with_prev(xb):
        prev = jnp.pad(xb, ((0, 0), (1, 0), (0, 0), (0, 0), (0, 0)))[:, :-1]
        return jnp.concatenate([prev, xb], axis=2)

    kc, vc = with_prev(kb), with_prev(vb)
    s = jnp.einsum('nbqhd,nbkhd->nbhqk', qb, kc, preferred_element_type=jnp.float32) * d ** -0.5
    bidx = jnp.arange(nb)[:, None, None] * blk
    qpos = bidx + jnp.arange(blk)[None, :, None]
    kpos = bidx - blk + jnp.arange(2 * blk)[None, None, :]
    dist = qpos - kpos
    mask = (dist >= 0) & (dist <= span) & (kpos >= 0)
    s = jnp.where(mask[None, :, None], s, -jnp.inf)
    m = jnp.max(s, axis=-1, keepdims=True)
    p = jnp.exp(s - m)
    den = jnp.sum(p, axis=-1)
    o = jnp.einsum('nbhqk,nbkhd->nbqhd', p, vc.astype(jnp.float32))
    o = o / jnp.swapaxes(den, -1, -2)[..., None]
    lse = jnp.swapaxes(m[..., 0] + jnp.log(den), -1, -2)
    return o.reshape(n, nb * blk, h, d)[:, :length], lse.reshape(n, nb * blk, h)[:, :length]


def merge_branches(outs, lses):
    alpha = jax.nn.softmax(jnp.stack(lses, axis=-1), axis=-1)
    return jnp.einsum('nlhgd,nlhg->nlhd', jnp.stack(outs, axis=-2), alpha)


def dilated_attention_prompt(q, k, v):
    n, s_len, h, d = q.shape
    outs, lses = [], []
    for window, dil in BRANCHES:
        sub = s_len // dil

        def to_sub(x):
            return x.reshape(n, sub, dil, h, d).transpose(0, 2, 1, 3, 4).reshape(n * dil, sub, h, d)

        o, lse = band_attention(to_sub(q), to_sub(k), to_sub(v), window // dil)
        outs.append(o.reshape(n, dil, sub, h, d).transpose(0, 2, 1, 3, 4).reshape(n, s_len, h, d))
        lses.append(lse.reshape(n, dil, sub, h).transpose(0, 2, 1, 3).reshape(n, s_len, h))
    return merge_branches(outs, lses)


def dilated_attention_sample(q, k_new, v_new, k_buf, v_buf):
    n, t_len, h, d = q.shape
    w_buf = k_buf.shape[1]
    kk = jnp.concatenate([k_buf.astype(k_new.dtype), k_new], axis=1)
    vv = jnp.concatenate([v_buf.astype(v_new.dtype), v_new], axis=1)
    outs, lses = [], []
    for window, dil in BRANCHES:
        n_keys = window // dil + 1
        idx = w_buf + jnp.arange(t_len)[:, None] - dil * jnp.arange(n_keys)[None, :]
        valid = idx >= 0
        idx = jnp.maximum(idx, 0)
        kg = kk[:, idx]
        vg = vv[:, idx]
        s = jnp.einsum('nthd,ntjhd->nthj', q, kg, preferred_element_type=jnp.float32) * d ** -0.5
        s = jnp.where(valid[None, :, None, :], s, -jnp.inf)
        m = jnp.max(s, axis=-1, keepdims=True)
        p = jnp.exp(s - m)
        den = jnp.sum(p, axis=-1)
        o = jnp.einsum('nthj,ntjhd->nthd', p, vg.astype(jnp.float32)) / den[..., None]
        outs.append(o)
        lses.append(m[..., 0] + jnp.log(den))
    return merge_branches(outs, lses)


def hier_moe(h, w_rg, b_rg, w_re, b_re, w_gate, w_up, w_down):
    n, length, d = h.shape
    f32 = jnp.float32
    x = h.reshape(n * length, d)
    p_group = jax.nn.softmax((x @ w_rg + b_rg).astype(f32), axis=-1)
    p_top, g_top = lax.top_k(p_group, 1)
    g_onehot = jax.nn.one_hot(g_top[:, 0], N_GROUPS, dtype=f32)
    e_logits = (x @ w_re + b_re).astype(f32).reshape(-1, N_GROUPS, E_PER_GROUP)
    p_in = jax.nn.softmax(jnp.einsum('tge,tg->te', e_logits, g_onehot), axis=-1)
    w_top, e_top = lax.top_k(p_in, TOP_K_IN_GROUP)
    w_top = w_top / jnp.sum(w_top, axis=-1, keepdims=True)
    gate_in = jnp.einsum('tk,tke->te', w_top, jax.nn.one_hot(e_top, E_PER_GROUP, dtype=f32))
    gate = g_onehot[:, :, None] * (p_top * gate_in)[:, None, :]
    y = jnp.zeros((n * length, d), f32)
    for gi in range(N_GROUPS):
        hid = jax.nn.silu(jnp.einsum('td,edf->tef', x, w_gate[gi])) * jnp.einsum('td,edf->tef', x, w_up[gi])
        y = y + jnp.einsum('tef,efd->td', hid * gate[:, gi, :, None], w_down[gi])
    return y.astype(h.dtype).reshape(n, length, d)


def trunk(x, c, a_ssm, a_conv, b_conv, kv_k, kv_v, p, sample):
    n, length, _ = x.shape
    cond = jax.nn.silu(c)
    new_ssm, new_aconv, new_bconv, new_k, new_v = [], [], [], [], []
    for layer in range(DEPTH):
        mod = cond @ p['w_ada'][layer] + p['b_ada'][layer]
        sh1, sc1, g1, sh2, sc2, g2 = [m[:, None, :] for m in jnp.split(mod, 6, axis=-1)]
        h = rms_norm(x, p['norm1'][layer]) * (1 + sc1) + sh1
        if layer % 2 == 0:
            e = layer // 2
            y, s_new, ca, cb = even_mixer(h, a_conv[e], b_conv[e], a_ssm[e], p['w_in_even'][e],
                                          p['conv_a_w'][e], p['a_log'][e], p['dt_bias'][e],
                                          p['o_gain_a'][e], p['conv_b_w'][e], p['w_out_even'][e])
            new_ssm.append(s_new)
            new_aconv.append(ca)
            new_bconv.append(cb)
        else:
            o_i = layer // 2
            qkv = (h @ p['w_qkv_odd'][o_i]).reshape(n, length, 3, H_C, HD_C)
            q, k, v = qkv[:, :, 0], qkv[:, :, 1], qkv[:, :, 2]
            if sample:
                attn = dilated_attention_sample(q, k, v, kv_k[o_i], kv_v[o_i])
                new_k.append(k)
                new_v.append(v)
            else:
                attn = dilated_attention_prompt(q, k, v)
                keep = min(WINDOW_MAX, length)
                new_k.append(k[:, length - keep:])
                new_v.append(v[:, length - keep:])
            y = attn.astype(x.dtype).reshape(n, length, H_C * HD_C) @ p['w_out_odd'][o_i]
        x = x + g1 * y.astype(x.dtype)
        h = rms_norm(x, p['norm2'][layer]) * (1 + sc2) + sh2
        x = x + g2 * hier_moe(h, p['w_router_group'][layer], p['b_router_group'][layer],
                              p['w_router_expert'][layer], p['b_router_expert'][layer],
                              p['w_exp_gate'][layer], p['w_exp_up'][layer], p['w_exp_down'][layer])
    y_out = rms_norm(x, p['norm_f'])
    return (y_out, jnp.stack(new_ssm), jnp.stack(new_aconv), jnp.stack(new_bconv),
            jnp.stack(new_k), jnp.stack(new_v))


def setup_inputs(seed: int = 0) -> dict:
    key = jax.random.key(seed)
    ks = iter(jax.random.split(key, 40))
    f32 = jnp.float32
    d = D_MODEL
    w_buf = min(WINDOW_MAX, PAST_LEN)

    def nrm(shape, scale=1.0):
        return jax.random.normal(next(ks), shape, f32) * scale

    x_prompt = nrm((BATCH, SEQ, d))
    x_sample = nrm((DEC_BATCH, DEC_SEQ, d))
    state_a_ssm = nrm((N_EVEN, DEC_BATCH, H_A, DK_A, DV_A), DK_A ** -0.5)
    state_a_conv = nrm((N_EVEN, DEC_BATCH, CONV_A - 1, QKV_A))
    state_b_conv = nrm((N_EVEN, DEC_BATCH, CONV_B - 1, D_B))
    cache_c_k = nrm((N_ODD, DEC_BATCH, w_buf, H_C, HD_C))
    cache_c_v = nrm((N_ODD, DEC_BATCH, w_buf, H_C, HD_C))
    c_prompt = nrm((BATCH, d))
    c_sample = nrm((DEC_BATCH, d))
    w_ada = nrm((DEPTH, d, 6 * d), 0.5 * d ** -0.5)
    b_ada = nrm((DEPTH, 6 * d), 0.02)
    norm1 = 1.0 + nrm((DEPTH, d), 0.02)
    norm2 = 1.0 + nrm((DEPTH, d), 0.02)
    norm_f = 1.0 + nrm((d,), 0.02)
    w_in_even = nrm((N_EVEN, d, IN_EVEN), d ** -0.5)
    conv_a_w = nrm((N_EVEN, CONV_A, QKV_A), CONV_A ** -0.5)
    a_log = jnp.log(jax.random.uniform(next(ks), (N_EVEN, H_A), f32, 1.0, 16.0))
    dt = jnp.exp(jax.random.uniform(next(ks), (N_EVEN, H_A), f32, math.log(1e-3), math.log(1e-1)))
    dt_bias = dt + jnp.log(-jnp.expm1(-dt))
    o_gain_a = 1.0 + nrm((N_EVEN, DV_A), 0.02)
    conv_b_w = nrm((N_EVEN, CONV_B, D_B), CONV_B ** -0.5)
    w_out_even = nrm((N_EVEN, CAT_EVEN, d), CAT_EVEN ** -0.5)
    w_qkv_odd = nrm((N_ODD, d, 3 * H_C * HD_C), d ** -0.5)
    w_out_odd = nrm((N_ODD, H_C * HD_C, d), (H_C * HD_C) ** -0.5)
    w_router_group = nrm((DEPTH, d, N_GROUPS), d ** -0.5)
    b_router_group = nrm((DEPTH, N_GROUPS), 0.01)
    w_router_expert = nrm((DEPTH, d, N_GROUPS * E_PER_GROUP), d ** -0.5)
    b_router_expert = nrm((DEPTH, N_GROUPS * E_PER_GROUP), 0.01)
    w_exp_gate = nrm((DEPTH, N_GROUPS, E_PER_GROUP, d, D_EXPERT), d ** -0.5)
    w_exp_up = nrm((DEPTH, N_GROUPS, E_PER_GROUP, d, D_EXPERT), d ** -0.5)
    w_exp_down = nrm((DEPTH, N_GROUPS, E_PER_GROUP, D_EXPERT, d), D_EXPERT ** -0.5)
    return {
        'x_prompt': x_prompt, 'x_sample': x_sample,
        'state_a_ssm': state_a_ssm, 'state_a_conv': state_a_conv, 'state_b_conv': state_b_conv,
        'cache_c_k': cache_c_k, 'cache_c_v': cache_c_v,
        'c_prompt': c_prompt, 'c_sample': c_sample,
        'w_ada': w_ada, 'b_ada': b_ada, 'norm1': norm1, 'norm2': norm2, 'norm_f': norm_f,
        'w_in_even': w_in_even, 'conv_a_w': conv_a_w, 'a_log': a_log, 'dt_bias': dt_bias,
        'o_gain_a': o_gain_a, 'conv_b_w': conv_b_w, 'w_out_even': w_out_even,
        'w_qkv_odd': w_qkv_odd, 'w_out_odd': w_out_odd,
        'w_router_group': w_router_group, 'b_router_group': b_router_group,
        'w_router_expert': w_router_expert, 'b_router_expert': b_router_expert,
        'w_exp_gate': w_exp_gate, 'w_exp_up': w_exp_up, 'w_exp_down': w_exp_down,
    }


def reference(x_prompt, x_sample, state_a_ssm, state_a_conv, state_b_conv, cache_c_k, cache_c_v,
              c_prompt, c_sample, w_ada, b_ada, norm1, norm2, norm_f, w_in_even, conv_a_w, a_log,
              dt_bias, o_gain_a, conv_b_w, w_out_even, w_qkv_odd, w_out_odd, w_router_group,
              b_router_group, w_router_expert, b_router_expert, w_exp_gate, w_exp_up, w_exp_down):
    p = {
        'w_ada': w_ada, 'b_ada': b_ada, 'norm1': norm1, 'norm2': norm2, 'norm_f': norm_f,
        'w_in_even': w_in_even, 'conv_a_w': conv_a_w, 'a_log': a_log, 'dt_bias': dt_bias,
        'o_gain_a': o_gain_a, 'conv_b_w': conv_b_w, 'w_out_even': w_out_even,
        'w_qkv_odd': w_qkv_odd, 'w_out_odd': w_out_odd,
        'w_router_group': w_router_group, 'b_router_group': b_router_group,
        'w_router_expert': w_router_expert, 'b_router_expert': b_router_expert,
        'w_exp_gate': w_exp_gate, 'w_exp_up': w_exp_up, 'w_exp_down': w_exp_down,
    }
    nb = x_prompt.shape[0]
    zero_ssm = jnp.zeros((N_EVEN, nb, H_A, DK_A, DV_A), jnp.float32)
    zero_aconv = jnp.zeros((N_EVEN, nb, CONV_A - 1, QKV_A), x_prompt.dtype)
    zero_bconv = jnp.zeros((N_EVEN, nb, CONV_B - 1, D_B), x_prompt.dtype)
    y_prompt, ssm_p, aconv_p, bconv_p, k_p, v_p = trunk(
        x_prompt, c_prompt, zero_ssm, zero_aconv, zero_bconv, None, None, p, False)
    y_sample, ssm_s, aconv_s, bconv_s, k_s, v_s = trunk(
        x_sample, c_sample, state_a_ssm, state_a_conv, state_b_conv, cache_c_k, cache_c_v, p, True)
    return (y_prompt, y_sample, ssm_p, ssm_s, aconv_p, aconv_s, bconv_p, bconv_s, k_p, k_s, v_p, v_s)
```

```python
import functools
import math

import jax
import jax.numpy as jnp
from jax import lax
from jax.experimental import pallas as pl
from jax.experimental.pallas import tpu as pltpu

F32 = jnp.float32
BF16 = jnp.bfloat16
I32 = jnp.int32

EPS = 1e-6
NEG_BIG = -1e30

LANES = 128
SUBLANES = 8
VMEM_BYTES_V7X = 64 * 1024 * 1024
VMEM_LIMIT = 56 * 1024 * 1024

H_A, DK_A, DV_A = 4, 128, 128
CONV_A, CONV_B = 4, 3
QKV_A = H_A * (2 * DK_A + DV_A)
D_B = 512
H_C, HD_C = 16, 64
BRANCHES = ((128, 1), (512, 4), (2048, 16))
N_GROUPS, E_PER_GROUP = 4, 8
N_EXPERTS = N_GROUPS * E_PER_GROUP
CHUNK_PROMPT = 64

TOK_TILE = 512
MOE_TOK_TILE = 256
MOE_ROW_TILE = 256
ATT_BLK = 256
SEQ_BLOCK_SCAN = 8
SAMPLE_KV_CHUNK = 512
SAMPLE_KV_SUB = 128


def _cparams(sem, vmem=None):
    return pltpu.CompilerParams(dimension_semantics=sem, vmem_limit_bytes=vmem)


def _silu(x):
    return x * jax.nn.sigmoid(x)


def _bdot(a, b):
    return jnp.dot(a.astype(BF16), b.astype(BF16), preferred_element_type=F32)


def _bdot_nt(a, b):
    return lax.dot_general(a.astype(BF16), b.astype(BF16), (((1,), (1,)), ((), ())),
                           preferred_element_type=F32)


def _bdot_tn(a, b):
    return lax.dot_general(a.astype(BF16), b.astype(BF16), (((0,), (0,)), ((), ())),
                           preferred_element_type=F32)


def _fdot(a, b):
    return jnp.dot(a, b, preferred_element_type=F32, precision=lax.Precision.HIGHEST)


def _fdot_nt(a, b):
    return lax.dot_general(a, b, (((1,), (1,)), ((), ())), preferred_element_type=F32,
                           precision=lax.Precision.HIGHEST)


def _tok_tiling(n_seq, length, tile):
    if length >= tile:
        sb, lb = 1, tile
    else:
        sb, lb = tile // length, length
    nl = length // lb
    return sb, lb, (n_seq // sb) * nl, nl


def _norm_mod(x, gain, sc, sh):
    sb, lb, d = x.shape
    y = x * lax.rsqrt(jnp.mean(x * x, axis=-1, keepdims=True) + EPS) * gain
    h = y * (1.0 + sc) + sh
    return h.reshape(sb * lb, d)


def _ada_kernel(c_ref, w_ref, b_ref, o_ref):
    c = _silu(c_ref[...])
    o_ref[0] = _bdot(c, w_ref[0]) + b_ref[0]


def _ada_mod(c_all, w_ada, b_ada):
    r, d = c_all.shape
    depth, _, n6 = w_ada.shape
    tn = 1536
    return pl.pallas_call(
        _ada_kernel,
        grid=(depth, n6 // tn),
        in_specs=[pl.BlockSpec((r, d), lambda l, j: (0, 0)),
                  pl.BlockSpec((1, d, tn), lambda l, j: (l, 0, j)),
                  pl.BlockSpec((1, 1, tn), lambda l, j: (l, 0, j))],
        out_specs=pl.BlockSpec((1, r, tn), lambda l, j: (l, 0, j)),
        out_shape=jax.ShapeDtypeStruct((depth, r, n6), F32),
        compiler_params=_cparams(("arbitrary", "arbitrary"), VMEM_LIMIT),
        name="ada_mod",
    )(c_all, w_ada, b_ada.reshape(depth, 1, n6))


def _nmm_kernel(x_ref, sh_ref, sc_ref, gain_ref, w_ref, *o_refs, splits):
    h = _norm_mod(x_ref[...], gain_ref[...], sc_ref[...], sh_ref[...]).astype(BF16)
    off = 0
    for o_ref, n in zip(o_refs, splits):
        o_ref[...] = jnp.dot(h, w_ref[:, off:off + n], preferred_element_type=F32)
        off += n


def _norm_mod_matmul(x3, mod3, k_sh, k_sc, gain, w_bf16, splits):
    n_seq, length, d = x3.shape
    sb, lb, steps, nl = _tok_tiling(n_seq, length, TOK_TILE)
    rows = sb * lb
    t = n_seq * length
    n_out = w_bf16.shape[1]
    assert sum(splits) == n_out
    return pl.pallas_call(
        functools.partial(_nmm_kernel, splits=splits),
        grid=(steps,),
        in_specs=[pl.BlockSpec((sb, lb, d), lambda i: (i // nl, i % nl, 0)),
                  pl.BlockSpec((sb, 1, d), lambda i: (i // nl, 0, k_sh)),
                  pl.BlockSpec((sb, 1, d), lambda i: (i // nl, 0, k_sc)),
                  pl.BlockSpec((1, d), lambda i: (0, 0)),
                  pl.BlockSpec((d, n_out), lambda i: (0, 0))],
        out_specs=[pl.BlockSpec((rows, n), lambda i: (i, 0)) for n in splits],
        out_shape=[jax.ShapeDtypeStruct((t, n), F32) for n in splits],
        compiler_params=_cparams(("arbitrary",), VMEM_LIMIT),
        name="norm_mod_matmul",
    )(x3, mod3, mod3, gain.reshape(1, d), w_bf16)


def _delta_prep_kernel(qkv_ref, halo_ref, hist_ref, ab_ref, cw_ref, al_ref, dtb_ref, lt_ref, sel_ref,
                       ub_ref, wm_ref, qd_ref, kd_ref, qk_ref, dec_ref, ext_ref,
                       *, g_chunks, chunk, chunks_are_seqs, chunks_per_seq):
    c = chunk
    step = pl.program_id(0)
    cw = cw_ref[...]
    row = lax.broadcasted_iota(I32, (c, c), 0)
    col = lax.broadcasted_iota(I32, (c, c), 1)
    lower_incl = row >= col
    strict = row > col
    eye = (row == col).astype(F32)
    n_fac = max(int(math.ceil(math.log2(c))), 1)

    for g in range(g_chunks):
        cur = qkv_ref[g] if chunks_are_seqs else qkv_ref[0, g * c:(g + 1) * c, :]
        if chunks_are_seqs:
            prev = hist_ref[g]
        elif g == 0:
            first = (step % (chunks_per_seq // g_chunks)) == 0
            prev = jnp.where(first, hist_ref[0], halo_ref[0])
        else:
            prev = qkv_ref[0, g * c - SUBLANES:g * c, :]
        ext_ref[0:SUBLANES, :] = prev
        ext_ref[SUBLANES:SUBLANES + c, :] = cur
        conv = ext_ref[pl.ds(SUBLANES - (CONV_A - 1), c), :] * cw[0:1, :]
        for j in range(1, CONV_A):
            conv = conv + ext_ref[pl.ds(SUBLANES - (CONV_A - 1) + j, c), :] * cw[j:j + 1, :]
        qkv = _silu(conv)

        ab = ab_ref[g] if chunks_are_seqs else ab_ref[0, g * c:(g + 1) * c, :]
        g_all = -jnp.exp(al_ref[...]) * jax.nn.softplus(ab + dtb_ref[...])
        sig = jax.nn.sigmoid(ab)
        gcum_all = _fdot(lt_ref[...], g_all)
        gcum_rows = _fdot_nt(sel_ref[...], gcum_all)

        ub, wm, qd, kd, qkm, dec = [], [], [], [], [], []
        for h in range(H_A):
            q = qkv[:, h * DK_A:(h + 1) * DK_A]
            k = qkv[:, H_A * DK_A + h * DK_A:H_A * DK_A + (h + 1) * DK_A]
            v = qkv[:, 2 * H_A * DK_A + h * DV_A:2 * H_A * DK_A + (h + 1) * DV_A]
            q = q * lax.rsqrt(jnp.sum(q * q, axis=-1, keepdims=True) + EPS) * (DK_A ** -0.5)
            k = k * lax.rsqrt(jnp.sum(k * k, axis=-1, keepdims=True) + EPS)
            beta = sig[:, H_A + h:H_A + h + 1]
            gc = gcum_all[:, h:h + 1]
            gr = gcum_rows[h:h + 1, :]
            gamma = jnp.exp(jnp.where(lower_incl, gc - gr, NEG_BIG))
            kk = _bdot_nt(k, k)
            a_mat = jnp.where(strict, beta * kk * gamma, 0.0)
            m_pow = -a_mat
            t_inv = eye + m_pow
            for _ in range(n_fac - 1):
                m_pow = _fdot(m_pow, m_pow)
                t_inv = t_inv + _fdot(t_inv, m_pow)
            eg = jnp.exp(gc)
            rhs = jnp.concatenate([v * beta, k * (beta * eg)], axis=-1)
            sol = _fdot(t_inv, rhs)
            ub.append(sol[:, :DV_A])
            wm.append(sol[:, DV_A:])
            qkm.append(_bdot_nt(q, k) * gamma)
            qd.append(q * eg)
            g_last = gcum_all[c - 1:c, h:h + 1]
            kd.append(k * jnp.exp(g_last - gc))
            dec.append(jnp.broadcast_to(jnp.exp(g_last), (SUBLANES, DV_A)))

        def put(ref, parts):
            val = jnp.concatenate(parts, axis=-1)
            if chunks_are_seqs:
                ref[g] = val
            else:
                ref[0, g * c:(g + 1) * c, :] = val

        put(ub_ref, ub)
        put(wm_ref, wm)
        put(qd_ref, qd)
        put(kd_ref, kd)
        put(qk_ref, qkm)
        dec_ref[g] = jnp.concatenate(dec, axis=-1)


def _delta_prep(qkv_pre3, hist8, ab3, conv_w, a_log, dt_bias, chunk, g_chunks):
    n_seq, length, w = qkv_pre3.shape
    c = chunk
    chunks_are_seqs = (length == c)
    nc = length // c
    if chunks_are_seqs:
        steps = n_seq // g_chunks
        blk = (g_chunks, c, w)
        x_map = lambda i: (i, 0, 0)
        halo_map = lambda i: (i, 0, 0)
        hist_spec = pl.BlockSpec((g_chunks, SUBLANES, w), lambda i: (i, 0, 0))
        ab_spec = pl.BlockSpec((g_chunks, c, LANES), lambda i: (i, 0, 0))
        out_map = lambda i: (i, 0, 0)
        out_rows = (g_chunks, c)
    else:
        rows = g_chunks * c
        spb = nc // g_chunks
        steps = n_seq * spb
        blk = (1, rows, w)
        x_map = lambda i: (i // spb, i % spb, 0)
        halo_map = lambda i: (i // spb, jnp.maximum((i % spb) * (rows // SUBLANES) - 1, 0), 0)
        hist_spec = pl.BlockSpec((1, SUBLANES, w), lambda i: (i // spb, 0, 0))
        ab_spec = pl.BlockSpec((1, rows, LANES), lambda i: (i // spb, i % spb, 0))
        out_map = x_map
        out_rows = (1, rows)
    lt = jnp.tril(jnp.ones((c, c), F32))
    sel = jnp.eye(SUBLANES, LANES, dtype=F32)
    hv = H_A * DV_A
    al = jnp.zeros((1, LANES), F32).at[0, :H_A].set(a_log)
    dtb = jnp.zeros((1, LANES), F32).at[0, :H_A].set(dt_bias)
    kern = functools.partial(_delta_prep_kernel, g_chunks=g_chunks, chunk=c,
                             chunks_are_seqs=chunks_are_seqs, chunks_per_seq=nc)
    big = lambda width: pl.BlockSpec(out_rows + (width,), out_map)
    return pl.pallas_call(
        kern,
        grid=(steps,),
        in_specs=[pl.BlockSpec(blk, x_map),
                  pl.BlockSpec((1, SUBLANES, w), halo_map) if not chunks_are_seqs
                  else pl.BlockSpec((g_chunks, SUBLANES, w), halo_map),
                  hist_spec, ab_spec,
                  pl.BlockSpec((CONV_A, w), lambda i: (0, 0)),
                  pl.BlockSpec((1, LANES), lambda i: (0, 0)),
                  pl.BlockSpec((1, LANES), lambda i: (0, 0)),
                  pl.BlockSpec((c, c), lambda i: (0, 0)),
                  pl.BlockSpec((SUBLANES, LANES), lambda i: (0, 0))],
        out_specs=[big(hv), big(hv), big(hv), big(hv), big(H_A * c),
                   pl.BlockSpec((g_chunks, SUBLANES, hv), lambda i: (i, 0, 0))],
        out_shape=[jax.ShapeDtypeStruct((n_seq, length, hv), F32)] * 4
        + [jax.ShapeDtypeStruct((n_seq, length, H_A * c), F32),
           jax.ShapeDtypeStruct((n_seq * nc, SUBLANES, hv), F32)],
        scratch_shapes=[pltpu.VMEM((SUBLANES + c, w), F32)],
        compiler_params=_cparams(("arbitrary",), VMEM_LIMIT),
        name="delta_prep",
    )(qkv_pre3, hist8 if chunks_are_seqs else qkv_pre3, hist8, ab3, conv_w, al, dtb, lt, sel)


def _delta_scan_kernel(ub_ref, wm_ref, qd_ref, kd_ref, qk_ref, dec_ref, z_ref, s0_ref, og_ref,
                       o_ref, sn_ref, s_ref, *, nb, chunk):
    c = chunk
    j = pl.program_id(1)

    @pl.when(j == 0)
    def _():
        s_ref[...] = s0_ref[...]

    og = og_ref[...]
    for b in range(nb):
        outs = []
        for h in range(H_A):
            lo, hi = h * DV_A, (h + 1) * DV_A
            s = s_ref[b, h]
            u = ub_ref[b, :, lo:hi] - _bdot(wm_ref[b, :, lo:hi], s)
            o = _bdot(qd_ref[b, :, lo:hi], s) + _bdot(qk_ref[b, :, h * c:(h + 1) * c], u)
            s_new = s * dec_ref[b, 0, 0:1, lo:hi] + _bdot_tn(kd_ref[b, :, lo:hi], u)
            s_ref[b, h] = s_new
            on = o * lax.rsqrt(jnp.mean(o * o, axis=-1, keepdims=True) + EPS) * og
            outs.append(on * _silu(z_ref[b, :, lo:hi]))
        o_ref[b] = jnp.concatenate(outs, axis=-1)
    sn_ref[...] = s_ref[...]


def _delta_scan(ub, wm, qd, kd, qk, dec, z3, s0, o_gain, chunk):
    n_seq, length, hv = ub.shape
    c = chunk
    nc = length // c
    nb = SEQ_BLOCK_SCAN
    dec4 = dec.reshape(n_seq, nc, SUBLANES, hv)
    tok = lambda width: pl.BlockSpec((nb, c, width), lambda b, j: (b, j, 0))
    st = pl.BlockSpec((nb, H_A, DK_A, DV_A), lambda b, j: (b, 0, 0, 0))
    return pl.pallas_call(
        functools.partial(_delta_scan_kernel, nb=nb, chunk=c),
        grid=(n_seq // nb, nc),
        in_specs=[tok(hv), tok(hv), tok(hv), tok(hv), tok(H_A * c),
                  pl.BlockSpec((nb, 1, SUBLANES, hv), lambda b, j: (b, j, 0, 0)),
                  tok(hv), st, pl.BlockSpec((1, DV_A), lambda b, j: (0, 0))],
        out_specs=[tok(hv), st],
        out_shape=[jax.ShapeDtypeStruct((n_seq, length, hv), F32),
                   jax.ShapeDtypeStruct((n_seq, H_A, DK_A, DV_A), F32)],
        scratch_shapes=[pltpu.VMEM((nb, H_A, DK_A, DV_A), F32)],
        compiler_params=_cparams(("arbitrary", "arbitrary"), VMEM_LIMIT),
        name="delta_scan",
    )(ub, wm, qd, kd, qk, dec4, z3, s0, o_gain.reshape(1, DV_A))


def _even_out_kernel(gcx_ref, halo_ref, hist_ref, oa_ref, x_ref, g1_ref, cw_ref, w_ref,
                     xo_ref, tail_ref, ext_ref, *, sb, lb, nl):
    step = pl.program_id(0)
    gcx = gcx_ref[...]
    b_gate = gcx[:, :, 0:D_B]
    bx = gcx[:, :, D_B:2 * D_B] * gcx[:, :, 2 * D_B:3 * D_B]
    if nl == 1:
        prev = hist_ref[...]
    else:
        hb = halo_ref[...]
        first = (step % nl) == 0
        prev = jnp.where(first, hist_ref[...], hb[:, :, D_B:2 * D_B] * hb[:, :, 2 * D_B:3 * D_B])
    ext_ref[:, 0:SUBLANES, :] = prev
    ext_ref[:, SUBLANES:SUBLANES + lb, :] = bx
    cw = cw_ref[...]
    conv = ext_ref[:, pl.ds(SUBLANES - (CONV_B - 1), lb), :] * cw[0:1, :]
    for j in range(1, CONV_B):
        conv = conv + ext_ref[:, pl.ds(SUBLANES - (CONV_B - 1) + j, lb), :] * cw[j:j + 1, :]
    out_b = (b_gate * conv).reshape(sb * lb, D_B)
    hv = H_A * DV_A
    y = _bdot(oa_ref[...], w_ref[0:hv, :]) + _bdot(out_b, w_ref[hv:hv + D_B, :])
    d = y.shape[-1]
    xo_ref[...] = x_ref[...] + g1_ref[...] * y.reshape(sb, lb, d)
    tail_ref[...] = bx[:, lb - SUBLANES:lb, :]


def _even_out(gcx3, hist8, out_a2, x3, mod3, k_gate, conv_w, w_out_bf16):
    n_seq, length, d = x3.shape
    sb, lb, steps, nl = _tok_tiling(n_seq, length, TOK_TILE)
    rows = sb * lb
    w3 = gcx3.shape[-1]
    hv = H_A * DV_A
    halo_map = lambda i: (i // nl, jnp.maximum((i % nl) * (lb // SUBLANES) - 1, 0), 0)
    return pl.pallas_call(
        functools.partial(_even_out_kernel, sb=sb, lb=lb, nl=nl),
        grid=(steps,),
        in_specs=[pl.BlockSpec((sb, lb, w3), lambda i: (i // nl, i % nl, 0)),
                  pl.BlockSpec((sb, SUBLANES, w3), halo_map if nl > 1 else (lambda i: (i, 0, 0))),
                  pl.BlockSpec((sb, SUBLANES, D_B), lambda i: (i // nl, 0, 0)),
                  pl.BlockSpec((rows, hv), lambda i: (i, 0)),
                  pl.BlockSpec((sb, lb, d), lambda i: (i // nl, i % nl, 0)),
                  pl.BlockSpec((sb, 1, d), lambda i: (i // nl, 0, k_gate)),
                  pl.BlockSpec((CONV_B, D_B), lambda i: (0, 0)),
                  pl.BlockSpec((hv + D_B, d), lambda i: (0, 0))],
        out_specs=[pl.BlockSpec((sb, lb, d), lambda i: (i // nl, i % nl, 0)),
                   pl.BlockSpec((sb, SUBLANES, D_B), lambda i: (i // nl, 0, 0))],
        out_shape=[jax.ShapeDtypeStruct((n_seq, length, d), F32),
                   jax.ShapeDtypeStruct((n_seq, SUBLANES, D_B), F32)],
        scratch_shapes=[pltpu.VMEM((sb, SUBLANES + lb, D_B), F32)],
        compiler_params=_cparams(("arbitrary",), VMEM_LIMIT),
        name="even_out",
    )(gcx3, gcx3, hist8, out_a2, x3, mod3, conv_w, w_out_bf16)


def _proj_res_kernel(a_ref, x_ref, g_ref, w_ref, xo_ref, *, sb, lb):
    y = _bdot(a_ref[...], w_ref[...])
    xo_ref[...] = x_ref[...] + g_ref[...] * y.reshape(sb, lb, y.shape[-1])


def _proj_residual(a2, x3, mod3, k_gate, w_bf16):
    n_seq, length, d = x3.shape
    sb, lb, steps, nl = _tok_tiling(n_seq, length, TOK_TILE)
    rows = sb * lb
    ka = a2.shape[-1]
    return pl.pallas_call(
        functools.partial(_proj_res_kernel, sb=sb, lb=lb),
        grid=(steps,),
        in_specs=[pl.BlockSpec((rows, ka), lambda i: (i, 0)),
                  pl.BlockSpec((sb, lb, d), lambda i: (i // nl, i % nl, 0)),
                  pl.BlockSpec((sb, 1, d), lambda i: (i // nl, 0, k_gate)),
                  pl.BlockSpec((ka, d), lambda i: (0, 0))],
        out_specs=pl.BlockSpec((sb, lb, d), lambda i: (i // nl, i % nl, 0)),
        out_shape=jax.ShapeDtypeStruct((n_seq, length, d), F32),
        compiler_params=_cparams(("arbitrary",), VMEM_LIMIT),
        name="proj_residual",
    )(a2, x3, mod3, w_bf16)


def _router_kernel(x_ref, sh_ref, sc_ref, gain_ref, wrt_ref, bcol_ref, utri_ref,
                   mi_ref, gcol_ref, cnt_ref, carry_ref):
    i = pl.program_id(0)

    @pl.when(i == 0)
    def _():
        carry_ref[...] = jnp.zeros_like(carry_ref)

    h2 = _norm_mod(x_ref[...], gain_ref[...], sc_ref[...], sh_ref[...])
    tm = h2.shape[0]
    logits = _bdot_nt(wrt_ref[...], h2) + bcol_ref[...]
    lg = logits[0:SUBLANES]
    e = jnp.exp(lg - jnp.max(lg, axis=0, keepdims=True))
    pg = e / jnp.sum(e, axis=0, keepdims=True)
    p_top = jnp.max(pg, axis=0, keepdims=True)
    rid8 = lax.broadcasted_iota(I32, (SUBLANES, tm), 0)
    g_top = jnp.min(jnp.where(pg == p_top, rid8, SUBLANES), axis=0, keepdims=True)
    le = logits[SUBLANES:SUBLANES + N_EXPERTS]
    sel = jnp.zeros((E_PER_GROUP, tm), F32)
    for gi in range(N_GROUPS):
        sel = sel + jnp.where(g_top == gi, le[gi * E_PER_GROUP:(gi + 1) * E_PER_GROUP], 0.0)
    e2 = jnp.exp(sel - jnp.max(sel, axis=0, keepdims=True))
    p_in = e2 / jnp.sum(e2, axis=0, keepdims=True)
    w_a = jnp.max(p_in, axis=0, keepdims=True)
    i_a = jnp.min(jnp.where(p_in == w_a, rid8, SUBLANES), axis=0, keepdims=True)
    p_rest = jnp.where(rid8 == i_a, -1.0, p_in)
    w_b = jnp.max(p_rest, axis=0, keepdims=True)
    i_b = jnp.min(jnp.where(p_rest == w_b, rid8, SUBLANES), axis=0, keepdims=True)
    den = w_a + w_b
    gate1 = p_top * (w_a / den)
    gate2 = p_top * (w_b / den)
    ex1 = g_top * E_PER_GROUP + i_a
    ex2 = g_top * E_PER_GROUP + i_b

    rid32 = lax.broadcasted_iota(I32, (N_EXPERTS, tm), 0)
    oh1 = rid32 == ex1
    oh2 = rid32 == ex2
    ohc = jnp.where(oh1, 1.0, jnp.where(oh2, 1.0, 0.0))
    cum = _bdot(ohc, utri_ref[...])
    carry = carry_ref[...]
    base = cum - ohc + carry[:, 0:1]
    r1 = jnp.sum(jnp.where(oh1, base, 0.0), axis=0, keepdims=True)
    r2 = jnp.sum(jnp.where(oh2, base, 0.0), axis=0, keepdims=True)
    new_carry = carry + cum[:, tm - 1:tm]
    carry_ref[...] = new_carry
    cnt_ref[...] = new_carry

    bro = lambda v: jnp.broadcast_to(v, (SUBLANES, tm))
    mi_ref[...] = jnp.where(rid8 == 0, bro(ex1),
                            jnp.where(rid8 == 1, bro(ex2),
                                      jnp.where(rid8 == 2, bro(r1.astype(I32)),
                                                jnp.where(rid8 == 3, bro(r2.astype(I32)), 0))))
    rid128 = lax.broadcasted_iota(I32, (LANES, tm), 0)
    g128 = jnp.where(rid128 == 0, jnp.broadcast_to(gate1, (LANES, tm)),
                     jnp.where(rid128 == 1, jnp.broadcast_to(gate2, (LANES, tm)), 0.0))
    gcol_ref[...] = g128.T


def _moe_router(x3, mod3, k_sh, k_sc, gain, wrt_bf16, bcol):
    n_seq, length, d = x3.shape
    sb, lb, steps, nl = _tok_tiling(n_seq, length, TOK_TILE)
    tm = sb * lb
    t = n_seq * length
    utri = jnp.triu(jnp.ones((tm, tm), F32)).astype(BF16)
    r_rows = wrt_bf16.shape[0]
    return pl.pallas_call(
        _router_kernel,
        grid=(steps,),
        in_specs=[pl.BlockSpec((sb, lb, d), lambda i: (i // nl, i % nl, 0)),
                  pl.BlockSpec((sb, 1, d), lambda i: (i // nl, 0, k_sh)),
                  pl.BlockSpec((sb, 1, d), lambda i: (i // nl, 0, k_sc)),
                  pl.BlockSpec((1, d), lambda i: (0, 0)),
                  pl.BlockSpec((r_rows, d), lambda i: (0, 0)),
                  pl.BlockSpec((r_rows, 1), lambda i: (0, 0)),
                  pl.BlockSpec((tm, tm), lambda i: (0, 0))],
        out_specs=[pl.BlockSpec((SUBLANES, tm), lambda i: (0, i)),
                   pl.BlockSpec((tm, LANES), lambda i: (i, 0)),
                   pl.BlockSpec((N_EXPERTS, LANES), lambda i: (0, 0))],
        out_shape=[jax.ShapeDtypeStruct((SUBLANES, t), I32),
                   jax.ShapeDtypeStruct((t, LANES), F32),
                   jax.ShapeDtypeStruct((N_EXPERTS, LANES), F32)],
        scratch_shapes=[pltpu.VMEM((N_EXPERTS, LANES), F32)],
        compiler_params=_cparams(("arbitrary",), VMEM_LIMIT),
        name="moe_router",
    )(x3, mod3, mod3, gain.reshape(1, d), wrt_bf16, bcol, utri)


def _plan_kernel(mi_ref, cnt_ref, ltri_ref, pos_ref, tinfo_ref, *, n_tiles_pad):
    cnt = cnt_ref[...]
    nt = jnp.floor((cnt + (MOE_ROW_TILE - 1)) * (1.0 / MOE_ROW_TILE))
    tstart = _bdot(ltri_ref[...], nt)
    mi = mi_ref[...]
    tm = mi.shape[1]
    rid32 = lax.broadcasted_iota(I32, (N_EXPERTS, tm), 0)
    ts_col = tstart[:, 0:1]

    def pos_of(ex, rank):
        start = jnp.sum(jnp.where(rid32 == ex, ts_col, 0.0), axis=0, keepdims=True)
        return start.astype(I32) * MOE_ROW_TILE + rank

    p1 = pos_of(mi[0:1], mi[2:3])
    p2 = pos_of(mi[1:2], mi[3:4])
    rid8 = lax.broadcasted_iota(I32, (SUBLANES, tm), 0)
    pos_ref[0] = jnp.where(rid8 == 0, jnp.broadcast_to(p1, (SUBLANES, tm)),
                           jnp.where(rid8 == 1, jnp.broadcast_to(p2, (SUBLANES, tm)), 0))
    tend_col = ts_col + nt[:, 0:1]
    jt = lax.broadcasted_iota(I32, (N_EXPERTS, n_tiles_pad), 1).astype(F32)
    te = jnp.sum(jnp.where(tend_col <= jt, 1.0, 0.0), axis=0, keepdims=True)
    te = jnp.minimum(te, N_EXPERTS - 1.0).astype(I32)
    total = jnp.sum(nt[:, 0:1], axis=0, keepdims=True).astype(I32)
    rid8t = lax.broadcasted_iota(I32, (SUBLANES, n_tiles_pad), 0)
    tinfo_ref[...] = jnp.where(rid8t == 0, jnp.broadcast_to(te, (SUBLANES, n_tiles_pad)),
                               jnp.broadcast_to(total, (SUBLANES, n_tiles_pad)))


def _moe_plan(meta_i, counts, n_tiles_pad):
    t = meta_i.shape[1]
    tm = MOE_TOK_TILE
    steps = t // tm
    ltri = jnp.tril(jnp.ones((N_EXPERTS, N_EXPERTS), F32), k=-1).astype(BF16)
    return pl.pallas_call(
        functools.partial(_plan_kernel, n_tiles_pad=n_tiles_pad),
        grid=(steps,),
        in_specs=[pl.BlockSpec((SUBLANES, tm), lambda i: (0, i)),
                  pl.BlockSpec((N_EXPERTS, LANES), lambda i: (0, 0)),
                  pl.BlockSpec((N_EXPERTS, N_EXPERTS), lambda i: (0, 0))],
        out_specs=[pl.BlockSpec((1, SUBLANES, tm), lambda i: (i, 0, 0)),
                   pl.BlockSpec((SUBLANES, n_tiles_pad), lambda i: (0, 0))],
        out_shape=[jax.ShapeDtypeStruct((steps, SUBLANES, tm), I32),
                   jax.ShapeDtypeStruct((SUBLANES, n_tiles_pad), I32)],
        compiler_params=_cparams(("arbitrary",)),
        name="moe_plan",
    )(meta_i, counts, ltri)


def _row_copy_out(buf_ref, slot, r, dst_hbm, p, sem):
    return pltpu.make_async_copy(buf_ref.at[slot, pl.ds(r, 1), :], dst_hbm.at[pl.ds(p, 1), :], sem.at[slot])


def _dispatch_kernel(x_ref, sh_ref, sc_ref, gain_ref, pos_ref, xs_in_ref, xs_ref,
                     buf_ref, idx_ref, sem_ref, isem_ref, *, tm, steps):
    del xs_in_ref
    i = pl.program_id(0)
    slot = i % 2

    def drain(s):
        def body(r, carry):
            _row_copy_out(buf_ref, s, 0, xs_ref, 0, sem_ref).wait()
            return carry
        lax.fori_loop(0, 2 * tm, body, 0)

    @pl.when(i >= 2)
    def _():
        drain(slot)

    buf_ref[slot] = _norm_mod(x_ref[...], gain_ref[...], sc_ref[...], sh_ref[...])
    cp = pltpu.make_async_copy(pos_ref.at[0], idx_ref, isem_ref)
    cp.start()
    cp.wait()

    def issue(r, carry):
        _row_copy_out(buf_ref, slot, r, xs_ref, idx_ref[0, r], sem_ref).start()
        _row_copy_out(buf_ref, slot, r, xs_ref, idx_ref[1, r], sem_ref).start()
        return carry
    lax.fori_loop(0, tm, issue, 0, unroll=8)

    @pl.when(i == steps - 1)
    def _():
        drain(slot)
        if steps > 1:
            drain(1 - slot)


def _moe_dispatch(x3, mod3, k_sh, k_sc, gain, pos3, p_rows):
    n_seq, length, d = x3.shape
    sb, lb, steps, nl = _tok_tiling(n_seq, length, MOE_TOK_TILE)
    tm = sb * lb
    xs0 = jnp.zeros((p_rows, d), F32)
    return pl.pallas_call(
        functools.partial(_dispatch_kernel, tm=tm, steps=steps),
        grid=(steps,),
        in_specs=[pl.BlockSpec((sb, lb, d), lambda i: (i // nl, i % nl, 0)),
                  pl.BlockSpec((sb, 1, d), lambda i: (i // nl, 0, k_sh)),
                  pl.BlockSpec((sb, 1, d), lambda i: (i // nl, 0, k_sc)),
                  pl.BlockSpec((1, d), lambda i: (0, 0)),
                  pl.BlockSpec((1, SUBLANES, tm), lambda i: (i, 0, 0)),
                  pl.BlockSpec(memory_space=pl.ANY)],
        out_specs=pl.BlockSpec(memory_space=pl.ANY),
        out_shape=jax.ShapeDtypeStruct((p_rows, d), F32),
        scratch_shapes=[pltpu.VMEM((2, tm, d), F32),
                        pltpu.SMEM((SUBLANES, tm), I32),
                        pltpu.SemaphoreType.DMA((2,)),
                        pltpu.SemaphoreType.DMA(())],
        input_output_aliases={5: 0},
        compiler_params=_cparams(("arbitrary",), VMEM_LIMIT),
        name="moe_dispatch",
    )(x3, mod3, mod3, gain.reshape(1, d), pos3, xs0)


def _expert_kernel(te_ref, nt_ref, xs_ref, wg_ref, wu_ref, wd_ref, ys_ref):
    i = pl.program_id(0)

    @pl.when(i < nt_ref[0])
    def _():
        x = xs_ref[...].astype(BF16)
        hid = _silu(jnp.dot(x, wg_ref[0], preferred_element_type=F32)) \
            * jnp.dot(x, wu_ref[0], preferred_element_type=F32)
        ys_ref[...] = jnp.dot(hid.astype(BF16), wd_ref[0], preferred_element_type=F32)

    @pl.when(i >= nt_ref[0])
    def _():
        ys_ref[...] = jnp.zeros_like(ys_ref)


def _moe_experts(xs, tile_expert, n_tiles, wg, wu, wd):
    p_rows, d = xs.shape
    f = wg.shape[-1]
    steps = p_rows // MOE_ROW_TILE
    grid_spec = pltpu.PrefetchScalarGridSpec(
        num_scalar_prefetch=2,
        grid=(steps,),
        in_specs=[pl.BlockSpec((MOE_ROW_TILE, d), lambda i, te, nt: (i, 0)),
                  pl.BlockSpec((1, d, f), lambda i, te, nt: (te[i], 0, 0)),
                  pl.BlockSpec((1, d, f), lambda i, te, nt: (te[i], 0, 0)),
                  pl.BlockSpec((1, f, d), lambda i, te, nt: (te[i], 0, 0))],
        out_specs=pl.BlockSpec((MOE_ROW_TILE, d), lambda i, te, nt: (i, 0)),
    )
    return pl.pallas_call(
        _expert_kernel,
        grid_spec=grid_spec,
        out_shape=jax.ShapeDtypeStruct((p_rows, d), F32),
        compiler_params=_cparams(("arbitrary",), VMEM_LIMIT),
        name="moe_experts",
    )(tile_expert, n_tiles, xs, wg, wu, wd)


def _row_copy_in(src_hbm, p, buf_ref, slot, which, r, sem):
    return pltpu.make_async_copy(src_hbm.at[pl.ds(p, 1), :], buf_ref.at[slot, which, pl.ds(r, 1), :],
                                 sem.at[slot])


def _combine_kernel(x_ref, g_ref, gcol_ref, pos_ref, posn_ref, ys_ref, gainf_ref, xo_ref,
                    buf_ref, idx_ref, sem_ref, isem_ref, *, tm, steps, sb, lb, final_norm):
    i = pl.program_id(0)
    slot = i % 2

    def issue(pref, s):
        cp = pltpu.make_async_copy(pref.at[0], idx_ref, isem_ref)
        cp.start()
        cp.wait()

        def body(r, carry):
            _row_copy_in(ys_ref, idx_ref[0, r], buf_ref, s, 0, r, sem_ref).start()
            _row_copy_in(ys_ref, idx_ref[1, r], buf_ref, s, 1, r, sem_ref).start()
            return carry
        lax.fori_loop(0, tm, body, 0, unroll=8)

    @pl.when(i == 0)
    def _():
        issue(pos_ref, slot)

    @pl.when(i + 1 < steps)
    def _():
        issue(posn_ref, 1 - slot)

    def wait_body(r, carry):
        _row_copy_in(ys_ref, 0, buf_ref, slot, 0, 0, sem_ref).wait()
        return carry
    lax.fori_loop(0, 2 * tm, wait_body, 0)

    gc = gcol_ref[...]
    y = gc[:, 0:1] * buf_ref[slot, 0] + gc[:, 1:2] * buf_ref[slot, 1]
    d = y.shape[-1]
    xn = x_ref[...] + g_ref[...] * y.reshape(sb, lb, d)
    if final_norm:
        xn = xn * lax.rsqrt(jnp.mean(xn * xn, axis=-1, keepdims=True) + EPS) * gainf_ref[...]
    xo_ref[...] = xn


def _moe_combine(x3, mod3, k_gate, gates_col, pos3, ys, gain_f, final_norm):
    n_seq, length, d = x3.shape
    sb, lb, steps, nl = _tok_tiling(n_seq, length, MOE_TOK_TILE)
    tm = sb * lb
    return pl.pallas_call(
        functools.partial(_combine_kernel, tm=tm, steps=steps, sb=sb, lb=lb, final_norm=final_norm),
        grid=(steps,),
        in_specs=[pl.BlockSpec((sb, lb, d), lambda i: (i // nl, i % nl, 0)),
                  pl.BlockSpec((sb, 1, d), lambda i: (i // nl, 0, k_gate)),
                  pl.BlockSpec((tm, LANES), lambda i: (i, 0)),
                  pl.BlockSpec((1, SUBLANES, tm), lambda i: (i, 0, 0)),
                  pl.BlockSpec((1, SUBLANES, tm), lambda i: (jnp.minimum(i + 1, steps - 1), 0, 0)),
                  pl.BlockSpec(memory_space=pl.ANY),
                  pl.BlockSpec((1, d), lambda i: (0, 0))],
        out_specs=pl.BlockSpec((sb, lb, d), lambda i: (i // nl, i % nl, 0)),
        out_shape=jax.ShapeDtypeStruct((n_seq, length, d), F32),
        scratch_shapes=[pltpu.VMEM((2, 2, tm, d), F32),
                        pltpu.SMEM((SUBLANES, tm), I32),
                        pltpu.SemaphoreType.DMA((2,)),
                        pltpu.SemaphoreType.DMA(())],
        compiler_params=_cparams(("arbitrary",), VMEM_LIMIT),
        name="moe_combine",
    )(x3, mod3, gates_col, pos3, pos3, ys, gain_f.reshape(1, d))


def _hier_moe_residual(x3, mod3, k_sh, k_sc, k_gate, gain, wrt_bf16, bcol, wg, wu, wd, gain_f, final_norm):
    n_seq, length, d = x3.shape
    t = n_seq * length
    max_tiles = (2 * t) // MOE_ROW_TILE + N_EXPERTS
    n_tiles_pad = -(-max_tiles // LANES) * LANES
    meta_i, gates_col, counts = _moe_router(x3, mod3, k_sh, k_sc, gain, wrt_bf16, bcol)
    pos3, tinfo = _moe_plan(meta_i, counts, n_tiles_pad)
    xs = _moe_dispatch(x3, mod3, k_sh, k_sc, gain, pos3, max_tiles * MOE_ROW_TILE)
    ys = _moe_experts(xs, tinfo[0, :max_tiles], tinfo[1, :1], wg, wu, wd)
    return _moe_combine(x3, mod3, k_gate, gates_col, pos3, ys, gain_f, final_norm)


def _multiplicity(dist):
    m = jnp.zeros(dist.shape, F32)
    for window, dil in BRANCHES:
        m = m + ((dist >= 0) & (dist <= window) & (dist % dil == 0)).astype(F32)
    return m


def _log_mult(dist):
    m = _multiplicity(dist)
    return jnp.where(m > 0, jnp.log(jnp.maximum(m, 1.0)), NEG_BIG)


def _attn_prompt_kernel(q_ref, k_ref, v_ref, bias_ref, o_ref, *, length):
    blk = ATT_BLK
    nq = length // blk
    lane = lax.broadcasted_iota(I32, (blk, LANES), 1)
    head0 = lane < HD_C
    scale = HD_C ** -0.5
    for qi in range(nq):
        q = q_ref[0, qi * blk:(qi + 1) * blk, :] * scale
        q0 = jnp.where(head0, q, 0.0).astype(BF16)
        q1 = jnp.where(head0, 0.0, q).astype(BF16)

        def body(kj, carry):
            m0, l0, m1, l1, acc = carry
            ks = pl.multiple_of(kj * blk, blk)
            k = k_ref[0, pl.ds(ks, blk), :].astype(BF16)
            v = v_ref[0, pl.ds(ks, blk), :].astype(BF16)
            bias = bias_ref[qi - kj]
            outs = []
            stats = []
            for qh, m_old, l_old in ((q0, m0, l0), (q1, m1, l1)):
                s = lax.dot_general(qh, k, (((1,), (1,)), ((), ())), preferred_element_type=F32) + bias
                m_new = jnp.maximum(m_old, jnp.max(s, axis=-1, keepdims=True))
                alpha = jnp.exp(m_old - m_new)
                p = jnp.exp(s - m_new)
                l_new = l_old * alpha + jnp.sum(p, axis=-1, keepdims=True)
                outs.append((alpha, jnp.dot(p.astype(BF16), v, preferred_element_type=F32)))
                stats.append((m_new, l_new))
            acc = jnp.where(head0, acc * outs[0][0] + outs[0][1], acc * outs[1][0] + outs[1][1])
            return stats[0][0], stats[0][1], stats[1][0], stats[1][1], acc

        init = (jnp.full((blk, 1), NEG_BIG, F32), jnp.zeros((blk, 1), F32),
                jnp.full((blk, 1), NEG_BIG, F32), jnp.zeros((blk, 1), F32),
                jnp.zeros((blk, LANES), F32))
        m0, l0, m1, l1, acc = lax.fori_loop(0, qi + 1, body, init)
        o_ref[0, qi * blk:(qi + 1) * blk, :] = acc / jnp.where(head0, l0, l1)


def _attn_prompt(qkv3):
    n_seq, length, w3 = qkv3.shape
    d = w3 // 3
    pairs = d // LANES
    blk = ATT_BLK
    nq = length // blk
    a = jnp.arange(blk)
    dist = (jnp.arange(nq)[:, None, None] * blk) + a[None, :, None] - a[None, None, :]
    bias = _log_mult(dist)
    return pl.pallas_call(
        functools.partial(_attn_prompt_kernel, length=length),
        grid=(n_seq, pairs),
        in_specs=[pl.BlockSpec((1, length, LANES), lambda n, p: (n, 0, p)),
                  pl.BlockSpec((1, length, LANES), lambda n, p: (n, 0, pairs + p)),
                  pl.BlockSpec((1, length, LANES), lambda n, p: (n, 0, 2 * pairs + p)),
                  pl.BlockSpec((nq, blk, blk), lambda n, p: (0, 0, 0))],
        out_specs=pl.BlockSpec((1, length, LANES), lambda n, p: (n, 0, p)),
        out_shape=jax.ShapeDtypeStruct((n_seq, length, d), F32),
        compiler_params=_cparams(("arbitrary", "arbitrary"), VMEM_LIMIT),
        name="attn_prompt",
    )(qkv3, qkv3, qkv3, bias)


def _attn_sample_kernel(qt_ref, kn_ref, vn_ref, kc_ref, vc_ref, fb_ref, fbn_ref, hm_ref, o_ref,
                        m_ref, l_ref, acc_ref, *, n_chunks, t_len):
    j = pl.program_id(1)
    qt = (qt_ref[0] * (HD_C ** -0.5)).astype(BF16)
    rows_q = t_len * H_C

    def fold(k2d, v2d, bias):
        s = lax.dot_general(qt, k2d.astype(BF16), (((1,), (1,)), ((), ())),
                            preferred_element_type=F32) + bias
        m_old = m_ref[...]
        m_new = jnp.maximum(m_old, jnp.max(s, axis=-1, keepdims=True))
        alpha = jnp.exp(m_old - m_new)
        p = jnp.exp(s - m_new)
        l_ref[...] = l_ref[...] * alpha + jnp.sum(p, axis=-1, keepdims=True)
        acc_ref[...] = acc_ref[...] * alpha + jnp.dot(p.astype(BF16), v2d.astype(BF16),
                                                      preferred_element_type=F32)
        m_ref[...] = m_new

    @pl.when(j == 0)
    def _():
        m_ref[...] = jnp.full(m_ref.shape, NEG_BIG, F32)
        l_ref[...] = jnp.zeros(l_ref.shape, F32)
        acc_ref[...] = jnp.zeros(acc_ref.shape, F32)
        hm_new = hm_ref[:, 0:t_len * H_C]
        fold(kn_ref[0].reshape(t_len * H_C, HD_C), vn_ref[0].reshape(t_len * H_C, HD_C),
             jnp.repeat(fbn_ref[...], H_C, axis=0) + hm_new)

    sub = SAMPLE_KV_SUB
    for c in range(SAMPLE_KV_CHUNK // sub):
        k2d = kc_ref[0, c * sub:(c + 1) * sub].reshape(sub * H_C, HD_C)
        v2d = vc_ref[0, c * sub:(c + 1) * sub].reshape(sub * H_C, HD_C)
        fb = fb_ref[0, :, c * sub * H_C:(c + 1) * sub * H_C]
        bias = jnp.repeat(fb, H_C, axis=0) + hm_ref[...]
        fold(k2d, v2d, bias)

    @pl.when(j == n_chunks - 1)
    def _():
        o_ref[0] = acc_ref[...] / l_ref[...]


def _attn_sample(q3, k_new3, v_new3, k_cache, v_cache):
    n_seq, t_len, d = q3.shape
    w_buf = k_cache.shape[1]
    rows_q = t_len * H_C
    qt = q3.reshape(n_seq, rows_q, HD_C)
    kn = k_new3.reshape(n_seq, t_len, H_C, HD_C)
    vn = v_new3.reshape(n_seq, t_len, H_C, HD_C)
    ch = SAMPLE_KV_CHUNK
    n_chunks = w_buf // ch
    tpos = w_buf + jnp.arange(t_len)
    dist_c = tpos[:, None] - jnp.arange(w_buf)[None, :]
    fb = jnp.repeat(_log_mult(dist_c), H_C, axis=1)
    fb = fb.reshape(t_len, n_chunks, ch * H_C).transpose(1, 0, 2)
    dist_n = jnp.arange(t_len)[:, None] - jnp.arange(t_len)[None, :]
    fbn = jnp.repeat(_log_mult(dist_n), H_C, axis=1)
    sub_l = SAMPLE_KV_SUB * H_C
    hq = jnp.arange(rows_q) % H_C
    hk = jnp.arange(sub_l) % H_C
    hm = jnp.where(hq[:, None] == hk[None, :], 0.0, NEG_BIG).astype(F32)
    out = pl.pallas_call(
        functools.partial(_attn_sample_kernel, n_chunks=n_chunks, t_len=t_len),
        grid=(n_seq, n_chunks),
        in_specs=[pl.BlockSpec((1, rows_q, HD_C), lambda n, j: (n, 0, 0)),
                  pl.BlockSpec((1, t_len, H_C, HD_C), lambda n, j: (n, 0, 0, 0)),
                  pl.BlockSpec((1, t_len, H_C, HD_C), lambda n, j: (n, 0, 0, 0)),
                  pl.BlockSpec((1, ch, H_C, HD_C), lambda n, j: (n, j, 0, 0)),
                  pl.BlockSpec((1, ch, H_C, HD_C), lambda n, j: (n, j, 0, 0)),
                  pl.BlockSpec((1, t_len, ch * H_C), lambda n, j: (j, 0, 0)),
                  pl.BlockSpec((t_len, t_len * H_C), lambda n, j: (0, 0)),
                  pl.BlockSpec((rows_q, sub_l), lambda n, j: (0, 0))],
        out_specs=pl.BlockSpec((1, rows_q, HD_C), lambda n, j: (n, 0, 0)),
        out_shape=jax.ShapeDtypeStruct((n_seq, rows_q, HD_C), F32),
        scratch_shapes=[pltpu.VMEM((rows_q, 1), F32), pltpu.VMEM((rows_q, 1), F32),
                        pltpu.VMEM((rows_q, HD_C), F32)],
        compiler_params=_cparams(("arbitrary", "arbitrary"), VMEM_LIMIT),
        name="attn_sample",
    )(qt, kn, vn, k_cache, v_cache, fb, fbn, hm)
    return out.reshape(n_seq, t_len, d)


def _pad_hist(hist, n_seq, width):
    k1 = hist.shape[1]
    return jnp.concatenate([jnp.zeros((n_seq, SUBLANES - k1, width), F32), hist.astype(F32)], axis=1)


def _prep_weights(p):
    d = p['w_in_even'].shape[1]
    out = {}
    w_in = p['w_in_even'][0]
    hv = H_A * DV_A
    o_z = QKV_A
    o_a = o_z + hv
    o_g = o_a + 2 * H_A
    ab_pad = jnp.zeros((d, LANES - 2 * H_A), F32)
    out['w_in'] = jnp.concatenate([w_in[:, :o_a], w_in[:, o_a:o_g], ab_pad, w_in[:, o_g:]], axis=1).astype(BF16)
    out['w_out_even'] = p['w_out_even'][0].astype(BF16)
    out['w_qkv'] = p['w_qkv_odd'][0].astype(BF16)
    out['w_out_odd'] = p['w_out_odd'][0].astype(BF16)
    depth = p['w_router_group'].shape[0]
    wrt, bcol, wg, wu, wd = [], [], [], [], []
    for l in range(depth):
        pad_w = jnp.zeros((SUBLANES - N_GROUPS, d), F32)
        wrt.append(jnp.concatenate([p['w_router_group'][l].T, pad_w, p['w_router_expert'][l].T], axis=0).astype(BF16))
        pad_b = jnp.full((SUBLANES - N_GROUPS,), NEG_BIG, F32)
        bcol.append(jnp.concatenate([p['b_router_group'][l], pad_b, p['b_router_expert'][l]]).reshape(-1, 1))
        f = p['w_exp_gate'].shape[-1]
        wg.append(p['w_exp_gate'][l].reshape(N_EXPERTS, d, f).astype(BF16))
        wu.append(p['w_exp_up'][l].reshape(N_EXPERTS, d, f).astype(BF16))
        wd.append(p['w_exp_down'][l].reshape(N_EXPERTS, f, d).astype(BF16))
    out.update(wrt=wrt, bcol=bcol, wg=wg, wu=wu, wd=wd)
    return out


def _trunk(x3, mod, a_ssm, a_conv, b_conv, kv_k, kv_v, p, w, sample):
    n_seq, length, d = x3.shape
    t = n_seq * length
    hv = H_A * DV_A
    mod3 = [m.reshape(n_seq, 1, 6 * d) for m in mod]

    qkv_pre, z, ab, gcx = _norm_mod_matmul(x3, mod3[0], 0, 1, p['norm1'][0], w['w_in'],
                                           (QKV_A, hv, LANES, 3 * D_B))
    chunk = min(CHUNK_PROMPT, length)
    g_chunks = 4
    qkv_pre3 = qkv_pre.reshape(n_seq, length, QKV_A)
    ub, wm, qd, kd, qk, dec = _delta_prep(qkv_pre3, _pad_hist(a_conv[0], n_seq, QKV_A),
                                          ab.reshape(n_seq, length, LANES), p['conv_a_w'][0],
                                          p['a_log'][0], p['dt_bias'][0], chunk, g_chunks)
    out_a, s_new = _delta_scan(ub, wm, qd, kd, qk, dec, z.reshape(n_seq, length, hv),
                               a_ssm[0].astype(F32), p['o_gain_a'][0], chunk)
    x3, bx_tail = _even_out(gcx.reshape(n_seq, length, 3 * D_B), _pad_hist(b_conv[0], n_seq, D_B),
                            out_a.reshape(t, hv), x3, mod3[0], 2, p['conv_b_w'][0], w['w_out_even'])
    new_aconv = jnp.concatenate([a_conv[0].astype(F32), qkv_pre3], axis=1)[:, -(CONV_A - 1):]
    new_bconv = jnp.concatenate([b_conv[0].astype(F32), bx_tail], axis=1)[:, -(CONV_B - 1):]
    x3 = _hier_moe_residual(x3, mod3[0], 3, 4, 5, p['norm2'][0], w['wrt'][0], w['bcol'][0],
                            w['wg'][0], w['wu'][0], w['wd'][0], p['norm_f'], False)

    (qkv,) = _norm_mod_matmul(x3, mod3[1], 0, 1, p['norm1'][1], w['w_qkv'], (3 * d,))
    qkv3 = qkv.reshape(n_seq, length, 3 * d)
    k_new = qkv3[:, :, d:2 * d]
    v_new = qkv3[:, :, 2 * d:]
    if sample:
        attn = _attn_sample(qkv3[:, :, :d], k_new, v_new, kv_k[0], kv_v[0])
        keep = length
    else:
        attn = _attn_prompt(qkv3)
        keep = min(BRANCHES[-1][0], length)
    new_k = k_new[:, length - keep:].reshape(n_seq, keep, H_C, HD_C)
    new_v = v_new[:, length - keep:].reshape(n_seq, keep, H_C, HD_C)
    x3 = _proj_residual(attn.reshape(t, d), x3, mod3[1], 2, w['w_out_odd'])
    y = _hier_moe_residual(x3, mod3[1], 3, 4, 5, p['norm2'][1], w['wrt'][1], w['bcol'][1],
                           w['wg'][1], w['wu'][1], w['wd'][1], p['norm_f'], True)
    return y, s_new[None], new_aconv[None], new_bconv[None], new_k[None], new_v[None]


def kernel(x_prompt, x_sample, state_a_ssm, state_a_conv, state_b_conv, cache_c_k, cache_c_v, c_prompt, c_sample, w_ada, b_ada, norm1, norm2, norm_f, w_in_even, conv_a_w, a_log, dt_bias, o_gain_a, conv_b_w, w_out_even, w_qkv_odd, w_out_odd, w_router_group, b_router_group, w_router_expert, b_router_expert, w_exp_gate, w_exp_up, w_exp_down):
    p = dict(norm1=norm1, norm2=norm2, norm_f=norm_f, w_in_even=w_in_even, conv_a_w=conv_a_w,
             a_log=a_log, dt_bias=dt_bias, o_gain_a=o_gain_a, conv_b_w=conv_b_w, w_out_even=w_out_even,
             w_qkv_odd=w_qkv_odd, w_out_odd=w_out_odd, w_router_group=w_router_group,
             b_router_group=b_router_group, w_router_expert=w_router_expert,
             b_router_expert=b_router_expert, w_exp_gate=w_exp_gate, w_exp_up=w_exp_up,
             w_exp_down=w_exp_down)
    w = _prep_weights(p)
    nb = x_prompt.shape[0]
    ns = x_sample.shape[0]
    mod_all = _ada_mod(jnp.concatenate([c_prompt, c_sample], axis=0), w_ada, b_ada)
    depth = w_ada.shape[0]
    mod_p = [mod_all[l, :nb] for l in range(depth)]
    mod_s = [mod_all[l, nb:] for l in range(depth)]
    n_even = state_a_ssm.shape[0]
    zero_ssm = jnp.zeros((n_even, nb, H_A, DK_A, DV_A), F32)
    zero_aconv = jnp.zeros((n_even, nb, CONV_A - 1, QKV_A), F32)
    zero_bconv = jnp.zeros((n_even, nb, CONV_B - 1, D_B), F32)
    y_p, ssm_p, aconv_p, bconv_p, k_p, v_p = _trunk(x_prompt, mod_p, zero_ssm, zero_aconv, zero_bconv,
                                                    None, None, p, w, False)
    y_s, ssm_s, aconv_s, bconv_s, k_s, v_s = _trunk(x_sample, mod_s, state_a_ssm, state_a_conv,
                                                    state_b_conv, cache_c_k, cache_c_v, p, w, True)
    return (y_p, y_s, ssm_p, ssm_s, aconv_p, aconv_s, bconv_p, bconv_s, k_p, k_s, v_p, v_s)
```

```python
import functools
import math

import jax
import jax.numpy as jnp
from jax import lax
from jax.experimental import pallas as pl
from jax.experimental.pallas import tpu as pltpu

F32 = jnp.float32
BF16 = jnp.bfloat16
I32 = jnp.int32

EPS = 1e-6
NEG_BIG = -1e30

LANES = 128
SUBLANES = 8
VMEM_BYTES_V7X = 64 * 1024 * 1024
VMEM_LIMIT = 56 * 1024 * 1024

H_A, DK_A, DV_A = 4, 128, 128
CONV_A, CONV_B = 4, 3
QKV_A = H_A * (2 * DK_A + DV_A)
D_B = 512
H_C, HD_C = 16, 64
BRANCHES = ((128, 1), (512, 4), (2048, 16))
N_GROUPS, E_PER_GROUP = 4, 8
N_EXPERTS = N_GROUPS * E_PER_GROUP
CHUNK_PROMPT = 64

TOK_TILE = 512
MOE_TOK_TILE = 256
MOE_ROW_TILE = 256
ATT_BLK = 256
SEQ_BLOCK_SCAN = 8
SAMPLE_HEADS_PER_STEP = 8


def _cparams(sem, vmem=None):
    return pltpu.CompilerParams(dimension_semantics=sem, vmem_limit_bytes=vmem)


def _silu(x):
    return x * jax.nn.sigmoid(x)


def _bdot(a, b):
    return jnp.dot(a.astype(BF16), b.astype(BF16), preferred_element_type=F32)


def _bdot_nt(a, b):
    return lax.dot_general(a.astype(BF16), b.astype(BF16), (((1,), (1,)), ((), ())),
                           preferred_element_type=F32)


def _bdot_tn(a, b):
    return lax.dot_general(a.astype(BF16), b.astype(BF16), (((0,), (0,)), ((), ())),
                           preferred_element_type=F32)


def _split_bf16(x):
    hi = x.astype(BF16)
    return hi, (x - hi.astype(F32)).astype(BF16)


def _dot3(a, b):
    ah, al = _split_bf16(a)
    bh, bl = _split_bf16(b)
    dot = functools.partial(jnp.dot, preferred_element_type=F32)
    return dot(ah, bh) + (dot(ah, bl) + dot(al, bh))


def _fdot(a, b):
    return jnp.dot(a, b, preferred_element_type=F32, precision=lax.Precision.HIGHEST)


def _fdot_nt(a, b):
    return lax.dot_general(a, b, (((1,), (1,)), ((), ())), preferred_element_type=F32,
                           precision=lax.Precision.HIGHEST)


def _tok_tiling(n_seq, length, tile):
    if length >= tile:
        sb, lb = 1, tile
    else:
        sb, lb = tile // length, length
    nl = length // lb
    return sb, lb, (n_seq // sb) * nl, nl


def _norm_mod(x, gain, sc, sh):
    sb, lb, d = x.shape
    y = x * lax.rsqrt(jnp.mean(x * x, axis=-1, keepdims=True) + EPS) * gain
    h = y * (1.0 + sc) + sh
    return h.reshape(sb * lb, d)


def _ada_kernel(c_ref, w_ref, b_ref, o_ref):
    c = _silu(c_ref[...])
    o_ref[0] = _bdot(c, w_ref[0]) + b_ref[0]


def _ada_mod(c_all, w_ada, b_ada):
    r, d = c_all.shape
    depth, _, n6 = w_ada.shape
    tn = 1536
    return pl.pallas_call(
        _ada_kernel,
        grid=(depth, n6 // tn),
        in_specs=[pl.BlockSpec((r, d), lambda l, j: (0, 0)),
                  pl.BlockSpec((1, d, tn), lambda l, j: (l, 0, j)),
                  pl.BlockSpec((1, 1, tn), lambda l, j: (l, 0, j))],
        out_specs=pl.BlockSpec((1, r, tn), lambda l, j: (l, 0, j)),
        out_shape=jax.ShapeDtypeStruct((depth, r, n6), F32),
        compiler_params=_cparams(("arbitrary", "arbitrary"), VMEM_LIMIT),
        name="ada_mod",
    )(c_all, w_ada, b_ada.reshape(depth, 1, n6))


def _nmm_kernel(x_ref, sh_ref, sc_ref, gain_ref, w_ref, *o_refs, splits):
    h = _norm_mod(x_ref[...], gain_ref[...], sc_ref[...], sh_ref[...]).astype(BF16)
    off = 0
    for o_ref, n in zip(o_refs, splits):
        o_ref[...] = jnp.dot(h, w_ref[:, off:off + n], preferred_element_type=F32)
        off += n


def _norm_mod_matmul(x3, mod3, k_sh, k_sc, gain, w_bf16, splits):
    n_seq, length, d = x3.shape
    sb, lb, steps, nl = _tok_tiling(n_seq, length, TOK_TILE)
    rows = sb * lb
    t = n_seq * length
    n_out = w_bf16.shape[1]
    assert sum(splits) == n_out
    return pl.pallas_call(
        functools.partial(_nmm_kernel, splits=splits),
        grid=(steps,),
        in_specs=[pl.BlockSpec((sb, lb, d), lambda i: (i // nl, i % nl, 0)),
                  pl.BlockSpec((sb, 1, d), lambda i: (i // nl, 0, k_sh)),
                  pl.BlockSpec((sb, 1, d), lambda i: (i // nl, 0, k_sc)),
                  pl.BlockSpec((1, d), lambda i: (0, 0)),
                  pl.BlockSpec((d, n_out), lambda i: (0, 0))],
        out_specs=[pl.BlockSpec((rows, n), lambda i: (i, 0)) for n in splits],
        out_shape=[jax.ShapeDtypeStruct((t, n), F32) for n in splits],
        compiler_params=_cparams(("arbitrary",), VMEM_LIMIT),
        name="norm_mod_matmul",
    )(x3, mod3, mod3, gain.reshape(1, d), w_bf16)


def _delta_prep_kernel(qkv_ref, halo_ref, hist_ref, ab_ref, cw_ref, al_ref, dtb_ref, lt_ref, sel_ref,
                       ub_ref, wm_ref, qd_ref, kd_ref, qk_ref, dec_ref, ext_ref,
                       *, g_chunks, chunk, chunks_are_seqs, chunks_per_seq):
    c = chunk
    step = pl.program_id(0)
    cw = cw_ref[...]
    row = lax.broadcasted_iota(I32, (c, c), 0)
    col = lax.broadcasted_iota(I32, (c, c), 1)
    lower_incl = row >= col
    strict = row > col
    eye = (row == col).astype(F32)
    n_fac = max(int(math.ceil(math.log2(c))), 1)

    for g in range(g_chunks):
        cur = qkv_ref[g] if chunks_are_seqs else qkv_ref[0, g * c:(g + 1) * c, :]
        if chunks_are_seqs:
            prev = hist_ref[g]
        elif g == 0:
            first = (step % (chunks_per_seq // g_chunks)) == 0
            prev = jnp.where(first, hist_ref[0], halo_ref[0])
        else:
            prev = qkv_ref[0, g * c - SUBLANES:g * c, :]
        ext_ref[0:SUBLANES, :] = prev
        ext_ref[SUBLANES:SUBLANES + c, :] = cur
        conv = ext_ref[pl.ds(SUBLANES - (CONV_A - 1), c), :] * cw[0:1, :]
        for j in range(1, CONV_A):
            conv = conv + ext_ref[pl.ds(SUBLANES - (CONV_A - 1) + j, c), :] * cw[j:j + 1, :]
        qkv = _silu(conv)

        ab = ab_ref[g] if chunks_are_seqs else ab_ref[0, g * c:(g + 1) * c, :]
        g_all = -jnp.exp(al_ref[...]) * jax.nn.softplus(ab + dtb_ref[...])
        sig = jax.nn.sigmoid(ab)
        gcum_all = _fdot(lt_ref[...], g_all)
        gcum_rows = _fdot_nt(sel_ref[...], gcum_all)

        ub, wm, qd, kd, qkm, dec = [], [], [], [], [], []
        for h in range(H_A):
            q = qkv[:, h * DK_A:(h + 1) * DK_A]
            k = qkv[:, H_A * DK_A + h * DK_A:H_A * DK_A + (h + 1) * DK_A]
            v = qkv[:, 2 * H_A * DK_A + h * DV_A:2 * H_A * DK_A + (h + 1) * DV_A]
            q = q * lax.rsqrt(jnp.sum(q * q, axis=-1, keepdims=True) + EPS) * (DK_A ** -0.5)
            k = k * lax.rsqrt(jnp.sum(k * k, axis=-1, keepdims=True) + EPS)
            beta = sig[:, H_A + h:H_A + h + 1]
            gc = gcum_all[:, h:h + 1]
            gr = gcum_rows[h:h + 1, :]
            gamma = jnp.exp(jnp.where(lower_incl, gc - gr, NEG_BIG))
            kk = _bdot_nt(k, k)
            a_mat = jnp.where(strict, beta * kk * gamma, 0.0)
            m_pow = -a_mat
            t_inv = eye + m_pow
            for _ in range(n_fac - 1):
                m_pow = _bdot(m_pow, m_pow)
                t_inv = t_inv + _bdot(t_inv, m_pow)
            resid = eye - t_inv - _dot3(a_mat, t_inv)
            t_inv = t_inv + _bdot(t_inv, resid)
            eg = jnp.exp(gc)
            rhs = jnp.concatenate([v * beta, k * (beta * eg)], axis=-1)
            sol = rhs + _dot3(t_inv - eye, rhs)
            ub.append(sol[:, :DV_A])
            wm.append(sol[:, DV_A:])
            qkm.append(_bdot_nt(q, k) * gamma)
            qd.append(q * eg)
            g_last = gcum_all[c - 1:c, h:h + 1]
            kd.append(k * jnp.exp(g_last - gc))
            dec.append(jnp.broadcast_to(jnp.exp(g_last), (SUBLANES, DV_A)))

        def put(ref, parts):
            val = jnp.concatenate(parts, axis=-1)
            if chunks_are_seqs:
                ref[g] = val
            else:
                ref[0, g * c:(g + 1) * c, :] = val

        put(ub_ref, ub)
        put(wm_ref, wm)
        put(qd_ref, qd)
        put(kd_ref, kd)
        put(qk_ref, qkm)
        dec_ref[g] = jnp.concatenate(dec, axis=-1)


def _delta_prep(qkv_pre3, hist8, ab3, conv_w, a_log, dt_bias, chunk, g_chunks):
    n_seq, length, w = qkv_pre3.shape
    c = chunk
    chunks_are_seqs = (length == c)
    nc = length // c
    if chunks_are_seqs:
        steps = n_seq // g_chunks
        blk = (g_chunks, c, w)
        x_map = lambda i: (i, 0, 0)
        halo_map = lambda i: (i, 0, 0)
        hist_spec = pl.BlockSpec((g_chunks, SUBLANES, w), lambda i: (i, 0, 0))
        ab_spec = pl.BlockSpec((g_chunks, c, LANES), lambda i: (i, 0, 0))
        out_map = lambda i: (i, 0, 0)
        out_rows = (g_chunks, c)
    else:
        rows = g_chunks * c
        spb = nc // g_chunks
        steps = n_seq * spb
        blk = (1, rows, w)
        x_map = lambda i: (i // spb, i % spb, 0)
        halo_map = lambda i: (i // spb, jnp.maximum((i % spb) * (rows // SUBLANES) - 1, 0), 0)
        hist_spec = pl.BlockSpec((1, SUBLANES, w), lambda i: (i // spb, 0, 0))
        ab_spec = pl.BlockSpec((1, rows, LANES), lambda i: (i // spb, i % spb, 0))
        out_map = x_map
        out_rows = (1, rows)
    lt = jnp.tril(jnp.ones((c, c), F32))
    sel = jnp.eye(SUBLANES, LANES, dtype=F32)
    hv = H_A * DV_A
    al = jnp.zeros((1, LANES), F32).at[0, :H_A].set(a_log)
    dtb = jnp.zeros((1, LANES), F32).at[0, :H_A].set(dt_bias)
    kern = functools.partial(_delta_prep_kernel, g_chunks=g_chunks, chunk=c,
                             chunks_are_seqs=chunks_are_seqs, chunks_per_seq=nc)
    big = lambda width: pl.BlockSpec(out_rows + (width,), out_map)
    return pl.pallas_call(
        kern,
        grid=(steps,),
        in_specs=[pl.BlockSpec(blk, x_map),
                  pl.BlockSpec((1, SUBLANES, w), halo_map) if not chunks_are_seqs
                  else pl.BlockSpec((g_chunks, SUBLANES, w), halo_map),
                  hist_spec, ab_spec,
                  pl.BlockSpec((CONV_A, w), lambda i: (0, 0)),
                  pl.BlockSpec((1, LANES), lambda i: (0, 0)),
                  pl.BlockSpec((1, LANES), lambda i: (0, 0)),
                  pl.BlockSpec((c, c), lambda i: (0, 0)),
                  pl.BlockSpec((SUBLANES, LANES), lambda i: (0, 0))],
        out_specs=[big(hv), big(hv), big(hv), big(hv), big(H_A * c),
                   pl.BlockSpec((g_chunks, SUBLANES, hv), lambda i: (i, 0, 0))],
        out_shape=[jax.ShapeDtypeStruct((n_seq, length, hv), F32)] * 4
        + [jax.ShapeDtypeStruct((n_seq, length, H_A * c), F32),
           jax.ShapeDtypeStruct((n_seq * nc, SUBLANES, hv), F32)],
        scratch_shapes=[pltpu.VMEM((SUBLANES + c, w), F32)],
        compiler_params=_cparams(("arbitrary",), VMEM_LIMIT),
        name="delta_prep",
    )(qkv_pre3, hist8 if chunks_are_seqs else qkv_pre3, hist8, ab3, conv_w, al, dtb, lt, sel)


def _delta_scan_kernel(ub_ref, wm_ref, qd_ref, kd_ref, qk_ref, dec_ref, z_ref, s0_ref, og_ref,
                       o_ref, sn_ref, s_ref, *, nb, chunk):
    c = chunk
    j = pl.program_id(1)

    @pl.when(j == 0)
    def _():
        s_ref[...] = s0_ref[...]

    og = og_ref[...]
    for b in range(nb):
        outs = []
        for h in range(H_A):
            lo, hi = h * DV_A, (h + 1) * DV_A
            s = s_ref[b, h]
            u = ub_ref[b, :, lo:hi] - _bdot(wm_ref[b, :, lo:hi], s)
            o = _bdot(qd_ref[b, :, lo:hi], s) + _bdot(qk_ref[b, :, h * c:(h + 1) * c], u)
            s_new = s * dec_ref[b, 0, 0:1, lo:hi] + _bdot_tn(kd_ref[b, :, lo:hi], u)
            s_ref[b, h] = s_new
            on = o * lax.rsqrt(jnp.mean(o * o, axis=-1, keepdims=True) + EPS) * og
            outs.append(on * _silu(z_ref[b, :, lo:hi]))
        o_ref[b] = jnp.concatenate(outs, axis=-1)
    sn_ref[...] = s_ref[...]


def _delta_scan(ub, wm, qd, kd, qk, dec, z3, s0, o_gain, chunk):
    n_seq, length, hv = ub.shape
    c = chunk
    nc = length // c
    nb = SEQ_BLOCK_SCAN
    dec4 = dec.reshape(n_seq, nc, SUBLANES, hv)
    tok = lambda width: pl.BlockSpec((nb, c, width), lambda b, j: (b, j, 0))
    st = pl.BlockSpec((nb, H_A, DK_A, DV_A), lambda b, j: (b, 0, 0, 0))
    return pl.pallas_call(
        functools.partial(_delta_scan_kernel, nb=nb, chunk=c),
        grid=(n_seq // nb, nc),
        in_specs=[tok(hv), tok(hv), tok(hv), tok(hv), tok(H_A * c),
                  pl.BlockSpec((nb, 1, SUBLANES, hv), lambda b, j: (b, j, 0, 0)),
                  tok(hv), st, pl.BlockSpec((1, DV_A), lambda b, j: (0, 0))],
        out_specs=[tok(hv), st],
        out_shape=[jax.ShapeDtypeStruct((n_seq, length, hv), F32),
                   jax.ShapeDtypeStruct((n_seq, H_A, DK_A, DV_A), F32)],
        scratch_shapes=[pltpu.VMEM((nb, H_A, DK_A, DV_A), F32)],
        compiler_params=_cparams(("arbitrary", "arbitrary"), VMEM_LIMIT),
        name="delta_scan",
    )(ub, wm, qd, kd, qk, dec4, z3, s0, o_gain.reshape(1, DV_A))


def _even_out_kernel(gcx_ref, halo_ref, hist_ref, oa_ref, x_ref, g1_ref, cw_ref, w_ref,
                     xo_ref, tail_ref, ext_ref, *, sb, lb, nl):
    step = pl.program_id(0)
    gcx = gcx_ref[...]
    b_gate = gcx[:, :, 0:D_B]
    bx = gcx[:, :, D_B:2 * D_B] * gcx[:, :, 2 * D_B:3 * D_B]
    if nl == 1:
        prev = hist_ref[...]
    else:
        hb = halo_ref[...]
        first = (step % nl) == 0
        prev = jnp.where(first, hist_ref[...], hb[:, :, D_B:2 * D_B] * hb[:, :, 2 * D_B:3 * D_B])
    ext_ref[:, 0:SUBLANES, :] = prev
    ext_ref[:, SUBLANES:SUBLANES + lb, :] = bx
    cw = cw_ref[...]
    conv = ext_ref[:, pl.ds(SUBLANES - (CONV_B - 1), lb), :] * cw[0:1, :]
    for j in range(1, CONV_B):
        conv = conv + ext_ref[:, pl.ds(SUBLANES - (CONV_B - 1) + j, lb), :] * cw[j:j + 1, :]
    out_b = (b_gate * conv).reshape(sb * lb, D_B)
    hv = H_A * DV_A
    y = _bdot(oa_ref[...], w_ref[0:hv, :]) + _bdot(out_b, w_ref[hv:hv + D_B, :])
    d = y.shape[-1]
    xo_ref[...] = x_ref[...] + g1_ref[...] * y.reshape(sb, lb, d)
    tail_ref[...] = bx[:, lb - SUBLANES:lb, :]


def _even_out(gcx3, hist8, out_a2, x3, mod3, k_gate, conv_w, w_out_bf16):
    n_seq, length, d = x3.shape
    sb, lb, steps, nl = _tok_tiling(n_seq, length, TOK_TILE)
    rows = sb * lb
    w3 = gcx3.shape[-1]
    hv = H_A * DV_A
    halo_map = lambda i: (i // nl, jnp.maximum((i % nl) * (lb // SUBLANES) - 1, 0), 0)
    return pl.pallas_call(
        functools.partial(_even_out_kernel, sb=sb, lb=lb, nl=nl),
        grid=(steps,),
        in_specs=[pl.BlockSpec((sb, lb, w3), lambda i: (i // nl, i % nl, 0)),
                  pl.BlockSpec((sb, SUBLANES, w3), halo_map if nl > 1 else (lambda i: (i, 0, 0))),
                  pl.BlockSpec((sb, SUBLANES, D_B), lambda i: (i // nl, 0, 0)),
                  pl.BlockSpec((rows, hv), lambda i: (i, 0)),
                  pl.BlockSpec((sb, lb, d), lambda i: (i // nl, i % nl, 0)),
                  pl.BlockSpec((sb, 1, d), lambda i: (i // nl, 0, k_gate)),
                  pl.BlockSpec((CONV_B, D_B), lambda i: (0, 0)),
                  pl.BlockSpec((hv + D_B, d), lambda i: (0, 0))],
        out_specs=[pl.BlockSpec((sb, lb, d), lambda i: (i // nl, i % nl, 0)),
                   pl.BlockSpec((sb, SUBLANES, D_B), lambda i: (i // nl, 0, 0))],
        out_shape=[jax.ShapeDtypeStruct((n_seq, length, d), F32),
                   jax.ShapeDtypeStruct((n_seq, SUBLANES, D_B), F32)],
        scratch_shapes=[pltpu.VMEM((sb, SUBLANES + lb, D_B), F32)],
        compiler_params=_cparams(("arbitrary",), VMEM_LIMIT),
        name="even_out",
    )(gcx3, gcx3, hist8, out_a2, x3, mod3, conv_w, w_out_bf16)


def _proj_res_kernel(a_ref, x_ref, g_ref, w_ref, xo_ref, *, sb, lb):
    y = _bdot(a_ref[...], w_ref[...])
    xo_ref[...] = x_ref[...] + g_ref[...] * y.reshape(sb, lb, y.shape[-1])


def _proj_residual(a2, x3, mod3, k_gate, w_bf16):
    n_seq, length, d = x3.shape
    sb, lb, steps, nl = _tok_tiling(n_seq, length, TOK_TILE)
    rows = sb * lb
    ka = a2.shape[-1]
    return pl.pallas_call(
        functools.partial(_proj_res_kernel, sb=sb, lb=lb),
        grid=(steps,),
        in_specs=[pl.BlockSpec((rows, ka), lambda i: (i, 0)),
                  pl.BlockSpec((sb, lb, d), lambda i: (i // nl, i % nl, 0)),
                  pl.BlockSpec((sb, 1, d), lambda i: (i // nl, 0, k_gate)),
                  pl.BlockSpec((ka, d), lambda i: (0, 0))],
        out_specs=pl.BlockSpec((sb, lb, d), lambda i: (i // nl, i % nl, 0)),
        out_shape=jax.ShapeDtypeStruct((n_seq, length, d), F32),
        compiler_params=_cparams(("arbitrary",), VMEM_LIMIT),
        name="proj_residual",
    )(a2, x3, mod3, w_bf16)


def _router_kernel(x_ref, sh_ref, sc_ref, gain_ref, wrt_ref, bcol_ref, utri_ref,
                   mi_ref, gcol_ref, cnt_ref, carry_ref):
    i = pl.program_id(0)

    @pl.when(i == 0)
    def _():
        carry_ref[...] = jnp.zeros_like(carry_ref)

    h2 = _norm_mod(x_ref[...], gain_ref[...], sc_ref[...], sh_ref[...])
    tm = h2.shape[0]
    logits = _bdot_nt(wrt_ref[...], h2) + bcol_ref[...]
    lg = logits[0:SUBLANES]
    e = jnp.exp(lg - jnp.max(lg, axis=0, keepdims=True))
    pg = e / jnp.sum(e, axis=0, keepdims=True)
    p_top = jnp.max(pg, axis=0, keepdims=True)
    rid8 = lax.broadcasted_iota(I32, (SUBLANES, tm), 0)
    g_top = jnp.min(jnp.where(pg == p_top, rid8, SUBLANES), axis=0, keepdims=True)
    le = logits[SUBLANES:SUBLANES + N_EXPERTS]
    sel = jnp.zeros((E_PER_GROUP, tm), F32)
    for gi in range(N_GROUPS):
        sel = sel + jnp.where(g_top == gi, le[gi * E_PER_GROUP:(gi + 1) * E_PER_GROUP], 0.0)
    e2 = jnp.exp(sel - jnp.max(sel, axis=0, keepdims=True))
    p_in = e2 / jnp.sum(e2, axis=0, keepdims=True)
    w_a = jnp.max(p_in, axis=0, keepdims=True)
    i_a = jnp.min(jnp.where(p_in == w_a, rid8, SUBLANES), axis=0, keepdims=True)
    p_rest = jnp.where(rid8 == i_a, -1.0, p_in)
    w_b = jnp.max(p_rest, axis=0, keepdims=True)
    i_b = jnp.min(jnp.where(p_rest == w_b, rid8, SUBLANES), axis=0, keepdims=True)
    den = w_a + w_b
    gate1 = p_top * (w_a / den)
    gate2 = p_top * (w_b / den)
    ex1 = g_top * E_PER_GROUP + i_a
    ex2 = g_top * E_PER_GROUP + i_b

    rid32 = lax.broadcasted_iota(I32, (N_EXPERTS, tm), 0)
    oh1 = rid32 == ex1
    oh2 = rid32 == ex2
    ohc = jnp.where(oh1, 1.0, jnp.where(oh2, 1.0, 0.0))
    cum = _bdot(ohc, utri_ref[...])
    carry = carry_ref[...]
    base = cum - ohc + carry[:, 0:1]
    r1 = jnp.sum(jnp.where(oh1, base, 0.0), axis=0, keepdims=True)
    r2 = jnp.sum(jnp.where(oh2, base, 0.0), axis=0, keepdims=True)
    new_carry = carry + cum[:, tm - 1:tm]
    carry_ref[...] = new_carry
    cnt_ref[...] = new_carry

    bro = lambda v: jnp.broadcast_to(v, (SUBLANES, tm))
    mi_ref[...] = jnp.where(rid8 == 0, bro(ex1),
                            jnp.where(rid8 == 1, bro(ex2),
                                      jnp.where(rid8 == 2, bro(r1.astype(I32)),
                                                jnp.where(rid8 == 3, bro(r2.astype(I32)), 0))))
    rid128 = lax.broadcasted_iota(I32, (LANES, tm), 0)
    g128 = jnp.where(rid128 == 0, jnp.broadcast_to(gate1, (LANES, tm)),
                     jnp.where(rid128 == 1, jnp.broadcast_to(gate2, (LANES, tm)), 0.0))
    gcol_ref[...] = g128.T


def _moe_router(x3, mod3, k_sh, k_sc, gain, wrt_bf16, bcol):
    n_seq, length, d = x3.shape
    sb, lb, steps, nl = _tok_tiling(n_seq, length, TOK_TILE)
    tm = sb * lb
    t = n_seq * length
    utri = jnp.triu(jnp.ones((tm, tm), F32)).astype(BF16)
    r_rows = wrt_bf16.shape[0]
    return pl.pallas_call(
        _router_kernel,
        grid=(steps,),
        in_specs=[pl.BlockSpec((sb, lb, d), lambda i: (i // nl, i % nl, 0)),
                  pl.BlockSpec((sb, 1, d), lambda i: (i // nl, 0, k_sh)),
                  pl.BlockSpec((sb, 1, d), lambda i: (i // nl, 0, k_sc)),
                  pl.BlockSpec((1, d), lambda i: (0, 0)),
                  pl.BlockSpec((r_rows, d), lambda i: (0, 0)),
                  pl.BlockSpec((r_rows, 1), lambda i: (0, 0)),
                  pl.BlockSpec((tm, tm), lambda i: (0, 0))],
        out_specs=[pl.BlockSpec((SUBLANES, tm), lambda i: (0, i)),
                   pl.BlockSpec((tm, LANES), lambda i: (i, 0)),
                   pl.BlockSpec((N_EXPERTS, LANES), lambda i: (0, 0))],
        out_shape=[jax.ShapeDtypeStruct((SUBLANES, t), I32),
                   jax.ShapeDtypeStruct((t, LANES), F32),
                   jax.ShapeDtypeStruct((N_EXPERTS, LANES), F32)],
        scratch_shapes=[pltpu.VMEM((N_EXPERTS, LANES), F32)],
        compiler_params=_cparams(("arbitrary",), VMEM_LIMIT),
        name="moe_router",
    )(x3, mod3, mod3, gain.reshape(1, d), wrt_bf16, bcol, utri)


def _plan_kernel(mi_ref, cnt_ref, ltri_ref, pos_ref, tinfo_ref, *, n_tiles_pad):
    cnt = cnt_ref[...]
    nt = jnp.floor((cnt + (MOE_ROW_TILE - 1)) * (1.0 / MOE_ROW_TILE))
    tstart = _bdot(ltri_ref[...], nt)
    mi = mi_ref[...]
    tm = mi.shape[1]
    rid32 = lax.broadcasted_iota(I32, (N_EXPERTS, tm), 0)
    ts_col = tstart[:, 0:1]

    def pos_of(ex, rank):
        start = jnp.sum(jnp.where(rid32 == ex, ts_col, 0.0), axis=0, keepdims=True)
        return start.astype(I32) * MOE_ROW_TILE + rank

    p1 = pos_of(mi[0:1], mi[2:3])
    p2 = pos_of(mi[1:2], mi[3:4])
    rid8 = lax.broadcasted_iota(I32, (SUBLANES, tm), 0)
    pos_ref[0] = jnp.where(rid8 == 0, jnp.broadcast_to(p1, (SUBLANES, tm)),
                           jnp.where(rid8 == 1, jnp.broadcast_to(p2, (SUBLANES, tm)), 0))
    tend_col = ts_col + nt[:, 0:1]
    jt = lax.broadcasted_iota(I32, (N_EXPERTS, n_tiles_pad), 1).astype(F32)
    te = jnp.sum(jnp.where(tend_col <= jt, 1.0, 0.0), axis=0, keepdims=True)
    te = jnp.minimum(te, N_EXPERTS - 1.0).astype(I32)
    total = jnp.sum(nt[:, 0:1], axis=0, keepdims=True).astype(I32)
    rid8t = lax.broadcasted_iota(I32, (SUBLANES, n_tiles_pad), 0)
    tinfo_ref[...] = jnp.where(rid8t == 0, jnp.broadcast_to(te, (SUBLANES, n_tiles_pad)),
                               jnp.broadcast_to(total, (SUBLANES, n_tiles_pad)))


def _moe_plan(meta_i, counts, n_tiles_pad):
    t = meta_i.shape[1]
    tm = MOE_TOK_TILE
    steps = t // tm
    ltri = jnp.tril(jnp.ones((N_EXPERTS, N_EXPERTS), F32), k=-1).astype(BF16)
    return pl.pallas_call(
        functools.partial(_plan_kernel, n_tiles_pad=n_tiles_pad),
        grid=(steps,),
        in_specs=[pl.BlockSpec((SUBLANES, tm), lambda i: (0, i)),
                  pl.BlockSpec((N_EXPERTS, LANES), lambda i: (0, 0)),
                  pl.BlockSpec((N_EXPERTS, N_EXPERTS), lambda i: (0, 0))],
        out_specs=[pl.BlockSpec((1, SUBLANES, tm), lambda i: (i, 0, 0)),
                   pl.BlockSpec((SUBLANES, n_tiles_pad), lambda i: (0, 0))],
        out_shape=[jax.ShapeDtypeStruct((steps, SUBLANES, tm), I32),
                   jax.ShapeDtypeStruct((SUBLANES, n_tiles_pad), I32)],
        compiler_params=_cparams(("arbitrary",)),
        name="moe_plan",
    )(meta_i, counts, ltri)


def _row_copy_out(buf_ref, slot, r, dst_hbm, p, sem):
    return pltpu.make_async_copy(buf_ref.at[slot, pl.ds(r, 1), :], dst_hbm.at[pl.ds(p, 1), :], sem.at[slot])


def _dispatch_kernel(x_ref, sh_ref, sc_ref, gain_ref, pos_ref, xs_in_ref, xs_ref,
                     buf_ref, idx_ref, sem_ref, isem_ref, *, tm, steps):
    del xs_in_ref
    i = pl.program_id(0)
    slot = i % 2

    def drain(s):
        for _ in range(2):
            pltpu.make_async_copy(buf_ref.at[s], xs_ref.at[pl.ds(0, tm), :], sem_ref.at[s]).wait()

    @pl.when(i >= 2)
    def _():
        drain(slot)

    buf_ref[slot] = _norm_mod(x_ref[...], gain_ref[...], sc_ref[...], sh_ref[...])
    cp = pltpu.make_async_copy(pos_ref.at[0], idx_ref, isem_ref)
    cp.start()
    cp.wait()

    def issue(r, carry):
        _row_copy_out(buf_ref, slot, r, xs_ref, idx_ref[0, r], sem_ref).start()
        _row_copy_out(buf_ref, slot, r, xs_ref, idx_ref[1, r], sem_ref).start()
        return carry
    lax.fori_loop(0, tm, issue, 0, unroll=8)

    @pl.when(i == steps - 1)
    def _():
        drain(slot)
        if steps > 1:
            drain(1 - slot)


def _moe_dispatch(x3, mod3, k_sh, k_sc, gain, pos3, p_rows):
    n_seq, length, d = x3.shape
    sb, lb, steps, nl = _tok_tiling(n_seq, length, MOE_TOK_TILE)
    tm = sb * lb
    xs0 = jnp.zeros((p_rows, d), F32)
    return pl.pallas_call(
        functools.partial(_dispatch_kernel, tm=tm, steps=steps),
        grid=(steps,),
        in_specs=[pl.BlockSpec((sb, lb, d), lambda i: (i // nl, i % nl, 0)),
                  pl.BlockSpec((sb, 1, d), lambda i: (i // nl, 0, k_sh)),
                  pl.BlockSpec((sb, 1, d), lambda i: (i // nl, 0, k_sc)),
                  pl.BlockSpec((1, d), lambda i: (0, 0)),
                  pl.BlockSpec((1, SUBLANES, tm), lambda i: (i, 0, 0)),
                  pl.BlockSpec(memory_space=pl.ANY)],
        out_specs=pl.BlockSpec(memory_space=pl.ANY),
        out_shape=jax.ShapeDtypeStruct((p_rows, d), F32),
        scratch_shapes=[pltpu.VMEM((2, tm, d), F32),
                        pltpu.SMEM((SUBLANES, tm), I32),
                        pltpu.SemaphoreType.DMA((2,)),
                        pltpu.SemaphoreType.DMA(())],
        input_output_aliases={5: 0},
        compiler_params=_cparams(("arbitrary",), VMEM_LIMIT),
        name="moe_dispatch",
    )(x3, mod3, mod3, gain.reshape(1, d), pos3, xs0)


def _expert_kernel(te_ref, nt_ref, xs_ref, wg_ref, wu_ref, wd_ref, ys_ref):
    i = pl.program_id(0)

    @pl.when(i < nt_ref[0])
    def _():
        x = xs_ref[...].astype(BF16)
        hid = _silu(jnp.dot(x, wg_ref[0], preferred_element_type=F32)) \
            * jnp.dot(x, wu_ref[0], preferred_element_type=F32)
        ys_ref[...] = jnp.dot(hid.astype(BF16), wd_ref[0], preferred_element_type=F32)

    @pl.when(i >= nt_ref[0])
    def _():
        ys_ref[...] = jnp.zeros_like(ys_ref)


def _moe_experts(xs, tile_expert, n_tiles, wg, wu, wd):
    p_rows, d = xs.shape
    f = wg.shape[-1]
    steps = p_rows // MOE_ROW_TILE
    grid_spec = pltpu.PrefetchScalarGridSpec(
        num_scalar_prefetch=2,
        grid=(steps,),
        in_specs=[pl.BlockSpec((MOE_ROW_TILE, d), lambda i, te, nt: (i, 0)),
                  pl.BlockSpec((1, d, f), lambda i, te, nt: (te[i], 0, 0)),
                  pl.BlockSpec((1, d, f), lambda i, te, nt: (te[i], 0, 0)),
                  pl.BlockSpec((1, f, d), lambda i, te, nt: (te[i], 0, 0))],
        out_specs=pl.BlockSpec((MOE_ROW_TILE, d), lambda i, te, nt: (i, 0)),
    )
    return pl.pallas_call(
        _expert_kernel,
        grid_spec=grid_spec,
        out_shape=jax.ShapeDtypeStruct((p_rows, d), F32),
        compiler_params=_cparams(("arbitrary",), VMEM_LIMIT),
        name="moe_experts",
    )(tile_expert, n_tiles, xs, wg, wu, wd)


def _row_copy_in(src_hbm, p, buf_ref, slot, which, r, sem):
    return pltpu.make_async_copy(src_hbm.at[pl.ds(p, 1), :], buf_ref.at[slot, which, pl.ds(r, 1), :],
                                 sem.at[slot])


def _combine_kernel(x_ref, g_ref, gcol_ref, pos_ref, posn_ref, ys_ref, gainf_ref, xo_ref,
                    buf_ref, idx_ref, sem_ref, isem_ref, *, tm, steps, sb, lb, final_norm):
    i = pl.program_id(0)
    slot = i % 2

    def issue(pref, s):
        cp = pltpu.make_async_copy(pref.at[0], idx_ref, isem_ref)
        cp.start()
        cp.wait()

        def body(r, carry):
            _row_copy_in(ys_ref, idx_ref[0, r], buf_ref, s, 0, r, sem_ref).start()
            _row_copy_in(ys_ref, idx_ref[1, r], buf_ref, s, 1, r, sem_ref).start()
            return carry
        lax.fori_loop(0, tm, body, 0, unroll=8)

    @pl.when(i == 0)
    def _():
        issue(pos_ref, slot)

    @pl.when(i + 1 < steps)
    def _():
        issue(posn_ref, 1 - slot)

    for which in range(2):
        pltpu.make_async_copy(ys_ref.at[pl.ds(0, tm), :], buf_ref.at[slot, which], sem_ref.at[slot]).wait()

    gc = gcol_ref[...]
    y = gc[:, 0:1] * buf_ref[slot, 0] + gc[:, 1:2] * buf_ref[slot, 1]
    d = y.shape[-1]
    xn = x_ref[...] + g_ref[...] * y.reshape(sb, lb, d)
    if final_norm:
        xn = xn * lax.rsqrt(jnp.mean(xn * xn, axis=-1, keepdims=True) + EPS) * gainf_ref[...]
    xo_ref[...] = xn


def _moe_combine(x3, mod3, k_gate, gates_col, pos3, ys, gain_f, final_norm):
    n_seq, length, d = x3.shape
    sb, lb, steps, nl = _tok_tiling(n_seq, length, MOE_TOK_TILE)
    tm = sb * lb
    return pl.pallas_call(
        functools.partial(_combine_kernel, tm=tm, steps=steps, sb=sb, lb=lb, final_norm=final_norm),
        grid=(steps,),
        in_specs=[pl.BlockSpec((sb, lb, d), lambda i: (i // nl, i % nl, 0)),
                  pl.BlockSpec((sb, 1, d), lambda i: (i // nl, 0, k_gate)),
                  pl.BlockSpec((tm, LANES), lambda i: (i, 0)),
                  pl.BlockSpec((1, SUBLANES, tm), lambda i: (i, 0, 0)),
                  pl.BlockSpec((1, SUBLANES, tm), lambda i: (jnp.minimum(i + 1, steps - 1), 0, 0)),
                  pl.BlockSpec(memory_space=pl.ANY),
                  pl.BlockSpec((1, d), lambda i: (0, 0))],
        out_specs=pl.BlockSpec((sb, lb, d), lambda i: (i // nl, i % nl, 0)),
        out_shape=jax.ShapeDtypeStruct((n_seq, length, d), F32),
        scratch_shapes=[pltpu.VMEM((2, 2, tm, d), F32),
                        pltpu.SMEM((SUBLANES, tm), I32),
                        pltpu.SemaphoreType.DMA((2,)),
                        pltpu.SemaphoreType.DMA(())],
        compiler_params=_cparams(("arbitrary",), VMEM_LIMIT),
        name="moe_combine",
    )(x3, mod3, gates_col, pos3, pos3, ys, gain_f.reshape(1, d))


def _hier_moe_residual(x3, mod3, k_sh, k_sc, k_gate, gain, wrt_bf16, bcol, wg, wu, wd, gain_f, final_norm):
    n_seq, length, d = x3.shape
    t = n_seq * length
    max_tiles = (2 * t) // MOE_ROW_TILE + N_EXPERTS
    n_tiles_pad = -(-max_tiles // LANES) * LANES
    meta_i, gates_col, counts = _moe_router(x3, mod3, k_sh, k_sc, gain, wrt_bf16, bcol)
    pos3, tinfo = _moe_plan(meta_i, counts, n_tiles_pad)
    xs = _moe_dispatch(x3, mod3, k_sh, k_sc, gain, pos3, max_tiles * MOE_ROW_TILE)
    ys = _moe_experts(xs, tinfo[0, :max_tiles], tinfo[1, :1], wg, wu, wd)
    return _moe_combine(x3, mod3, k_gate, gates_col, pos3, ys, gain_f, final_norm)


def _multiplicity(dist):
    m = jnp.zeros(dist.shape, F32)
    for window, dil in BRANCHES:
        m = m + ((dist >= 0) & (dist <= window) & (dist % dil == 0)).astype(F32)
    return m


def _log_mult(dist):
    m = _multiplicity(dist)
    return jnp.where(m > 0, jnp.log(jnp.maximum(m, 1.0)), NEG_BIG)


def _attn_prompt_kernel(q_ref, k_ref, v_ref, bias_ref, o_ref, *, length):
    blk = ATT_BLK
    nq = length // blk
    lane = lax.broadcasted_iota(I32, (blk, LANES), 1)
    head0 = lane < HD_C
    scale = HD_C ** -0.5
    for qi in range(nq):
        q = q_ref[0, qi * blk:(qi + 1) * blk, :] * scale
        q0 = jnp.where(head0, q, 0.0).astype(BF16)
        q1 = jnp.where(head0, 0.0, q).astype(BF16)

        def body(kj, carry):
            m0, l0, m1, l1, acc = carry
            ks = pl.multiple_of(kj * blk, blk)
            k = k_ref[0, pl.ds(ks, blk), :].astype(BF16)
            v = v_ref[0, pl.ds(ks, blk), :].astype(BF16)
            bias = bias_ref[qi - kj]
            outs = []
            stats = []
            for qh, m_old, l_old in ((q0, m0, l0), (q1, m1, l1)):
                s = lax.dot_general(qh, k, (((1,), (1,)), ((), ())), preferred_element_type=F32) + bias
                m_new = jnp.maximum(m_old, jnp.max(s, axis=-1, keepdims=True))
                alpha = jnp.exp(m_old - m_new)
                p = jnp.exp(s - m_new)
                l_new = l_old * alpha + jnp.sum(p, axis=-1, keepdims=True)
                outs.append((alpha, jnp.dot(p.astype(BF16), v, preferred_element_type=F32)))
                stats.append((m_new, l_new))
            acc = jnp.where(head0, acc * outs[0][0] + outs[0][1], acc * outs[1][0] + outs[1][1])
            return stats[0][0], stats[0][1], stats[1][0], stats[1][1], acc

        init = (jnp.full((blk, 1), NEG_BIG, F32), jnp.zeros((blk, 1), F32),
                jnp.full((blk, 1), NEG_BIG, F32), jnp.zeros((blk, 1), F32),
                jnp.zeros((blk, LANES), F32))
        m0, l0, m1, l1, acc = lax.fori_loop(0, qi + 1, body, init)
        o_ref[0, qi * blk:(qi + 1) * blk, :] = acc / jnp.where(head0, l0, l1)


def _attn_prompt(qkv3):
    n_seq, length, w3 = qkv3.shape
    d = w3 // 3
    pairs = d // LANES
    blk = ATT_BLK
    nq = length // blk
    a = jnp.arange(blk)
    dist = (jnp.arange(nq)[:, None, None] * blk) + a[None, :, None] - a[None, None, :]
    bias = _log_mult(dist)
    return pl.pallas_call(
        functools.partial(_attn_prompt_kernel, length=length),
        grid=(n_seq, pairs),
        in_specs=[pl.BlockSpec((1, length, LANES), lambda n, p: (n, 0, p)),
                  pl.BlockSpec((1, length, LANES), lambda n, p: (n, 0, pairs + p)),
                  pl.BlockSpec((1, length, LANES), lambda n, p: (n, 0, 2 * pairs + p)),
                  pl.BlockSpec((nq, blk, blk), lambda n, p: (0, 0, 0))],
        out_specs=pl.BlockSpec((1, length, LANES), lambda n, p: (n, 0, p)),
        out_shape=jax.ShapeDtypeStruct((n_seq, length, d), F32),
        compiler_params=_cparams(("arbitrary", "arbitrary"), VMEM_LIMIT),
        name="attn_prompt",
    )(qkv3, qkv3, qkv3, bias)


def _attn_sample_kernel(q_ref, kn_ref, vn_ref, kt_ref, vt_ref, bias_ref, biasn_ref, o_ref, *, heads):
    q = q_ref[0] * (HD_C ** -0.5)
    kn = kn_ref[0]
    vn = vn_ref[0]
    bias = bias_ref[...]
    biasn = biasn_ref[...]
    outs = []
    for h in range(heads):
        lo, hi = h * HD_C, (h + 1) * HD_C
        qh = q[:, lo:hi].astype(BF16)
        s = jnp.dot(qh, kt_ref[0, h].astype(BF16), preferred_element_type=F32) + bias
        sn = _bdot_nt(qh, kn[:, lo:hi]) + biasn
        m = jnp.maximum(jnp.max(s, axis=-1, keepdims=True), jnp.max(sn, axis=-1, keepdims=True))
        p = jnp.exp(s - m)
        pn = jnp.exp(sn - m)
        den = jnp.sum(p, axis=-1, keepdims=True) + jnp.sum(pn, axis=-1, keepdims=True)
        o = _bdot_nt(p, vt_ref[0, h]) + _bdot(pn, vn[:, lo:hi])
        outs.append(o / den)
    o_ref[0] = jnp.concatenate(outs, axis=-1)


def _attn_sample(qkv3, k_cache, v_cache):
    n_seq, t_len, w3 = qkv3.shape
    d = w3 // 3
    w_buf = k_cache.shape[1]
    kt = jnp.transpose(k_cache, (0, 2, 3, 1))
    vt = jnp.transpose(v_cache, (0, 2, 3, 1))
    heads = SAMPLE_HEADS_PER_STEP
    hw = heads * HD_C
    nj = H_C // heads
    tpos = w_buf + jnp.arange(t_len)
    bias = _log_mult(tpos[:, None] - jnp.arange(w_buf)[None, :])
    biasn = _log_mult(jnp.arange(t_len)[:, None] - jnp.arange(t_len)[None, :])
    return pl.pallas_call(
        functools.partial(_attn_sample_kernel, heads=heads),
        grid=(n_seq, nj),
        in_specs=[pl.BlockSpec((1, t_len, hw), lambda n, j: (n, 0, j)),
                  pl.BlockSpec((1, t_len, hw), lambda n, j: (n, 0, nj + j)),
                  pl.BlockSpec((1, t_len, hw), lambda n, j: (n, 0, 2 * nj + j)),
                  pl.BlockSpec((1, heads, HD_C, w_buf), lambda n, j: (n, j, 0, 0)),
                  pl.BlockSpec((1, heads, HD_C, w_buf), lambda n, j: (n, j, 0, 0)),
                  pl.BlockSpec((t_len, w_buf), lambda n, j: (0, 0)),
                  pl.BlockSpec((t_len, t_len), lambda n, j: (0, 0))],
        out_specs=pl.BlockSpec((1, t_len, hw), lambda n, j: (n, 0, j)),
        out_shape=jax.ShapeDtypeStruct((n_seq, t_len, d), F32),
        compiler_params=_cparams(("arbitrary", "arbitrary"), VMEM_LIMIT),
        name="attn_sample",
    )(qkv3, qkv3, qkv3, kt, vt, bias, biasn)


def _pad_hist(hist, n_seq, width):
    k1 = hist.shape[1]
    return jnp.concatenate([jnp.zeros((n_seq, SUBLANES - k1, width), F32), hist.astype(F32)], axis=1)


def _prep_weights(p):
    d = p['w_in_even'].shape[1]
    out = {}
    w_in = p['w_in_even'][0]
    hv = H_A * DV_A
    o_z = QKV_A
    o_a = o_z + hv
    o_g = o_a + 2 * H_A
    ab_pad = jnp.zeros((d, LANES - 2 * H_A), F32)
    out['w_in'] = jnp.concatenate([w_in[:, :o_a], w_in[:, o_a:o_g], ab_pad, w_in[:, o_g:]], axis=1).astype(BF16)
    out['w_out_even'] = p['w_out_even'][0].astype(BF16)
    out['w_qkv'] = p['w_qkv_odd'][0].astype(BF16)
    out['w_out_odd'] = p['w_out_odd'][0].astype(BF16)
    depth = p['w_router_group'].shape[0]
    wrt, bcol, wg, wu, wd = [], [], [], [], []
    for l in range(depth):
        pad_w = jnp.zeros((SUBLANES - N_GROUPS, d), F32)
        wrt.append(jnp.concatenate([p['w_router_group'][l].T, pad_w, p['w_router_expert'][l].T], axis=0).astype(BF16))
        pad_b = jnp.full((SUBLANES - N_GROUPS,), NEG_BIG, F32)
        bcol.append(jnp.concatenate([p['b_router_group'][l], pad_b, p['b_router_expert'][l]]).reshape(-1, 1))
        f = p['w_exp_gate'].shape[-1]
        wg.append(p['w_exp_gate'][l].reshape(N_EXPERTS, d, f).astype(BF16))
        wu.append(p['w_exp_up'][l].reshape(N_EXPERTS, d, f).astype(BF16))
        wd.append(p['w_exp_down'][l].reshape(N_EXPERTS, f, d).astype(BF16))
    out.update(wrt=wrt, bcol=bcol, wg=wg, wu=wu, wd=wd)
    return out


def _trunk(x3, mod, a_ssm, a_conv, b_conv, kv_k, kv_v, p, w, sample):
    n_seq, length, d = x3.shape
    t = n_seq * length
    hv = H_A * DV_A
    mod3 = [m.reshape(n_seq, 1, 6 * d) for m in mod]

    qkv_pre, z, ab, gcx = _norm_mod_matmul(x3, mod3[0], 0, 1, p['norm1'][0], w['w_in'],
                                           (QKV_A, hv, LANES, 3 * D_B))
    chunk = min(CHUNK_PROMPT, length)
    g_chunks = 4
    qkv_pre3 = qkv_pre.reshape(n_seq, length, QKV_A)
    ub, wm, qd, kd, qk, dec = _delta_prep(qkv_pre3, _pad_hist(a_conv[0], n_seq, QKV_A),
                                          ab.reshape(n_seq, length, LANES), p['conv_a_w'][0],
                                          p['a_log'][0], p['dt_bias'][0], chunk, g_chunks)
    out_a, s_new = _delta_scan(ub, wm, qd, kd, qk, dec, z.reshape(n_seq, length, hv),
                               a_ssm[0].astype(F32), p['o_gain_a'][0], chunk)
    x3, bx_tail = _even_out(gcx.reshape(n_seq, length, 3 * D_B), _pad_hist(b_conv[0], n_seq, D_B),
                            out_a.reshape(t, hv), x3, mod3[0], 2, p['conv_b_w'][0], w['w_out_even'])
    new_aconv = jnp.concatenate([a_conv[0].astype(F32), qkv_pre3], axis=1)[:, -(CONV_A - 1):]
    new_bconv = jnp.concatenate([b_conv[0].astype(F32), bx_tail], axis=1)[:, -(CONV_B - 1):]
    x3 = _hier_moe_residual(x3, mod3[0], 3, 4, 5, p['norm2'][0], w['wrt'][0], w['bcol'][0],
                            w['wg'][0], w['wu'][0], w['wd'][0], p['norm_f'], False)

    (qkv,) = _norm_mod_matmul(x3, mod3[1], 0, 1, p['norm1'][1], w['w_qkv'], (3 * d,))
    qkv3 = qkv.reshape(n_seq, length, 3 * d)
    k_new = qkv3[:, :, d:2 * d]
    v_new = qkv3[:, :, 2 * d:]
    if sample:
        attn = _attn_sample(qkv3, kv_k[0], kv_v[0])
        keep = length
    else:
        attn = _attn_prompt(qkv3)
        keep = min(BRANCHES[-1][0], length)
    new_k = k_new[:, length - keep:].reshape(n_seq, keep, H_C, HD_C)
    new_v = v_new[:, length - keep:].reshape(n_seq, keep, H_C, HD_C)
    x3 = _proj_residual(attn.reshape(t, d), x3, mod3[1], 2, w['w_out_odd'])
    y = _hier_moe_residual(x3, mod3[1], 3, 4, 5, p['norm2'][1], w['wrt'][1], w['bcol'][1],
                           w['wg'][1], w['wu'][1], w['wd'][1], p['norm_f'], True)
    return y, s_new[None], new_aconv[None], new_bconv[None], new_k[None], new_v[None]


def kernel(x_prompt, x_sample, state_a_ssm, state_a_conv, state_b_conv, cache_c_k, cache_c_v, c_prompt, c_sample, w_ada, b_ada, norm1, norm2, norm_f, w_in_even, conv_a_w, a_log, dt_bias, o_gain_a, conv_b_w, w_out_even, w_qkv_odd, w_out_odd, w_router_group, b_router_group, w_router_expert, b_router_expert, w_exp_gate, w_exp_up, w_exp_down):
    p = dict(norm1=norm1, norm2=norm2, norm_f=norm_f, w_in_even=w_in_even, conv_a_w=conv_a_w,
             a_log=a_log, dt_bias=dt_bias, o_gain_a=o_gain_a, conv_b_w=conv_b_w, w_out_even=w_out_even,
             w_qkv_odd=w_qkv_odd, w_out_odd=w_out_odd, w_router_group=w_router_group,
             b_router_group=b_router_group, w_router_expert=w_router_expert,
             b_router_expert=b_router_expert, w_exp_gate=w_exp_gate, w_exp_up=w_exp_up,
             w_exp_down=w_exp_down)
    w = _prep_weights(p)
    nb = x_prompt.shape[0]
    ns = x_sample.shape[0]
    mod_all = _ada_mod(jnp.concatenate([c_prompt, c_sample], axis=0), w_ada, b_ada)
    depth = w_ada.shape[0]
    mod_p = [mod_all[l, :nb] for l in range(depth)]
    mod_s = [mod_all[l, nb:] for l in range(depth)]
    n_even = state_a_ssm.shape[0]
    zero_ssm = jnp.zeros((n_even, nb, H_A, DK_A, DV_A), F32)
    zero_aconv = jnp.zeros((n_even, nb, CONV_A - 1, QKV_A), F32)
    zero_bconv = jnp.zeros((n_even, nb, CONV_B - 1, D_B), F32)
    y_p, ssm_p, aconv_p, bconv_p, k_p, v_p = _trunk(x_prompt, mod_p, zero_ssm, zero_aconv, zero_bconv,
                                                    None, None, p, w, False)
    y_s, ssm_s, aconv_s, bconv_s, k_s, v_s = _trunk(x_sample, mod_s, state_a_ssm, state_a_conv,
                                                    state_b_conv, cache_c_k, cache_c_v, p, w, True)
    return (y_p, y_s, ssm_p, ssm_s, aconv_p, aconv_s, bconv_p, bconv_s, k_p, k_s, v_p, v_s)
```

```python
import functools
import math

import jax
import jax.numpy as jnp
from jax import lax
from jax.experimental import pallas as pl
from jax.experimental.pallas import tpu as pltpu

F32 = jnp.float32
BF16 = jnp.bfloat16
I32 = jnp.int32

EPS = 1e-6
NEG_BIG = -1e30

LANES = 128
SUBLANES = 8
VMEM_BYTES_V7X = 64 * 1024 * 1024
VMEM_LIMIT = 56 * 1024 * 1024

H_A, DK_A, DV_A = 4, 128, 128
CONV_A, CONV_B = 4, 3
QKV_A = H_A * (2 * DK_A + DV_A)
D_B = 512
H_C, HD_C = 16, 64
BRANCHES = ((128, 1), (512, 4), (2048, 16))
N_GROUPS, E_PER_GROUP = 4, 8
N_EXPERTS = N_GROUPS * E_PER_GROUP
CHUNK_PROMPT = 64

TOK_TILE = 512
MOE_TOK_TILE = 512
MOE_ROW_TILE_BIG = 512
MOE_ROW_TILE_SMALL = 128
ATT_BLK = 256
ATT_NEAR_BLOCKS = 3
LOG2_E = math.log2(math.e)
SEQ_BLOCK_SCAN = 8
SAMPLE_HEADS_PER_STEP = 8


def _cparams(sem, vmem=None):
    return pltpu.CompilerParams(dimension_semantics=sem, vmem_limit_bytes=vmem)


def _silu(x):
    return x * jax.nn.sigmoid(x)


def _bdot(a, b):
    return jnp.dot(a.astype(BF16), b.astype(BF16), preferred_element_type=F32)


def _bdot_nt(a, b):
    return lax.dot_general(a.astype(BF16), b.astype(BF16), (((1,), (1,)), ((), ())),
                           preferred_element_type=F32)


def _bdot_tn(a, b):
    return lax.dot_general(a.astype(BF16), b.astype(BF16), (((0,), (0,)), ((), ())),
                           preferred_element_type=F32)


def _split_bf16(x):
    hi = x.astype(BF16)
    return hi, (x - hi.astype(F32)).astype(BF16)


def _dot3(a, b):
    ah, al = _split_bf16(a)
    bh, bl = _split_bf16(b)
    dot = functools.partial(jnp.dot, preferred_element_type=F32)
    return dot(ah, bh) + (dot(ah, bl) + dot(al, bh))


def _fdot(a, b):
    return jnp.dot(a, b, preferred_element_type=F32, precision=lax.Precision.HIGHEST)


def _fdot_nt(a, b):
    return lax.dot_general(a, b, (((1,), (1,)), ((), ())), preferred_element_type=F32,
                           precision=lax.Precision.HIGHEST)


def _tok_tiling(n_seq, length, tile):
    if length >= tile:
        sb, lb = 1, tile
    else:
        sb, lb = tile // length, length
    nl = length // lb
    return sb, lb, (n_seq // sb) * nl, nl


def _norm_mod(x, gain, sc, sh):
    sb, lb, d = x.shape
    y = x * lax.rsqrt(jnp.mean(x * x, axis=-1, keepdims=True) + EPS) * gain
    h = y * (1.0 + sc) + sh
    return h.reshape(sb * lb, d)


def _ada_kernel(c_ref, w_ref, b_ref, o_ref):
    c = _silu(c_ref[...])
    o_ref[0] = _bdot(c, w_ref[0]) + b_ref[0]


def _ada_mod(c_all, w_ada, b_ada):
    r, d = c_all.shape
    depth, _, n6 = w_ada.shape
    tn = 1536
    return pl.pallas_call(
        _ada_kernel,
        grid=(depth, n6 // tn),
        in_specs=[pl.BlockSpec((r, d), lambda l, j: (0, 0)),
                  pl.BlockSpec((1, d, tn), lambda l, j: (l, 0, j)),
                  pl.BlockSpec((1, 1, tn), lambda l, j: (l, 0, j))],
        out_specs=pl.BlockSpec((1, r, tn), lambda l, j: (l, 0, j)),
        out_shape=jax.ShapeDtypeStruct((depth, r, n6), F32),
        compiler_params=_cparams(("arbitrary", "arbitrary"), VMEM_LIMIT),
        name="ada_mod",
    )(c_all, w_ada, b_ada.reshape(depth, 1, n6))


def _nmm_kernel(x_ref, sh_ref, sc_ref, gain_ref, w_ref, *o_refs, splits):
    h = _norm_mod(x_ref[...], gain_ref[...], sc_ref[...], sh_ref[...]).astype(BF16)
    off = 0
    for o_ref, n in zip(o_refs, splits):
        o_ref[...] = jnp.dot(h, w_ref[:, off:off + n], preferred_element_type=F32)
        off += n


def _norm_mod_matmul(x3, mod3, k_sh, k_sc, gain, w_bf16, splits):
    n_seq, length, d = x3.shape
    sb, lb, steps, nl = _tok_tiling(n_seq, length, TOK_TILE)
    rows = sb * lb
    t = n_seq * length
    n_out = w_bf16.shape[1]
    assert sum(splits) == n_out
    return pl.pallas_call(
        functools.partial(_nmm_kernel, splits=splits),
        grid=(steps,),
        in_specs=[pl.BlockSpec((sb, lb, d), lambda i: (i // nl, i % nl, 0)),
                  pl.BlockSpec((sb, 1, d), lambda i: (i // nl, 0, k_sh)),
                  pl.BlockSpec((sb, 1, d), lambda i: (i // nl, 0, k_sc)),
                  pl.BlockSpec((1, d), lambda i: (0, 0)),
                  pl.BlockSpec((d, n_out), lambda i: (0, 0))],
        out_specs=[pl.BlockSpec((rows, n), lambda i: (i, 0)) for n in splits],
        out_shape=[jax.ShapeDtypeStruct((t, n), F32) for n in splits],
        compiler_params=_cparams(("arbitrary",), VMEM_LIMIT),
        name="norm_mod_matmul",
    )(x3, mod3, mod3, gain.reshape(1, d), w_bf16)


def _delta_prep_kernel(qkv_ref, halo_ref, hist_ref, ab_ref, cw_ref, al_ref, dtb_ref, lt_ref, sel_ref,
                       ub_ref, wm_ref, qd_ref, kd_ref, qk_ref, dec_ref, ext_ref,
                       *, g_chunks, chunk, chunks_are_seqs, chunks_per_seq):
    c = chunk
    step = pl.program_id(0)
    cw = cw_ref[...]
    row = lax.broadcasted_iota(I32, (c, c), 0)
    col = lax.broadcasted_iota(I32, (c, c), 1)
    lower_incl = row >= col
    strict = row > col
    eye = (row == col).astype(F32)
    n_fac = max(int(math.ceil(math.log2(c))), 1)

    qkvs, abs_ = [], []
    for g in range(g_chunks):
        cur = qkv_ref[g] if chunks_are_seqs else qkv_ref[0, g * c:(g + 1) * c, :]
        if chunks_are_seqs:
            prev = hist_ref[g]
        elif g == 0:
            first = (step % (chunks_per_seq // g_chunks)) == 0
            prev = jnp.where(first, hist_ref[0], halo_ref[0])
        else:
            prev = qkv_ref[0, g * c - SUBLANES:g * c, :]
        ext_ref[g, 0:SUBLANES, :] = prev
        ext_ref[g, SUBLANES:SUBLANES + c, :] = cur
        conv = ext_ref[g, pl.ds(SUBLANES - (CONV_A - 1), c), :] * cw[0:1, :]
        for j in range(1, CONV_A):
            conv = conv + ext_ref[g, pl.ds(SUBLANES - (CONV_A - 1) + j, c), :] * cw[j:j + 1, :]
        qkvs.append(_silu(conv))
        abs_.append(ab_ref[g] if chunks_are_seqs else ab_ref[0, g * c:(g + 1) * c, :])

    g_alls = [-jnp.exp(al_ref[...]) * jax.nn.softplus(ab + dtb_ref[...]) for ab in abs_]
    sigs = [jax.nn.sigmoid(ab) for ab in abs_]
    gcum_alls = [_fdot(lt_ref[...], ga) for ga in g_alls]
    gcum_rows = [_fdot_nt(sel_ref[...], gc) for gc in gcum_alls]

    chains = [(g, h) for g in range(g_chunks) for h in range(H_A)]
    qs, ks, vs, betas, gcs, gammas, egs = [], [], [], [], [], [], []
    for g, h in chains:
        qkv = qkvs[g]
        q = qkv[:, h * DK_A:(h + 1) * DK_A]
        k = qkv[:, H_A * DK_A + h * DK_A:H_A * DK_A + (h + 1) * DK_A]
        v = qkv[:, 2 * H_A * DK_A + h * DV_A:2 * H_A * DK_A + (h + 1) * DV_A]
        qs.append(q * lax.rsqrt(jnp.sum(q * q, axis=-1, keepdims=True) + EPS) * (DK_A ** -0.5))
        ks.append(k * lax.rsqrt(jnp.sum(k * k, axis=-1, keepdims=True) + EPS))
        vs.append(v)
        betas.append(sigs[g][:, H_A + h:H_A + h + 1])
        gc = gcum_alls[g][:, h:h + 1]
        gr = gcum_rows[g][h:h + 1, :]
        gcs.append(gc)
        gammas.append(jnp.exp(jnp.where(lower_incl, gc - gr, NEG_BIG)))
        egs.append(jnp.exp(gc))

    kq = [_bdot_nt(jnp.concatenate([k, q], axis=0), k) for k, q in zip(ks, qs)]
    a_mats = [jnp.where(strict, b * x[0:c] * gm, 0.0) for b, x, gm in zip(betas, kq, gammas)]
    qkms = [x[c:2 * c] * gm for x, gm in zip(kq, gammas)]
    m_pows = [-a for a in a_mats]
    t_invs = [eye + m for m in m_pows]
    for _ in range(n_fac - 1):
        m_pows = [_bdot(m, m) for m in m_pows]
        t_invs = [t + _bdot(t, m) for t, m in zip(t_invs, m_pows)]
    resids = [eye - t - _dot3(a, t) for a, t in zip(a_mats, t_invs)]
    t_invs = [t + _bdot(t, r) for t, r in zip(t_invs, resids)]
    rhss = [jnp.concatenate([v * b, k * (b * eg)], axis=-1) for v, k, b, eg in zip(vs, ks, betas, egs)]
    sols = [rhs + _dot3(t - eye, rhs) for t, rhs in zip(t_invs, rhss)]

    for g in range(g_chunks):
        idx = [i for i, (gg, _) in enumerate(chains) if gg == g]
        g_last = [gcum_alls[g][c - 1:c, h:h + 1] for h in range(H_A)]

        def put(ref, parts):
            val = jnp.concatenate(parts, axis=-1)
            if chunks_are_seqs:
                ref[g] = val
            else:
                ref[0, g * c:(g + 1) * c, :] = val

        put(ub_ref, [sols[i][:, :DV_A] for i in idx])
        put(wm_ref, [sols[i][:, DV_A:] for i in idx])
        put(qd_ref, [qs[i] * egs[i] for i in idx])
        put(kd_ref, [ks[i] * jnp.exp(g_last[h] - gcs[i]) for h, i in enumerate(idx)])
        put(qk_ref, [qkms[i] for i in idx])
        dec_ref[g] = jnp.concatenate([jnp.broadcast_to(jnp.exp(gl), (SUBLANES, DV_A)) for gl in g_last], axis=-1)


def _delta_prep(qkv_pre3, hist8, ab3, conv_w, a_log, dt_bias, chunk, g_chunks):
    n_seq, length, w = qkv_pre3.shape
    c = chunk
    chunks_are_seqs = (length == c)
    nc = length // c
    if chunks_are_seqs:
        steps = n_seq // g_chunks
        blk = (g_chunks, c, w)
        x_map = lambda i: (i, 0, 0)
        halo_map = lambda i: (i, 0, 0)
        hist_spec = pl.BlockSpec((g_chunks, SUBLANES, w), lambda i: (i, 0, 0))
        ab_spec = pl.BlockSpec((g_chunks, c, LANES), lambda i: (i, 0, 0))
        out_map = lambda i: (i, 0, 0)
        out_rows = (g_chunks, c)
    else:
        rows = g_chunks * c
        spb = nc // g_chunks
        steps = n_seq * spb
        blk = (1, rows, w)
        x_map = lambda i: (i // spb, i % spb, 0)
        halo_map = lambda i: (i // spb, jnp.maximum((i % spb) * (rows // SUBLANES) - 1, 0), 0)
        hist_spec = pl.BlockSpec((1, SUBLANES, w), lambda i: (i // spb, 0, 0))
        ab_spec = pl.BlockSpec((1, rows, LANES), lambda i: (i // spb, i % spb, 0))
        out_map = x_map
        out_rows = (1, rows)
    lt = jnp.tril(jnp.ones((c, c), F32))
    sel = jnp.eye(SUBLANES, LANES, dtype=F32)
    hv = H_A * DV_A
    al = jnp.zeros((1, LANES), F32).at[0, :H_A].set(a_log)
    dtb = jnp.zeros((1, LANES), F32).at[0, :H_A].set(dt_bias)
    kern = functools.partial(_delta_prep_kernel, g_chunks=g_chunks, chunk=c,
                             chunks_are_seqs=chunks_are_seqs, chunks_per_seq=nc)
    big = lambda width: pl.BlockSpec(out_rows + (width,), out_map)
    return pl.pallas_call(
        kern,
        grid=(steps,),
        in_specs=[pl.BlockSpec(blk, x_map),
                  pl.BlockSpec((1, SUBLANES, w), halo_map) if not chunks_are_seqs
                  else pl.BlockSpec((g_chunks, SUBLANES, w), halo_map),
                  hist_spec, ab_spec,
                  pl.BlockSpec((CONV_A, w), lambda i: (0, 0)),
                  pl.BlockSpec((1, LANES), lambda i: (0, 0)),
                  pl.BlockSpec((1, LANES), lambda i: (0, 0)),
                  pl.BlockSpec((c, c), lambda i: (0, 0)),
                  pl.BlockSpec((SUBLANES, LANES), lambda i: (0, 0))],
        out_specs=[big(hv), big(hv), big(hv), big(hv), big(H_A * c),
                   pl.BlockSpec((g_chunks, SUBLANES, hv), lambda i: (i, 0, 0))],
        out_shape=[jax.ShapeDtypeStruct((n_seq, length, hv), F32)] * 4
        + [jax.ShapeDtypeStruct((n_seq, length, H_A * c), F32),
           jax.ShapeDtypeStruct((n_seq * nc, SUBLANES, hv), F32)],
        scratch_shapes=[pltpu.VMEM((g_chunks, SUBLANES + c, w), F32)],
        compiler_params=_cparams(("arbitrary",), VMEM_LIMIT),
        name="delta_prep",
    )(qkv_pre3, hist8 if chunks_are_seqs else qkv_pre3, hist8, ab3, conv_w, al, dtb, lt, sel)


def _delta_scan_kernel(ub_ref, wm_ref, qd_ref, kd_ref, qk_ref, dec_ref, z_ref, s0_ref, og_ref,
                       o_ref, sn_ref, s_ref, *, nb, chunk):
    c = chunk
    j = pl.program_id(1)

    @pl.when(j == 0)
    def _():
        s_ref[...] = s0_ref[...]

    og = og_ref[...]
    chains = [(b, h) for b in range(nb) for h in range(H_A)]
    sl = lambda h: slice(h * DV_A, (h + 1) * DV_A)
    states = [s_ref[b, h] for b, h in chains]
    ws = [_bdot(jnp.concatenate([wm_ref[b, :, sl(h)], qd_ref[b, :, sl(h)]], axis=0), s)
          for (b, h), s in zip(chains, states)]
    us = [ub_ref[b, :, sl(h)] - x[0:c] for (b, h), x in zip(chains, ws)]
    os_ = [x[c:2 * c] + _bdot(qk_ref[b, :, h * c:(h + 1) * c], u) for (b, h), x, u in zip(chains, ws, us)]
    for (b, h), s, u in zip(chains, states, us):
        s_ref[b, h] = s * dec_ref[b, 0, 0:1, sl(h)] + _bdot_tn(kd_ref[b, :, sl(h)], u)
    for b in range(nb):
        outs = []
        for h in range(H_A):
            o = os_[b * H_A + h]
            on = o * lax.rsqrt(jnp.mean(o * o, axis=-1, keepdims=True) + EPS) * og
            outs.append(on * _silu(z_ref[b, :, sl(h)]))
        o_ref[b] = jnp.concatenate(outs, axis=-1)
    sn_ref[...] = s_ref[...]


def _delta_scan(ub, wm, qd, kd, qk, dec, z3, s0, o_gain, chunk):
    n_seq, length, hv = ub.shape
    c = chunk
    nc = length // c
    nb = SEQ_BLOCK_SCAN
    dec4 = dec.reshape(n_seq, nc, SUBLANES, hv)
    tok = lambda width: pl.BlockSpec((nb, c, width), lambda b, j: (b, j, 0))
    st = pl.BlockSpec((nb, H_A, DK_A, DV_A), lambda b, j: (b, 0, 0, 0))
    return pl.pallas_call(
        functools.partial(_delta_scan_kernel, nb=nb, chunk=c),
        grid=(n_seq // nb, nc),
        in_specs=[tok(hv), tok(hv), tok(hv), tok(hv), tok(H_A * c),
                  pl.BlockSpec((nb, 1, SUBLANES, hv), lambda b, j: (b, j, 0, 0)),
                  tok(hv), st, pl.BlockSpec((1, DV_A), lambda b, j: (0, 0))],
        out_specs=[tok(hv), st],
        out_shape=[jax.ShapeDtypeStruct((n_seq, length, hv), F32),
                   jax.ShapeDtypeStruct((n_seq, H_A, DK_A, DV_A), F32)],
        scratch_shapes=[pltpu.VMEM((nb, H_A, DK_A, DV_A), F32)],
        compiler_params=_cparams(("arbitrary", "arbitrary"), VMEM_LIMIT),
        name="delta_scan",
    )(ub, wm, qd, kd, qk, dec4, z3, s0, o_gain.reshape(1, DV_A))


def _even_out_kernel(gcx_ref, halo_ref, hist_ref, oa_ref, x_ref, g1_ref, cw_ref, w_ref,
                     xo_ref, tail_ref, ext_ref, *, sb, lb, nl):
    step = pl.program_id(0)
    gcx = gcx_ref[...]
    b_gate = gcx[:, :, 0:D_B]
    bx = gcx[:, :, D_B:2 * D_B] * gcx[:, :, 2 * D_B:3 * D_B]
    if nl == 1:
        prev = hist_ref[...]
    else:
        hb = halo_ref[...]
        first = (step % nl) == 0
        prev = jnp.where(first, hist_ref[...], hb[:, :, D_B:2 * D_B] * hb[:, :, 2 * D_B:3 * D_B])
    ext_ref[:, 0:SUBLANES, :] = prev
    ext_ref[:, SUBLANES:SUBLANES + lb, :] = bx
    cw = cw_ref[...]
    conv = ext_ref[:, pl.ds(SUBLANES - (CONV_B - 1), lb), :] * cw[0:1, :]
    for j in range(1, CONV_B):
        conv = conv + ext_ref[:, pl.ds(SUBLANES - (CONV_B - 1) + j, lb), :] * cw[j:j + 1, :]
    out_b = (b_gate * conv).reshape(sb * lb, D_B)
    hv = H_A * DV_A
    y = _bdot(oa_ref[...], w_ref[0:hv, :]) + _bdot(out_b, w_ref[hv:hv + D_B, :])
    d = y.shape[-1]
    xo_ref[...] = x_ref[...] + g1_ref[...] * y.reshape(sb, lb, d)
    tail_ref[...] = bx[:, lb - SUBLANES:lb, :]


def _even_out(gcx3, hist8, out_a2, x3, mod3, k_gate, conv_w, w_out_bf16):
    n_seq, length, d = x3.shape
    sb, lb, steps, nl = _tok_tiling(n_seq, length, TOK_TILE)
    rows = sb * lb
    w3 = gcx3.shape[-1]
    hv = H_A * DV_A
    halo_map = lambda i: (i // nl, jnp.maximum((i % nl) * (lb // SUBLANES) - 1, 0), 0)
    return pl.pallas_call(
        functools.partial(_even_out_kernel, sb=sb, lb=lb, nl=nl),
        grid=(steps,),
        in_specs=[pl.BlockSpec((sb, lb, w3), lambda i: (i // nl, i % nl, 0)),
                  pl.BlockSpec((sb, SUBLANES, w3), halo_map if nl > 1 else (lambda i: (i, 0, 0))),
                  pl.BlockSpec((sb, SUBLANES, D_B), lambda i: (i // nl, 0, 0)),
                  pl.BlockSpec((rows, hv), lambda i: (i, 0)),
                  pl.BlockSpec((sb, lb, d), lambda i: (i // nl, i % nl, 0)),
                  pl.BlockSpec((sb, 1, d), lambda i: (i // nl, 0, k_gate)),
                  pl.BlockSpec((CONV_B, D_B), lambda i: (0, 0)),
                  pl.BlockSpec((hv + D_B, d), lambda i: (0, 0))],
        out_specs=[pl.BlockSpec((sb, lb, d), lambda i: (i // nl, i % nl, 0)),
                   pl.BlockSpec((sb, SUBLANES, D_B), lambda i: (i // nl, 0, 0))],
        out_shape=[jax.ShapeDtypeStruct((n_seq, length, d), F32),
                   jax.ShapeDtypeStruct((n_seq, SUBLANES, D_B), F32)],
        scratch_shapes=[pltpu.VMEM((sb, SUBLANES + lb, D_B), F32)],
        compiler_params=_cparams(("arbitrary",), VMEM_LIMIT),
        name="even_out",
    )(gcx3, gcx3, hist8, out_a2, x3, mod3, conv_w, w_out_bf16)


def _proj_res_kernel(a_ref, x_ref, g_ref, w_ref, xo_ref, *, sb, lb):
    y = _bdot(a_ref[...], w_ref[...])
    xo_ref[...] = x_ref[...] + g_ref[...] * y.reshape(sb, lb, y.shape[-1])


def _proj_residual(a2, x3, mod3, k_gate, w_bf16):
    n_seq, length, d = x3.shape
    sb, lb, steps, nl = _tok_tiling(n_seq, length, TOK_TILE)
    rows = sb * lb
    ka = a2.shape[-1]
    return pl.pallas_call(
        functools.partial(_proj_res_kernel, sb=sb, lb=lb),
        grid=(steps,),
        in_specs=[pl.BlockSpec((rows, ka), lambda i: (i, 0)),
                  pl.BlockSpec((sb, lb, d), lambda i: (i // nl, i % nl, 0)),
                  pl.BlockSpec((sb, 1, d), lambda i: (i // nl, 0, k_gate)),
                  pl.BlockSpec((ka, d), lambda i: (0, 0))],
        out_specs=pl.BlockSpec((sb, lb, d), lambda i: (i // nl, i % nl, 0)),
        out_shape=jax.ShapeDtypeStruct((n_seq, length, d), F32),
        compiler_params=_cparams(("arbitrary",), VMEM_LIMIT),
        name="proj_residual",
    )(a2, x3, mod3, w_bf16)


def _router_kernel(x_ref, sh_ref, sc_ref, gain_ref, wrt_ref, bcol_ref, utri_ref,
                   mi_ref, gcol_ref, cnt_ref, carry_ref):
    i = pl.program_id(0)

    @pl.when(i == 0)
    def _():
        carry_ref[...] = jnp.zeros_like(carry_ref)

    h2 = _norm_mod(x_ref[...], gain_ref[...], sc_ref[...], sh_ref[...])
    tm = h2.shape[0]
    logits = _bdot_nt(wrt_ref[...], h2) + bcol_ref[...]
    lg = logits[0:SUBLANES]
    e = jnp.exp(lg - jnp.max(lg, axis=0, keepdims=True))
    pg = e / jnp.sum(e, axis=0, keepdims=True)
    p_top = jnp.max(pg, axis=0, keepdims=True)
    rid8 = lax.broadcasted_iota(I32, (SUBLANES, tm), 0)
    g_top = jnp.min(jnp.where(pg == p_top, rid8, SUBLANES), axis=0, keepdims=True)
    le = logits[SUBLANES:SUBLANES + N_EXPERTS]
    sel = jnp.zeros((E_PER_GROUP, tm), F32)
    for gi in range(N_GROUPS):
        sel = sel + jnp.where(g_top == gi, le[gi * E_PER_GROUP:(gi + 1) * E_PER_GROUP], 0.0)
    e2 = jnp.exp(sel - jnp.max(sel, axis=0, keepdims=True))
    p_in = e2 / jnp.sum(e2, axis=0, keepdims=True)
    w_a = jnp.max(p_in, axis=0, keepdims=True)
    i_a = jnp.min(jnp.where(p_in == w_a, rid8, SUBLANES), axis=0, keepdims=True)
    p_rest = jnp.where(rid8 == i_a, -1.0, p_in)
    w_b = jnp.max(p_rest, axis=0, keepdims=True)
    i_b = jnp.min(jnp.where(p_rest == w_b, rid8, SUBLANES), axis=0, keepdims=True)
    den = w_a + w_b
    gate1 = p_top * (w_a / den)
    gate2 = p_top * (w_b / den)
    ex1 = g_top * E_PER_GROUP + i_a
    ex2 = g_top * E_PER_GROUP + i_b

    rid32 = lax.broadcasted_iota(I32, (N_EXPERTS, tm), 0)
    oh1 = rid32 == ex1
    oh2 = rid32 == ex2
    ohc = jnp.where(oh1, 1.0, jnp.where(oh2, 1.0, 0.0))
    cum = _bdot(ohc, utri_ref[...])
    carry = carry_ref[...]
    base = cum - ohc + carry[:, 0:1]
    r1 = jnp.sum(jnp.where(oh1, base, 0.0), axis=0, keepdims=True)
    r2 = jnp.sum(jnp.where(oh2, base, 0.0), axis=0, keepdims=True)
    new_carry = carry + cum[:, tm - 1:tm]
    carry_ref[...] = new_carry
    cnt_ref[...] = new_carry

    bro = lambda v: jnp.broadcast_to(v, (SUBLANES, tm))
    mi_ref[...] = jnp.where(rid8 == 0, bro(ex1),
                            jnp.where(rid8 == 1, bro(ex2),
                                      jnp.where(rid8 == 2, bro(r1.astype(I32)),
                                                jnp.where(rid8 == 3, bro(r2.astype(I32)), 0))))
    rid128 = lax.broadcasted_iota(I32, (LANES, tm), 0)
    g128 = jnp.where(rid128 == 0, jnp.broadcast_to(gate1, (LANES, tm)),
                     jnp.where(rid128 == 1, jnp.broadcast_to(gate2, (LANES, tm)), 0.0))
    gcol_ref[...] = g128.T


def _moe_router(x3, mod3, k_sh, k_sc, gain, wrt_bf16, bcol):
    n_seq, length, d = x3.shape
    sb, lb, steps, nl = _tok_tiling(n_seq, length, TOK_TILE)
    tm = sb * lb
    t = n_seq * length
    utri = jnp.triu(jnp.ones((tm, tm), F32)).astype(BF16)
    r_rows = wrt_bf16.shape[0]
    return pl.pallas_call(
        _router_kernel,
        grid=(steps,),
        in_specs=[pl.BlockSpec((sb, lb, d), lambda i: (i // nl, i % nl, 0)),
                  pl.BlockSpec((sb, 1, d), lambda i: (i // nl, 0, k_sh)),
                  pl.BlockSpec((sb, 1, d), lambda i: (i // nl, 0, k_sc)),
                  pl.BlockSpec((1, d), lambda i: (0, 0)),
                  pl.BlockSpec((r_rows, d), lambda i: (0, 0)),
                  pl.BlockSpec((r_rows, 1), lambda i: (0, 0)),
                  pl.BlockSpec((tm, tm), lambda i: (0, 0))],
        out_specs=[pl.BlockSpec((SUBLANES, tm), lambda i: (0, i)),
                   pl.BlockSpec((tm, LANES), lambda i: (i, 0)),
                   pl.BlockSpec((N_EXPERTS, LANES), lambda i: (0, 0))],
        out_shape=[jax.ShapeDtypeStruct((SUBLANES, t), I32),
                   jax.ShapeDtypeStruct((t, LANES), F32),
                   jax.ShapeDtypeStruct((N_EXPERTS, LANES), F32)],
        scratch_shapes=[pltpu.VMEM((N_EXPERTS, LANES), F32)],
        compiler_params=_cparams(("arbitrary",), VMEM_LIMIT),
        name="moe_router",
    )(x3, mod3, mod3, gain.reshape(1, d), wrt_bf16, bcol, utri)


def _plan_kernel(mi_ref, cnt_ref, ltri_ref, pos_ref, tinfo_ref, *, n_tiles_pad, row_tile):
    cnt = cnt_ref[...]
    nt = jnp.floor((cnt + (row_tile - 1)) * (1.0 / row_tile))
    tstart = _bdot(ltri_ref[...], nt)
    mi = mi_ref[...]
    tm = mi.shape[1]
    rid32 = lax.broadcasted_iota(I32, (N_EXPERTS, tm), 0)
    ts_col = tstart[:, 0:1]

    def pos_of(ex, rank):
        start = jnp.sum(jnp.where(rid32 == ex, ts_col, 0.0), axis=0, keepdims=True)
        return start.astype(I32) * row_tile + rank

    p1 = pos_of(mi[0:1], mi[2:3])
    p2 = pos_of(mi[1:2], mi[3:4])
    rid8 = lax.broadcasted_iota(I32, (SUBLANES, tm), 0)
    pos_ref[0] = jnp.where(rid8 == 0, jnp.broadcast_to(p1, (SUBLANES, tm)),
                           jnp.where(rid8 == 1, jnp.broadcast_to(p2, (SUBLANES, tm)), 0))
    tend_col = ts_col + nt[:, 0:1]
    jt = lax.broadcasted_iota(I32, (N_EXPERTS, n_tiles_pad), 1).astype(F32)
    te = jnp.sum(jnp.where(tend_col <= jt, 1.0, 0.0), axis=0, keepdims=True)
    te = jnp.minimum(te, N_EXPERTS - 1.0).astype(I32)
    total = jnp.sum(nt[:, 0:1], axis=0, keepdims=True).astype(I32)
    rid8t = lax.broadcasted_iota(I32, (SUBLANES, n_tiles_pad), 0)
    tinfo_ref[...] = jnp.where(rid8t == 0, jnp.broadcast_to(te, (SUBLANES, n_tiles_pad)),
                               jnp.broadcast_to(total, (SUBLANES, n_tiles_pad)))


def _moe_plan(meta_i, counts, n_tiles_pad, row_tile):
    t = meta_i.shape[1]
    tm = MOE_TOK_TILE
    steps = t // tm
    ltri = jnp.tril(jnp.ones((N_EXPERTS, N_EXPERTS), F32), k=-1).astype(BF16)
    return pl.pallas_call(
        functools.partial(_plan_kernel, n_tiles_pad=n_tiles_pad, row_tile=row_tile),
        grid=(steps,),
        in_specs=[pl.BlockSpec((SUBLANES, tm), lambda i: (0, i)),
                  pl.BlockSpec((N_EXPERTS, LANES), lambda i: (0, 0)),
                  pl.BlockSpec((N_EXPERTS, N_EXPERTS), lambda i: (0, 0))],
        out_specs=[pl.BlockSpec((1, SUBLANES, tm), lambda i: (i, 0, 0)),
                   pl.BlockSpec((SUBLANES, n_tiles_pad), lambda i: (0, 0))],
        out_shape=[jax.ShapeDtypeStruct((steps, SUBLANES, tm), I32),
                   jax.ShapeDtypeStruct((SUBLANES, n_tiles_pad), I32)],
        compiler_params=_cparams(("arbitrary",)),
        name="moe_plan",
    )(meta_i, counts, ltri)


def _row_copy_out(buf_ref, slot, r, dst_hbm, p, sem):
    return pltpu.make_async_copy(buf_ref.at[slot, pl.ds(r, 1), :], dst_hbm.at[pl.ds(p, 1), :], sem.at[slot])


def _dispatch_kernel(x_ref, sh_ref, sc_ref, gain_ref, pos_ref, xs_in_ref, xs_ref,
                     buf_ref, idx_ref, sem_ref, isem_ref, *, tm, steps):
    del xs_in_ref
    i = pl.program_id(0)
    slot = i % 2

    def drain(s):
        for _ in range(2):
            pltpu.make_async_copy(buf_ref.at[s], xs_ref.at[pl.ds(0, tm), :], sem_ref.at[s]).wait()

    @pl.when(i >= 2)
    def _():
        drain(slot)

    buf_ref[slot] = _norm_mod(x_ref[...], gain_ref[...], sc_ref[...], sh_ref[...])
    cp = pltpu.make_async_copy(pos_ref.at[0], idx_ref, isem_ref)
    cp.start()
    cp.wait()

    def issue(r, carry):
        _row_copy_out(buf_ref, slot, r, xs_ref, idx_ref[0, r], sem_ref).start()
        _row_copy_out(buf_ref, slot, r, xs_ref, idx_ref[1, r], sem_ref).start()
        return carry
    lax.fori_loop(0, tm, issue, 0, unroll=8)

    @pl.when(i == steps - 1)
    def _():
        drain(slot)
        if steps > 1:
            drain(1 - slot)


def _moe_dispatch(x3, mod3, k_sh, k_sc, gain, pos3, p_rows):
    n_seq, length, d = x3.shape
    sb, lb, steps, nl = _tok_tiling(n_seq, length, MOE_TOK_TILE)
    tm = sb * lb
    xs0 = jnp.zeros((p_rows, d), F32)
    return pl.pallas_call(
        functools.partial(_dispatch_kernel, tm=tm, steps=steps),
        grid=(steps,),
        in_specs=[pl.BlockSpec((sb, lb, d), lambda i: (i // nl, i % nl, 0)),
                  pl.BlockSpec((sb, 1, d), lambda i: (i // nl, 0, k_sh)),
                  pl.BlockSpec((sb, 1, d), lambda i: (i // nl, 0, k_sc)),
                  pl.BlockSpec((1, d), lambda i: (0, 0)),
                  pl.BlockSpec((1, SUBLANES, tm), lambda i: (i, 0, 0)),
                  pl.BlockSpec(memory_space=pl.ANY)],
        out_specs=pl.BlockSpec(memory_space=pl.ANY),
        out_shape=jax.ShapeDtypeStruct((p_rows, d), F32),
        scratch_shapes=[pltpu.VMEM((2, tm, d), F32),
                        pltpu.SMEM((SUBLANES, tm), I32),
                        pltpu.SemaphoreType.DMA((2,)),
                        pltpu.SemaphoreType.DMA(())],
        input_output_aliases={5: 0},
        compiler_params=_cparams(("arbitrary",), VMEM_LIMIT),
        name="moe_dispatch",
    )(x3, mod3, mod3, gain.reshape(1, d), pos3, xs0)


def _expert_kernel(te_ref, nt_ref, xs_ref, wg_ref, wu_ref, wd_ref, ys_ref):
    i = pl.program_id(0)

    @pl.when(i < nt_ref[0])
    def _():
        x = xs_ref[...].astype(BF16)
        hid = _silu(jnp.dot(x, wg_ref[0], preferred_element_type=F32)) \
            * jnp.dot(x, wu_ref[0], preferred_element_type=F32)
        ys_ref[...] = jnp.dot(hid.astype(BF16), wd_ref[0], preferred_element_type=F32)

    @pl.when(i >= nt_ref[0])
    def _():
        ys_ref[...] = jnp.zeros_like(ys_ref)


def _moe_experts(xs, tile_expert, n_tiles, wg, wu, wd, row_tile):
    p_rows, d = xs.shape
    f = wg.shape[-1]
    steps = p_rows // row_tile
    grid_spec = pltpu.PrefetchScalarGridSpec(
        num_scalar_prefetch=2,
        grid=(steps,),
        in_specs=[pl.BlockSpec((row_tile, d), lambda i, te, nt: (i, 0)),
                  pl.BlockSpec((1, d, f), lambda i, te, nt: (te[i], 0, 0)),
                  pl.BlockSpec((1, d, f), lambda i, te, nt: (te[i], 0, 0)),
                  pl.BlockSpec((1, f, d), lambda i, te, nt: (te[i], 0, 0))],
        out_specs=pl.BlockSpec((row_tile, d), lambda i, te, nt: (i, 0)),
    )
    return pl.pallas_call(
        _expert_kernel,
        grid_spec=grid_spec,
        out_shape=jax.ShapeDtypeStruct((p_rows, d), F32),
        compiler_params=_cparams(("arbitrary",), VMEM_LIMIT),
        name="moe_experts",
    )(tile_expert, n_tiles, xs, wg, wu, wd)


def _row_copy_in(src_hbm, p, buf_ref, slot, which, r, sem):
    return pltpu.make_async_copy(src_hbm.at[pl.ds(p, 1), :], buf_ref.at[slot, which, pl.ds(r, 1), :],
                                 sem.at[slot])


def _combine_kernel(x_ref, g_ref, gcol_ref, pos_ref, posn_ref, ys_ref, gainf_ref, xo_ref,
                    buf_ref, idx_ref, sem_ref, isem_ref, *, tm, steps, sb, lb, final_norm):
    i = pl.program_id(0)
    slot = i % 2

    def issue(pref, s):
        cp = pltpu.make_async_copy(pref.at[0], idx_ref, isem_ref)
        cp.start()
        cp.wait()

        def body(r, carry):
            _row_copy_in(ys_ref, idx_ref[0, r], buf_ref, s, 0, r, sem_ref).start()
            _row_copy_in(ys_ref, idx_ref[1, r], buf_ref, s, 1, r, sem_ref).start()
            return carry
        lax.fori_loop(0, tm, body, 0, unroll=8)

    @pl.when(i == 0)
    def _():
        issue(pos_ref, slot)

    @pl.when(i + 1 < steps)
    def _():
        issue(posn_ref, 1 - slot)

    for which in range(2):
        pltpu.make_async_copy(ys_ref.at[pl.ds(0, tm), :], buf_ref.at[slot, which], sem_ref.at[slot]).wait()

    gc = gcol_ref[...]
    y = gc[:, 0:1] * buf_ref[slot, 0] + gc[:, 1:2] * buf_ref[slot, 1]
    d = y.shape[-1]
    xn = x_ref[...] + g_ref[...] * y.reshape(sb, lb, d)
    if final_norm:
        xn = xn * lax.rsqrt(jnp.mean(xn * xn, axis=-1, keepdims=True) + EPS) * gainf_ref[...]
    xo_ref[...] = xn


def _moe_combine(x3, mod3, k_gate, gates_col, pos3, ys, gain_f, final_norm):
    n_seq, length, d = x3.shape
    sb, lb, steps, nl = _tok_tiling(n_seq, length, MOE_TOK_TILE)
    tm = sb * lb
    return pl.pallas_call(
        functools.partial(_combine_kernel, tm=tm, steps=steps, sb=sb, lb=lb, final_norm=final_norm),
        grid=(steps,),
        in_specs=[pl.BlockSpec((sb, lb, d), lambda i: (i // nl, i % nl, 0)),
                  pl.BlockSpec((sb, 1, d), lambda i: (i // nl, 0, k_gate)),
                  pl.BlockSpec((tm, LANES), lambda i: (i, 0)),
                  pl.BlockSpec((1, SUBLANES, tm), lambda i: (i, 0, 0)),
                  pl.BlockSpec((1, SUBLANES, tm), lambda i: (jnp.minimum(i + 1, steps - 1), 0, 0)),
                  pl.BlockSpec(memory_space=pl.ANY),
                  pl.BlockSpec((1, d), lambda i: (0, 0))],
        out_specs=pl.BlockSpec((sb, lb, d), lambda i: (i // nl, i % nl, 0)),
        out_shape=jax.ShapeDtypeStruct((n_seq, length, d), F32),
        scratch_shapes=[pltpu.VMEM((2, 2, tm, d), F32),
                        pltpu.SMEM((SUBLANES, tm), I32),
                        pltpu.SemaphoreType.DMA((2,)),
                        pltpu.SemaphoreType.DMA(())],
        compiler_params=_cparams(("arbitrary",), VMEM_LIMIT),
        name="moe_combine",
    )(x3, mod3, gates_col, pos3, pos3, ys, gain_f.reshape(1, d))


def _hier_moe_residual(x3, mod3, k_sh, k_sc, k_gate, gain, wrt_bf16, bcol, wg, wu, wd, gain_f, final_norm):
    n_seq, length, d = x3.shape
    t = n_seq * length
    row_tile = MOE_ROW_TILE_BIG if (2 * t) // N_EXPERTS >= MOE_ROW_TILE_BIG else MOE_ROW_TILE_SMALL
    max_tiles = (2 * t) // row_tile + N_EXPERTS
    n_tiles_pad = -(-max_tiles // LANES) * LANES
    meta_i, gates_col, counts = _moe_router(x3, mod3, k_sh, k_sc, gain, wrt_bf16, bcol)
    pos3, tinfo = _moe_plan(meta_i, counts, n_tiles_pad, row_tile)
    xs = _moe_dispatch(x3, mod3, k_sh, k_sc, gain, pos3, max_tiles * row_tile)
    ys = _moe_experts(xs, tinfo[0, :max_tiles], tinfo[1, :1], wg, wu, wd, row_tile)
    return _moe_combine(x3, mod3, k_gate, gates_col, pos3, ys, gain_f, final_norm)


def _multiplicity(dist):
    m = jnp.zeros(dist.shape, F32)
    for window, dil in BRANCHES:
        m = m + ((dist >= 0) & (dist <= window) & (dist % dil == 0)).astype(F32)
    return m


def _log_mult(dist):
    m = _multiplicity(dist)
    return jnp.where(m > 0, jnp.log(jnp.maximum(m, 1.0)), NEG_BIG)


def _attn_prompt_kernel(q_ref, k_ref, v_ref, nbias_ref, fbias_ref, o_ref, mf_ref, lf_ref, af_ref, *, length):
    blk = ATT_BLK
    nq = length // blk
    near = ATT_NEAR_BLOCKS
    ncls = BRANCHES[-1][1]
    csz = length // ncls
    has_far = nq > near
    scale = (HD_C ** -0.5) * LOG2_E
    nt = lambda a, b: lax.dot_general(a, b, (((1,), (1,)), ((), ())), preferred_element_type=F32)
    mm = lambda a, b: jnp.dot(a, b, preferred_element_type=F32)

    def split_heads(q):
        head0 = lax.broadcasted_iota(I32, q.shape, 1) < HD_C
        return head0, jnp.where(head0, q, 0.0).astype(BF16), jnp.where(head0, 0.0, q).astype(BF16)

    if has_far:
        fbias = fbias_ref[...]
        group = 4
        for r0 in range(0, ncls, group):
            rs = list(range(r0, r0 + group))
            rows = [pl.ds(r, csz, stride=ncls) for r in rs]
            hq = [split_heads(q_ref[0, rw, :] * scale) for rw in rows]
            kr = [k_ref[0, rw, :].astype(BF16) for rw in rows]
            vr = [v_ref[0, rw, :].astype(BF16) for rw in rows]
            s0 = [nt(h[1], k) + fbias for h, k in zip(hq, kr)]
            s1 = [nt(h[2], k) + fbias for h, k in zip(hq, kr)]
            m0 = [jnp.max(s, axis=-1, keepdims=True) for s in s0]
            m1 = [jnp.max(s, axis=-1, keepdims=True) for s in s1]
            p0 = [jnp.exp2(s - m) for s, m in zip(s0, m0)]
            p1 = [jnp.exp2(s - m) for s, m in zip(s1, m1)]
            a0 = [mm(p.astype(BF16), v) for p, v in zip(p0, vr)]
            a1 = [mm(p.astype(BF16), v) for p, v in zip(p1, vr)]
            for i, rw in enumerate(rows):
                head0 = hq[i][0]
                af_ref[rw, :] = jnp.where(head0, a0[i], a1[i])
                mf_ref[rw, :] = jnp.where(head0, m0[i], m1[i])
                lf_ref[rw, :] = jnp.where(head0, jnp.sum(p0[i], axis=-1, keepdims=True),
                                          jnp.sum(p1[i], axis=-1, keepdims=True))

    for qi in range(nq):
        rq = slice(qi * blk, (qi + 1) * blk)
        head0, q0, q1 = split_heads(q_ref[0, rq, :] * scale)
        js = list(range(max(0, qi - near + 1), qi + 1))
        ks = [k_ref[0, kj * blk:(kj + 1) * blk, :].astype(BF16) for kj in js]
        vs = [v_ref[0, kj * blk:(kj + 1) * blk, :].astype(BF16) for kj in js]
        s0 = [nt(q0, k) + nbias_ref[qi - kj] for k, kj in zip(ks, js)]
        s1 = [nt(q1, k) + nbias_ref[qi - kj] for k, kj in zip(ks, js)]
        far = has_far and qi >= near
        m0 = functools.reduce(jnp.maximum, [jnp.max(s, axis=-1, keepdims=True) for s in s0])
        m1 = functools.reduce(jnp.maximum, [jnp.max(s, axis=-1, keepdims=True) for s in s1])
        if far:
            mf = mf_ref[rq, :]
            mf0, mf1 = mf[:, 0:1], mf[:, HD_C:HD_C + 1]
            m0 = jnp.maximum(m0, mf0)
            m1 = jnp.maximum(m1, mf1)
        p0 = [jnp.exp2(s - m0) for s in s0]
        p1 = [jnp.exp2(s - m1) for s in s1]
        l0 = sum(jnp.sum(p, axis=-1, keepdims=True) for p in p0)
        l1 = sum(jnp.sum(p, axis=-1, keepdims=True) for p in p1)
        a0 = sum(mm(p.astype(BF16), v) for p, v in zip(p0, vs))
        a1 = sum(mm(p.astype(BF16), v) for p, v in zip(p1, vs))
        acc = jnp.where(head0, a0, a1)
        den = jnp.where(head0, l0, l1)
        if far:
            w_far = jnp.where(head0, jnp.exp2(mf0 - m0), jnp.exp2(mf1 - m1))
            acc = acc + af_ref[rq, :] * w_far
            den = den + lf_ref[rq, :] * w_far
        o_ref[0, rq, :] = acc / den


def _attn_prompt(qkv3):
    n_seq, length, w3 = qkv3.shape
    d = w3 // 3
    pairs = d // LANES
    blk = ATT_BLK
    near = ATT_NEAR_BLOCKS
    ncls = BRANCHES[-1][1]
    csz = length // ncls
    per = blk // ncls
    assert (near - 1) * blk >= BRANCHES[-2][0] and length % blk == 0 and blk % ncls == 0
    a = jnp.arange(blk)
    dist = (jnp.arange(near)[:, None, None] * blk) + a[None, :, None] - a[None, None, :]
    nbias = _log_mult(dist) * jnp.where(_multiplicity(dist) > 0, LOG2_E, 1.0)
    m = jnp.arange(csz)
    far_ok = ((m[:, None] // per - m[None, :] // per) >= near) & \
        ((m[:, None] - m[None, :]) * ncls <= BRANCHES[-1][0])
    fbias = jnp.where(far_ok, 0.0, NEG_BIG).astype(F32)
    return pl.pallas_call(
        functools.partial(_attn_prompt_kernel, length=length),
        grid=(n_seq, pairs),
        in_specs=[pl.BlockSpec((1, length, LANES), lambda n, p: (n, 0, p)),
                  pl.BlockSpec((1, length, LANES), lambda n, p: (n, 0, pairs + p)),
                  pl.BlockSpec((1, length, LANES), lambda n, p: (n, 0, 2 * pairs + p)),
                  pl.BlockSpec((near, blk, blk), lambda n, p: (0, 0, 0)),
                  pl.BlockSpec((csz, csz), lambda n, p: (0, 0))],
        out_specs=pl.BlockSpec((1, length, LANES), lambda n, p: (n, 0, p)),
        out_shape=jax.ShapeDtypeStruct((n_seq, length, d), F32),
        scratch_shapes=[pltpu.VMEM((length, LANES), F32)] * 3,
        compiler_params=_cparams(("arbitrary", "arbitrary"), VMEM_LIMIT),
        name="attn_prompt",
    )(qkv3, qkv3, qkv3, nbias, fbias)


def _attn_sample_kernel(q_ref, kn_ref, vn_ref, kt_ref, vt_ref, bias_ref, biasn_ref, o_ref, *, heads):
    q = q_ref[0] * (HD_C ** -0.5)
    kn = kn_ref[0]
    vn = vn_ref[0]
    bias = bias_ref[...]
    biasn = biasn_ref[...]
    outs = []
    for h in range(heads):
        lo, hi = h * HD_C, (h + 1) * HD_C
        qh = q[:, lo:hi].astype(BF16)
        s = jnp.dot(qh, kt_ref[0, h].astype(BF16), preferred_element_type=F32) + bias
        sn = _bdot_nt(qh, kn[:, lo:hi]) + biasn
        m = jnp.maximum(jnp.max(s, axis=-1, keepdims=True), jnp.max(sn, axis=-1, keepdims=True))
        p = jnp.exp(s - m)
        pn = jnp.exp(sn - m)
        den = jnp.sum(p, axis=-1, keepdims=True) + jnp.sum(pn, axis=-1, keepdims=True)
        o = _bdot_nt(p, vt_ref[0, h]) + _bdot(pn, vn[:, lo:hi])
        outs.append(o / den)
    o_ref[0] = jnp.concatenate(outs, axis=-1)


def _attn_sample(qkv3, k_cache, v_cache):
    n_seq, t_len, w3 = qkv3.shape
    d = w3 // 3
    w_buf = k_cache.shape[1]
    kt = jnp.transpose(k_cache, (0, 2, 3, 1))
    vt = jnp.transpose(v_cache, (0, 2, 3, 1))
    heads = SAMPLE_HEADS_PER_STEP
    hw = heads * HD_C
    nj = H_C // heads
    tpos = w_buf + jnp.arange(t_len)
    bias = _log_mult(tpos[:, None] - jnp.arange(w_buf)[None, :])
    biasn = _log_mult(jnp.arange(t_len)[:, None] - jnp.arange(t_len)[None, :])
    return pl.pallas_call(
        functools.partial(_attn_sample_kernel, heads=heads),
        grid=(n_seq, nj),
        in_specs=[pl.BlockSpec((1, t_len, hw), lambda n, j: (n, 0, j)),
                  pl.BlockSpec((1, t_len, hw), lambda n, j: (n, 0, nj + j)),
                  pl.BlockSpec((1, t_len, hw), lambda n, j: (n, 0, 2 * nj + j)),
                  pl.BlockSpec((1, heads, HD_C, w_buf), lambda n, j: (n, j, 0, 0)),
                  pl.BlockSpec((1, heads, HD_C, w_buf), lambda n, j: (n, j, 0, 0)),
                  pl.BlockSpec((t_len, w_buf), lambda n, j: (0, 0)),
                  pl.BlockSpec((t_len, t_len), lambda n, j: (0, 0))],
        out_specs=pl.BlockSpec((1, t_len, hw), lambda n, j: (n, 0, j)),
        out_shape=jax.ShapeDtypeStruct((n_seq, t_len, d), F32),
        compiler_params=_cparams(("arbitrary", "arbitrary"), VMEM_LIMIT),
        name="attn_sample",
    )(qkv3, qkv3, qkv3, kt, vt, bias, biasn)


def _pad_hist(hist, n_seq, width):
    k1 = hist.shape[1]
    return jnp.concatenate([jnp.zeros((n_seq, SUBLANES - k1, width), F32), hist.astype(F32)], axis=1)


def _prep_weights(p):
    d = p['w_in_even'].shape[1]
    out = {}
    w_in = p['w_in_even'][0]
    hv = H_A * DV_A
    o_z = QKV_A
    o_a = o_z + hv
    o_g = o_a + 2 * H_A
    ab_pad = jnp.zeros((d, LANES - 2 * H_A), F32)
    out['w_in'] = jnp.concatenate([w_in[:, :o_a], w_in[:, o_a:o_g], ab_pad, w_in[:, o_g:]], axis=1).astype(BF16)
    out['w_out_even'] = p['w_out_even'][0].astype(BF16)
    out['w_qkv'] = p['w_qkv_odd'][0].astype(BF16)
    out['w_out_odd'] = p['w_out_odd'][0].astype(BF16)
    depth = p['w_router_group'].shape[0]
    wrt, bcol, wg, wu, wd = [], [], [], [], []
    for l in range(depth):
        pad_w = jnp.zeros((SUBLANES - N_GROUPS, d), F32)
        wrt.append(jnp.concatenate([p['w_router_group'][l].T, pad_w, p['w_router_expert'][l].T], axis=0).astype(BF16))
        pad_b = jnp.full((SUBLANES - N_GROUPS,), NEG_BIG, F32)
        bcol.append(jnp.concatenate([p['b_router_group'][l], pad_b, p['b_router_expert'][l]]).reshape(-1, 1))
        f = p['w_exp_gate'].shape[-1]
        wg.append(p['w_exp_gate'][l].reshape(N_EXPERTS, d, f).astype(BF16))
        wu.append(p['w_exp_up'][l].reshape(N_EXPERTS, d, f).astype(BF16))
        wd.append(p['w_exp_down'][l].reshape(N_EXPERTS, f, d).astype(BF16))
    out.update(wrt=wrt, bcol=bcol, wg=wg, wu=wu, wd=wd)
    return out


def _trunk(x3, mod, a_ssm, a_conv, b_conv, kv_k, kv_v, p, w, sample):
    n_seq, length, d = x3.shape
    t = n_seq * length
    hv = H_A * DV_A
    mod3 = [m.reshape(n_seq, 1, 6 * d) for m in mod]

    qkv_pre, z, ab, gcx = _norm_mod_matmul(x3, mod3[0], 0, 1, p['norm1'][0], w['w_in'],
                                           (QKV_A, hv, LANES, 3 * D_B))
    chunk = min(CHUNK_PROMPT, length)
    g_chunks = 4
    qkv_pre3 = qkv_pre.reshape(n_seq, length, QKV_A)
    ub, wm, qd, kd, qk, dec = _delta_prep(qkv_pre3, _pad_hist(a_conv[0], n_seq, QKV_A),
                                          ab.reshape(n_seq, length, LANES), p['conv_a_w'][0],
                                          p['a_log'][0], p['dt_bias'][0], chunk, g_chunks)
    out_a, s_new = _delta_scan(ub, wm, qd, kd, qk, dec, z.reshape(n_seq, length, hv),
                               a_ssm[0].astype(F32), p['o_gain_a'][0], chunk)
    x3, bx_tail = _even_out(gcx.reshape(n_seq, length, 3 * D_B), _pad_hist(b_conv[0], n_seq, D_B),
                            out_a.reshape(t, hv), x3, mod3[0], 2, p['conv_b_w'][0], w['w_out_even'])
    new_aconv = jnp.concatenate([a_conv[0].astype(F32), qkv_pre3], axis=1)[:, -(CONV_A - 1):]
    new_bconv = jnp.concatenate([b_conv[0].astype(F32), bx_tail], axis=1)[:, -(CONV_B - 1):]
    x3 = _hier_moe_residual(x3, mod3[0], 3, 4, 5, p['norm2'][0], w['wrt'][0], w['bcol'][0],
                            w['wg'][0], w['wu'][0], w['wd'][0], p['norm_f'], False)

    (qkv,) = _norm_mod_matmul(x3, mod3[1], 0, 1, p['norm1'][1], w['w_qkv'], (3 * d,))
    qkv3 = qkv.reshape(n_seq, length, 3 * d)
    k_new = qkv3[:, :, d:2 * d]
    v_new = qkv3[:, :, 2 * d:]
    if sample:
        attn = _attn_sample(qkv3, kv_k[0], kv_v[0])
        keep = length
    else:
        attn = _attn_prompt(qkv3)
        keep = min(BRANCHES[-1][0], length)
    new_k = k_new[:, length - keep:].reshape(n_seq, keep, H_C, HD_C)
    new_v = v_new[:, length - keep:].reshape(n_seq, keep, H_C, HD_C)
    x3 = _proj_residual(attn.reshape(t, d), x3, mod3[1], 2, w['w_out_odd'])
    y = _hier_moe_residual(x3, mod3[1], 3, 4, 5, p['norm2'][1], w['wrt'][1], w['bcol'][1],
                           w['wg'][1], w['wu'][1], w['wd'][1], p['norm_f'], True)
    return y, s_new[None], new_aconv[None], new_bconv[None], new_k[None], new_v[None]


def kernel(x_prompt, x_sample, state_a_ssm, state_a_conv, state_b_conv, cache_c_k, cache_c_v, c_prompt, c_sample, w_ada, b_ada, norm1, norm2, norm_f, w_in_even, conv_a_w, a_log, dt_bias, o_gain_a, conv_b_w, w_out_even, w_qkv_odd, w_out_odd, w_router_group, b_router_group, w_router_expert, b_router_expert, w_exp_gate, w_exp_up, w_exp_down):
    p = dict(norm1=norm1, norm2=norm2, norm_f=norm_f, w_in_even=w_in_even, conv_a_w=conv_a_w,
             a_log=a_log, dt_bias=dt_bias, o_gain_a=o_gain_a, conv_b_w=conv_b_w, w_out_even=w_out_even,
             w_qkv_odd=w_qkv_odd, w_out_odd=w_out_odd, w_router_group=w_router_group,
             b_router_group=b_router_group, w_router_expert=w_router_expert,
             b_router_expert=b_router_expert, w_exp_gate=w_exp_gate, w_exp_up=w_exp_up,
             w_exp_down=w_exp_down)
    w = _prep_weights(p)
    nb = x_prompt.shape[0]
    ns = x_sample.shape[0]
    mod_all = _ada_mod(jnp.concatenate([c_prompt, c_sample], axis=0), w_ada, b_ada)
    depth = w_ada.shape[0]
    mod_p = [mod_all[l, :nb] for l in range(depth)]
    mod_s = [mod_all[l, nb:] for l in range(depth)]
    n_even = state_a_ssm.shape[0]
    zero_ssm = jnp.zeros((n_even, nb, H_A, DK_A, DV_A), F32)
    zero_aconv = jnp.zeros((n_even, nb, CONV_A - 1, QKV_A), F32)
    zero_bconv = jnp.zeros((n_even, nb, CONV_B - 1, D_B), F32)
    y_p, ssm_p, aconv_p, bconv_p, k_p, v_p = _trunk(x_prompt, mod_p, zero_ssm, zero_aconv, zero_bconv,
                                                    None, None, p, w, False)
    y_s, ssm_s, aconv_s, bconv_s, k_s, v_s = _trunk(x_sample, mod_s, state_a_ssm, state_a_conv,
                                                    state_b_conv, cache_c_k, cache_c_v, p, w, True)
    return (y_p, y_s, ssm_p, ssm_s, aconv_p, aconv_s, bconv_p, bconv_s, k_p, k_s, v_p, v_s)
```

```python
import functools
import math

import jax
import jax.numpy as jnp
from jax import lax
from jax.experimental import pallas as pl
from jax.experimental.pallas import tpu as pltpu

F32 = jnp.float32
BF16 = jnp.bfloat16
I32 = jnp.int32

EPS = 1e-6
NEG_BIG = -1e30

LANES = 128
SUBLANES = 8
VMEM_BYTES_V7X = 64 * 1024 * 1024
VMEM_LIMIT = 56 * 1024 * 1024

H_A, DK_A, DV_A = 4, 128, 128
CONV_A, CONV_B = 4, 3
QKV_A = H_A * (2 * DK_A + DV_A)
D_B = 512
H_C, HD_C = 16, 64
BRANCHES = ((128, 1), (512, 4), (2048, 16))
N_GROUPS, E_PER_GROUP = 4, 8
N_EXPERTS = N_GROUPS * E_PER_GROUP
CHUNK_PROMPT = 64

TOK_TILE = 512
MOE_TOK_TILE = 512
MOE_ROW_TILE_BIG = 512
MOE_ROW_TILE_SMALL = 128
ATT_BLK = 256
ATT_NEAR_BLOCKS = 3
LOG2_E = math.log2(math.e)
SEQ_BLOCK_SCAN = 8
SAMPLE_HEADS_PER_STEP = 16


def _cparams(sem, vmem=None):
    return pltpu.CompilerParams(dimension_semantics=sem, vmem_limit_bytes=vmem)


def _silu(x):
    return x * jax.nn.sigmoid(x)


def _bdot(a, b):
    return jnp.dot(a.astype(BF16), b.astype(BF16), preferred_element_type=F32)


def _bdot_nt(a, b):
    return lax.dot_general(a.astype(BF16), b.astype(BF16), (((1,), (1,)), ((), ())),
                           preferred_element_type=F32)


def _bdot_tn(a, b):
    return lax.dot_general(a.astype(BF16), b.astype(BF16), (((0,), (0,)), ((), ())),
                           preferred_element_type=F32)


def _split_bf16(x):
    hi = x.astype(BF16)
    return hi, (x - hi.astype(F32)).astype(BF16)


def _dot3(a, b):
    ah, al = _split_bf16(a)
    bh, bl = _split_bf16(b)
    dot = functools.partial(jnp.dot, preferred_element_type=F32)
    return dot(ah, bh) + (dot(ah, bl) + dot(al, bh))


def _fdot(a, b):
    return jnp.dot(a, b, preferred_element_type=F32, precision=lax.Precision.HIGHEST)


def _fdot_nt(a, b):
    return lax.dot_general(a, b, (((1,), (1,)), ((), ())), preferred_element_type=F32,
                           precision=lax.Precision.HIGHEST)


def _tok_tiling(n_seq, length, tile):
    if length >= tile:
        sb, lb = 1, tile
    else:
        sb, lb = tile // length, length
    nl = length // lb
    return sb, lb, (n_seq // sb) * nl, nl


def _norm_mod(x, gain, sc, sh):
    sb, lb, d = x.shape
    y = x * lax.rsqrt(jnp.mean(x * x, axis=-1, keepdims=True) + EPS) * gain
    h = y * (1.0 + sc) + sh
    return h.reshape(sb * lb, d)


def _ada_kernel(c_ref, w_ref, b_ref, o_ref):
    c = _silu(c_ref[...])
    o_ref[0] = _bdot(c, w_ref[0]) + b_ref[0]


def _ada_mod(c_all, w_ada, b_ada):
    r, d = c_all.shape
    depth, _, n6 = w_ada.shape
    tn = 1536
    return pl.pallas_call(
        _ada_kernel,
        grid=(depth, n6 // tn),
        in_specs=[pl.BlockSpec((r, d), lambda l, j: (0, 0)),
                  pl.BlockSpec((1, d, tn), lambda l, j: (l, 0, j)),
                  pl.BlockSpec((1, 1, tn), lambda l, j: (l, 0, j))],
        out_specs=pl.BlockSpec((1, r, tn), lambda l, j: (l, 0, j)),
        out_shape=jax.ShapeDtypeStruct((depth, r, n6), F32),
        compiler_params=_cparams(("arbitrary", "arbitrary"), VMEM_LIMIT),
        name="ada_mod",
    )(c_all, w_ada, b_ada.reshape(depth, 1, n6))


def _nmm_kernel(x_ref, sh_ref, sc_ref, gain_ref, w_ref, *o_refs, splits):
    h = _norm_mod(x_ref[...], gain_ref[...], sc_ref[...], sh_ref[...]).astype(BF16)
    off = 0
    for o_ref, n in zip(o_refs, splits):
        o_ref[...] = jnp.dot(h, w_ref[:, off:off + n], preferred_element_type=F32)
        off += n


def _norm_mod_matmul(x3, mod3, k_sh, k_sc, gain, w_bf16, splits):
    n_seq, length, d = x3.shape
    sb, lb, steps, nl = _tok_tiling(n_seq, length, TOK_TILE)
    rows = sb * lb
    t = n_seq * length
    n_out = w_bf16.shape[1]
    assert sum(splits) == n_out
    return pl.pallas_call(
        functools.partial(_nmm_kernel, splits=splits),
        grid=(steps,),
        in_specs=[pl.BlockSpec((sb, lb, d), lambda i: (i // nl, i % nl, 0)),
                  pl.BlockSpec((sb, 1, d), lambda i: (i // nl, 0, k_sh)),
                  pl.BlockSpec((sb, 1, d), lambda i: (i // nl, 0, k_sc)),
                  pl.BlockSpec((1, d), lambda i: (0, 0)),
                  pl.BlockSpec((d, n_out), lambda i: (0, 0))],
        out_specs=[pl.BlockSpec((rows, n), lambda i: (i, 0)) for n in splits],
        out_shape=[jax.ShapeDtypeStruct((t, n), F32) for n in splits],
        compiler_params=_cparams(("arbitrary",), VMEM_LIMIT),
        name="norm_mod_matmul",
    )(x3, mod3, mod3, gain.reshape(1, d), w_bf16)


def _delta_prep_kernel(qkv_ref, halo_ref, hist_ref, ab_ref, cw_ref, al_ref, dtb_ref, lt_ref, sel_ref,
                       ub_ref, wm_ref, qd_ref, kd_ref, qk_ref, dec_ref, ext_ref,
                       *, g_chunks, chunk, chunks_are_seqs, chunks_per_seq):
    c = chunk
    step = pl.program_id(0)
    cw = cw_ref[...]
    row = lax.broadcasted_iota(I32, (c, c), 0)
    col = lax.broadcasted_iota(I32, (c, c), 1)
    lower_incl = row >= col
    strict = row > col
    eye = (row == col).astype(F32)
    n_fac = max(int(math.ceil(math.log2(c))), 1)

    qkvs, abs_ = [], []
    for g in range(g_chunks):
        cur = qkv_ref[g] if chunks_are_seqs else qkv_ref[0, g * c:(g + 1) * c, :]
        if chunks_are_seqs:
            prev = hist_ref[g]
        elif g == 0:
            first = (step % (chunks_per_seq // g_chunks)) == 0
            prev = jnp.where(first, hist_ref[0], halo_ref[0])
        else:
            prev = qkv_ref[0, g * c - SUBLANES:g * c, :]
        ext_ref[g, 0:SUBLANES, :] = prev
        ext_ref[g, SUBLANES:SUBLANES + c, :] = cur
        conv = ext_ref[g, pl.ds(SUBLANES - (CONV_A - 1), c), :] * cw[0:1, :]
        for j in range(1, CONV_A):
            conv = conv + ext_ref[g, pl.ds(SUBLANES - (CONV_A - 1) + j, c), :] * cw[j:j + 1, :]
        qkvs.append(_silu(conv))
        abs_.append(ab_ref[g] if chunks_are_seqs else ab_ref[0, g * c:(g + 1) * c, :])

    g_alls = [-jnp.exp(al_ref[...]) * jax.nn.softplus(ab + dtb_ref[...]) for ab in abs_]
    sigs = [jax.nn.sigmoid(ab) for ab in abs_]
    gcum_alls = [_fdot(lt_ref[...], ga) for ga in g_alls]
    gcum_rows = [_fdot_nt(sel_ref[...], gc) for gc in gcum_alls]

    chains = [(g, h) for g in range(g_chunks) for h in range(H_A)]
    qs, ks, vs, betas, gcs, gammas, egs = [], [], [], [], [], [], []
    for g, h in chains:
        qkv = qkvs[g]
        q = qkv[:, h * DK_A:(h + 1) * DK_A]
        k = qkv[:, H_A * DK_A + h * DK_A:H_A * DK_A + (h + 1) * DK_A]
        v = qkv[:, 2 * H_A * DK_A + h * DV_A:2 * H_A * DK_A + (h + 1) * DV_A]
        qs.append(q * lax.rsqrt(jnp.sum(q * q, axis=-1, keepdims=True) + EPS) * (DK_A ** -0.5))
        ks.append(k * lax.rsqrt(jnp.sum(k * k, axis=-1, keepdims=True) + EPS))
        vs.append(v)
        betas.append(sigs[g][:, H_A + h:H_A + h + 1])
        gc = gcum_alls[g][:, h:h + 1]
        gr = gcum_rows[g][h:h + 1, :]
        gcs.append(gc)
        gammas.append(jnp.exp(jnp.where(lower_incl, gc - gr, NEG_BIG)))
        egs.append(jnp.exp(gc))

    kq = [_bdot_nt(jnp.concatenate([k, q], axis=0), k) for k, q in zip(ks, qs)]
    a_mats = [jnp.where(strict, b * x[0:c] * gm, 0.0) for b, x, gm in zip(betas, kq, gammas)]
    qkms = [x[c:2 * c] * gm for x, gm in zip(kq, gammas)]
    m_pows = [-a for a in a_mats]
    t_invs = [eye + m for m in m_pows]
    for _ in range(n_fac - 1):
        m_pows = [_bdot(m, m) for m in m_pows]
        t_invs = [t + _bdot(t, m) for t, m in zip(t_invs, m_pows)]
    resids = [eye - t - _dot3(a, t) for a, t in zip(a_mats, t_invs)]
    t_invs = [t + _bdot(t, r) for t, r in zip(t_invs, resids)]
    rhss = [jnp.concatenate([v * b, k * (b * eg)], axis=-1) for v, k, b, eg in zip(vs, ks, betas, egs)]
    sols = [rhs + _dot3(t - eye, rhs) for t, rhs in zip(t_invs, rhss)]

    for g in range(g_chunks):
        idx = [i for i, (gg, _) in enumerate(chains) if gg == g]
        g_last = [gcum_alls[g][c - 1:c, h:h + 1] for h in range(H_A)]

        def put(ref, parts):
            val = jnp.concatenate(parts, axis=-1)
            if chunks_are_seqs:
                ref[g] = val
            else:
                ref[0, g * c:(g + 1) * c, :] = val

        put(ub_ref, [sols[i][:, :DV_A] for i in idx])
        put(wm_ref, [sols[i][:, DV_A:] for i in idx])
        put(qd_ref, [qs[i] * egs[i] for i in idx])
        put(kd_ref, [ks[i] * jnp.exp(g_last[h] - gcs[i]) for h, i in enumerate(idx)])
        put(qk_ref, [qkms[i] for i in idx])
        dec_ref[g] = jnp.concatenate([jnp.broadcast_to(jnp.exp(gl), (SUBLANES, DV_A)) for gl in g_last], axis=-1)


def _delta_prep(qkv_pre3, hist8, ab3, conv_w, a_log, dt_bias, chunk, g_chunks):
    n_seq, length, w = qkv_pre3.shape
    c = chunk
    chunks_are_seqs = (length == c)
    nc = length // c
    if chunks_are_seqs:
        steps = n_seq // g_chunks
        blk = (g_chunks, c, w)
        x_map = lambda i: (i, 0, 0)
        halo_map = lambda i: (i, 0, 0)
        hist_spec = pl.BlockSpec((g_chunks, SUBLANES, w), lambda i: (i, 0, 0))
        ab_spec = pl.BlockSpec((g_chunks, c, LANES), lambda i: (i, 0, 0))
        out_map = lambda i: (i, 0, 0)
        out_rows = (g_chunks, c)
    else:
        rows = g_chunks * c
        spb = nc // g_chunks
        steps = n_seq * spb
        blk = (1, rows, w)
        x_map = lambda i: (i // spb, i % spb, 0)
        halo_map = lambda i: (i // spb, jnp.maximum((i % spb) * (rows // SUBLANES) - 1, 0), 0)
        hist_spec = pl.BlockSpec((1, SUBLANES, w), lambda i: (i // spb, 0, 0))
        ab_spec = pl.BlockSpec((1, rows, LANES), lambda i: (i // spb, i % spb, 0))
        out_map = x_map
        out_rows = (1, rows)
    lt = jnp.tril(jnp.ones((c, c), F32))
    sel = jnp.eye(SUBLANES, LANES, dtype=F32)
    hv = H_A * DV_A
    al = jnp.zeros((1, LANES), F32).at[0, :H_A].set(a_log)
    dtb = jnp.zeros((1, LANES), F32).at[0, :H_A].set(dt_bias)
    kern = functools.partial(_delta_prep_kernel, g_chunks=g_chunks, chunk=c,
                             chunks_are_seqs=chunks_are_seqs, chunks_per_seq=nc)
    big = lambda width: pl.BlockSpec(out_rows + (width,), out_map)
    return pl.pallas_call(
        kern,
        grid=(steps,),
        in_specs=[pl.BlockSpec(blk, x_map),
                  pl.BlockSpec((1, SUBLANES, w), halo_map) if not chunks_are_seqs
                  else pl.BlockSpec((g_chunks, SUBLANES, w), halo_map),
                  hist_spec, ab_spec,
                  pl.BlockSpec((CONV_A, w), lambda i: (0, 0)),
                  pl.BlockSpec((1, LANES), lambda i: (0, 0)),
                  pl.BlockSpec((1, LANES), lambda i: (0, 0)),
                  pl.BlockSpec((c, c), lambda i: (0, 0)),
                  pl.BlockSpec((SUBLANES, LANES), lambda i: (0, 0))],
        out_specs=[big(hv), big(hv), big(hv), big(hv), big(H_A * c),
                   pl.BlockSpec((g_chunks, SUBLANES, hv), lambda i: (i, 0, 0))],
        out_shape=[jax.ShapeDtypeStruct((n_seq, length, hv), F32)] * 4
        + [jax.ShapeDtypeStruct((n_seq, length, H_A * c), F32),
           jax.ShapeDtypeStruct((n_seq * nc, SUBLANES, hv), F32)],
        scratch_shapes=[pltpu.VMEM((g_chunks, SUBLANES + c, w), F32)],
        compiler_params=_cparams(("arbitrary",), VMEM_LIMIT),
        name="delta_prep",
    )(qkv_pre3, hist8 if chunks_are_seqs else qkv_pre3, hist8, ab3, conv_w, al, dtb, lt, sel)


def _delta_scan_kernel(ub_ref, wm_ref, qd_ref, kd_ref, qk_ref, dec_ref, z_ref, s0_ref, og_ref,
                       o_ref, sn_ref, s_ref, *, nb, chunk):
    c = chunk
    j = pl.program_id(1)

    @pl.when(j == 0)
    def _():
        s_ref[...] = s0_ref[...]

    og = og_ref[...]
    chains = [(b, h) for b in range(nb) for h in range(H_A)]
    sl = lambda h: slice(h * DV_A, (h + 1) * DV_A)
    states = [s_ref[b, h] for b, h in chains]
    ws = [_bdot(jnp.concatenate([wm_ref[b, :, sl(h)], qd_ref[b, :, sl(h)]], axis=0), s)
          for (b, h), s in zip(chains, states)]
    us = [ub_ref[b, :, sl(h)] - x[0:c] for (b, h), x in zip(chains, ws)]
    os_ = [x[c:2 * c] + _bdot(qk_ref[b, :, h * c:(h + 1) * c], u) for (b, h), x, u in zip(chains, ws, us)]
    for (b, h), s, u in zip(chains, states, us):
        s_ref[b, h] = s * dec_ref[b, 0, 0:1, sl(h)] + _bdot_tn(kd_ref[b, :, sl(h)], u)
    for b in range(nb):
        outs = []
        for h in range(H_A):
            o = os_[b * H_A + h]
            on = o * lax.rsqrt(jnp.mean(o * o, axis=-1, keepdims=True) + EPS) * og
            outs.append(on * _silu(z_ref[b, :, sl(h)]))
        o_ref[b] = jnp.concatenate(outs, axis=-1)
    sn_ref[...] = s_ref[...]


def _delta_scan(ub, wm, qd, kd, qk, dec, z3, s0, o_gain, chunk):
    n_seq, length, hv = ub.shape
    c = chunk
    nc = length // c
    nb = SEQ_BLOCK_SCAN
    dec4 = dec.reshape(n_seq, nc, SUBLANES, hv)
    tok = lambda width: pl.BlockSpec((nb, c, width), lambda b, j: (b, j, 0))
    st = pl.BlockSpec((nb, H_A, DK_A, DV_A), lambda b, j: (b, 0, 0, 0))
    return pl.pallas_call(
        functools.partial(_delta_scan_kernel, nb=nb, chunk=c),
        grid=(n_seq // nb, nc),
        in_specs=[tok(hv), tok(hv), tok(hv), tok(hv), tok(H_A * c),
                  pl.BlockSpec((nb, 1, SUBLANES, hv), lambda b, j: (b, j, 0, 0)),
                  tok(hv), st, pl.BlockSpec((1, DV_A), lambda b, j: (0, 0))],
        out_specs=[tok(hv), st],
        out_shape=[jax.ShapeDtypeStruct((n_seq, length, hv), F32),
                   jax.ShapeDtypeStruct((n_seq, H_A, DK_A, DV_A), F32)],
        scratch_shapes=[pltpu.VMEM((nb, H_A, DK_A, DV_A), F32)],
        compiler_params=_cparams(("arbitrary", "arbitrary"), VMEM_LIMIT),
        name="delta_scan",
    )(ub, wm, qd, kd, qk, dec4, z3, s0, o_gain.reshape(1, DV_A))


def _even_out_kernel(gcx_ref, halo_ref, hist_ref, oa_ref, x_ref, g1_ref, cw_ref, w_ref,
                     xo_ref, tail_ref, ext_ref, *, sb, lb, nl):
    step = pl.program_id(0)
    gcx = gcx_ref[...]
    b_gate = gcx[:, :, 0:D_B]
    bx = gcx[:, :, D_B:2 * D_B] * gcx[:, :, 2 * D_B:3 * D_B]
    if nl == 1:
        prev = hist_ref[...]
    else:
        hb = halo_ref[...]
        first = (step % nl) == 0
        prev = jnp.where(first, hist_ref[...], hb[:, :, D_B:2 * D_B] * hb[:, :, 2 * D_B:3 * D_B])
    ext_ref[:, 0:SUBLANES, :] = prev
    ext_ref[:, SUBLANES:SUBLANES + lb, :] = bx
    cw = cw_ref[...]
    conv = ext_ref[:, pl.ds(SUBLANES - (CONV_B - 1), lb), :] * cw[0:1, :]
    for j in range(1, CONV_B):
        conv = conv + ext_ref[:, pl.ds(SUBLANES - (CONV_B - 1) + j, lb), :] * cw[j:j + 1, :]
    out_b = (b_gate * conv).reshape(sb * lb, D_B)
    hv = H_A * DV_A
    y = _bdot(oa_ref[...], w_ref[0:hv, :]) + _bdot(out_b, w_ref[hv:hv + D_B, :])
    d = y.shape[-1]
    xo_ref[...] = x_ref[...] + g1_ref[...] * y.reshape(sb, lb, d)
    tail_ref[...] = bx[:, lb - SUBLANES:lb, :]


def _even_out(gcx3, hist8, out_a2, x3, mod3, k_gate, conv_w, w_out_bf16):
    n_seq, length, d = x3.shape
    sb, lb, steps, nl = _tok_tiling(n_seq, length, TOK_TILE)
    rows = sb * lb
    w3 = gcx3.shape[-1]
    hv = H_A * DV_A
    halo_map = lambda i: (i // nl, jnp.maximum((i % nl) * (lb // SUBLANES) - 1, 0), 0)
    return pl.pallas_call(
        functools.partial(_even_out_kernel, sb=sb, lb=lb, nl=nl),
        grid=(steps,),
        in_specs=[pl.BlockSpec((sb, lb, w3), lambda i: (i // nl, i % nl, 0)),
                  pl.BlockSpec((sb, SUBLANES, w3), halo_map if nl > 1 else (lambda i: (i, 0, 0))),
                  pl.BlockSpec((sb, SUBLANES, D_B), lambda i: (i // nl, 0, 0)),
                  pl.BlockSpec((rows, hv), lambda i: (i, 0)),
                  pl.BlockSpec((sb, lb, d), lambda i: (i // nl, i % nl, 0)),
                  pl.BlockSpec((sb, 1, d), lambda i: (i // nl, 0, k_gate)),
                  pl.BlockSpec((CONV_B, D_B), lambda i: (0, 0)),
                  pl.BlockSpec((hv + D_B, d), lambda i: (0, 0))],
        out_specs=[pl.BlockSpec((sb, lb, d), lambda i: (i // nl, i % nl, 0)),
                   pl.BlockSpec((sb, SUBLANES, D_B), lambda i: (i // nl, 0, 0))],
        out_shape=[jax.ShapeDtypeStruct((n_seq, length, d), F32),
                   jax.ShapeDtypeStruct((n_seq, SUBLANES, D_B), F32)],
        scratch_shapes=[pltpu.VMEM((sb, SUBLANES + lb, D_B), F32)],
        compiler_params=_cparams(("arbitrary",), VMEM_LIMIT),
        name="even_out",
    )(gcx3, gcx3, hist8, out_a2, x3, mod3, conv_w, w_out_bf16)


def _proj_res_kernel(a_ref, x_ref, g_ref, w_ref, xo_ref, *, sb, lb):
    y = _bdot(a_ref[...], w_ref[...])
    xo_ref[...] = x_ref[...] + g_ref[...] * y.reshape(sb, lb, y.shape[-1])


def _proj_residual(a2, x3, mod3, k_gate, w_bf16):
    n_seq, length, d = x3.shape
    sb, lb, steps, nl = _tok_tiling(n_seq, length, TOK_TILE)
    rows = sb * lb
    ka = a2.shape[-1]
    return pl.pallas_call(
        functools.partial(_proj_res_kernel, sb=sb, lb=lb),
        grid=(steps,),
        in_specs=[pl.BlockSpec((rows, ka), lambda i: (i, 0)),
                  pl.BlockSpec((sb, lb, d), lambda i: (i // nl, i % nl, 0)),
                  pl.BlockSpec((sb, 1, d), lambda i: (i // nl, 0, k_gate)),
                  pl.BlockSpec((ka, d), lambda i: (0, 0))],
        out_specs=pl.BlockSpec((sb, lb, d), lambda i: (i // nl, i % nl, 0)),
        out_shape=jax.ShapeDtypeStruct((n_seq, length, d), F32),
        compiler_params=_cparams(("arbitrary",), VMEM_LIMIT),
        name="proj_residual",
    )(a2, x3, mod3, w_bf16)


def _router_kernel(x_ref, sh_ref, sc_ref, gain_ref, wrt_ref, bcol_ref, utri_ref,
                   mi_ref, gcol_ref, cnt_ref, carry_ref):
    i = pl.program_id(0)

    @pl.when(i == 0)
    def _():
        carry_ref[...] = jnp.zeros_like(carry_ref)

    h2 = _norm_mod(x_ref[...], gain_ref[...], sc_ref[...], sh_ref[...])
    tm = h2.shape[0]
    logits = _bdot_nt(wrt_ref[...], h2) + bcol_ref[...]
    lg = logits[0:SUBLANES]
    e = jnp.exp(lg - jnp.max(lg, axis=0, keepdims=True))
    pg = e / jnp.sum(e, axis=0, keepdims=True)
    p_top = jnp.max(pg, axis=0, keepdims=True)
    rid8 = lax.broadcasted_iota(I32, (SUBLANES, tm), 0)
    g_top = jnp.min(jnp.where(pg == p_top, rid8, SUBLANES), axis=0, keepdims=True)
    le = logits[SUBLANES:SUBLANES + N_EXPERTS]
    sel = jnp.zeros((E_PER_GROUP, tm), F32)
    for gi in range(N_GROUPS):
        sel = sel + jnp.where(g_top == gi, le[gi * E_PER_GROUP:(gi + 1) * E_PER_GROUP], 0.0)
    e2 = jnp.exp(sel - jnp.max(sel, axis=0, keepdims=True))
    p_in = e2 / jnp.sum(e2, axis=0, keepdims=True)
    w_a = jnp.max(p_in, axis=0, keepdims=True)
    i_a = jnp.min(jnp.where(p_in == w_a, rid8, SUBLANES), axis=0, keepdims=True)
    p_rest = jnp.where(rid8 == i_a, -1.0, p_in)
    w_b = jnp.max(p_rest, axis=0, keepdims=True)
    i_b = jnp.min(jnp.where(p_rest == w_b, rid8, SUBLANES), axis=0, keepdims=True)
    den = w_a + w_b
    gate1 = p_top * (w_a / den)
    gate2 = p_top * (w_b / den)
    ex1 = g_top * E_PER_GROUP + i_a
    ex2 = g_top * E_PER_GROUP + i_b

    rid32 = lax.broadcasted_iota(I32, (N_EXPERTS, tm), 0)
    oh1 = rid32 == ex1
    oh2 = rid32 == ex2
    ohc = jnp.where(oh1, 1.0, jnp.where(oh2, 1.0, 0.0))
    cum = _bdot(ohc, utri_ref[...])
    carry = carry_ref[...]
    base = cum - ohc + carry[:, 0:1]
    r1 = jnp.sum(jnp.where(oh1, base, 0.0), axis=0, keepdims=True)
    r2 = jnp.sum(jnp.where(oh2, base, 0.0), axis=0, keepdims=True)
    new_carry = carry + cum[:, tm - 1:tm]
    carry_ref[...] = new_carry
    cnt_ref[...] = new_carry

    bro = lambda v: jnp.broadcast_to(v, (SUBLANES, tm))
    mi_ref[...] = jnp.where(rid8 == 0, bro(ex1),
                            jnp.where(rid8 == 1, bro(ex2),
                                      jnp.where(rid8 == 2, bro(r1.astype(I32)),
                                                jnp.where(rid8 == 3, bro(r2.astype(I32)), 0))))
    rid128 = lax.broadcasted_iota(I32, (LANES, tm), 0)
    g128 = jnp.where(rid128 == 0, jnp.broadcast_to(gate1, (LANES, tm)),
                     jnp.where(rid128 == 1, jnp.broadcast_to(gate2, (LANES, tm)), 0.0))
    gcol_ref[...] = g128.T


def _moe_router(x3, mod3, k_sh, k_sc, gain, wrt_bf16, bcol):
    n_seq, length, d = x3.shape
    sb, lb, steps, nl = _tok_tiling(n_seq, length, TOK_TILE)
    tm = sb * lb
    t = n_seq * length
    utri = jnp.triu(jnp.ones((tm, tm), F32)).astype(BF16)
    r_rows = wrt_bf16.shape[0]
    return pl.pallas_call(
        _router_kernel,
        grid=(steps,),
        in_specs=[pl.BlockSpec((sb, lb, d), lambda i: (i // nl, i % nl, 0)),
                  pl.BlockSpec((sb, 1, d), lambda i: (i // nl, 0, k_sh)),
                  pl.BlockSpec((sb, 1, d), lambda i: (i // nl, 0, k_sc)),
                  pl.BlockSpec((1, d), lambda i: (0, 0)),
                  pl.BlockSpec((r_rows, d), lambda i: (0, 0)),
                  pl.BlockSpec((r_rows, 1), lambda i: (0, 0)),
                  pl.BlockSpec((tm, tm), lambda i: (0, 0))],
        out_specs=[pl.BlockSpec((SUBLANES, tm), lambda i: (0, i)),
                   pl.BlockSpec((tm, LANES), lambda i: (i, 0)),
                   pl.BlockSpec((N_EXPERTS, LANES), lambda i: (0, 0))],
        out_shape=[jax.ShapeDtypeStruct((SUBLANES, t), I32),
                   jax.ShapeDtypeStruct((t, LANES), F32),
                   jax.ShapeDtypeStruct((N_EXPERTS, LANES), F32)],
        scratch_shapes=[pltpu.VMEM((N_EXPERTS, LANES), F32)],
        compiler_params=_cparams(("arbitrary",), VMEM_LIMIT),
        name="moe_router",
    )(x3, mod3, mod3, gain.reshape(1, d), wrt_bf16, bcol, utri)


def _plan_kernel(mi_ref, cnt_ref, ltri_ref, pos_ref, tinfo_ref, *, n_tiles_pad, row_tile):
    cnt = cnt_ref[...]
    nt = jnp.floor((cnt + (row_tile - 1)) * (1.0 / row_tile))
    tstart = _bdot(ltri_ref[...], nt)
    mi = mi_ref[...]
    tm = mi.shape[1]
    rid32 = lax.broadcasted_iota(I32, (N_EXPERTS, tm), 0)
    ts_col = tstart[:, 0:1]

    def pos_of(ex, rank):
        start = jnp.sum(jnp.where(rid32 == ex, ts_col, 0.0), axis=0, keepdims=True)
        return start.astype(I32) * row_tile + rank

    p1 = pos_of(mi[0:1], mi[2:3])
    p2 = pos_of(mi[1:2], mi[3:4])
    rid8 = lax.broadcasted_iota(I32, (SUBLANES, tm), 0)
    pos_ref[0] = jnp.where(rid8 == 0, jnp.broadcast_to(p1, (SUBLANES, tm)),
                           jnp.where(rid8 == 1, jnp.broadcast_to(p2, (SUBLANES, tm)), 0))
    tend_col = ts_col + nt[:, 0:1]
    jt = lax.broadcasted_iota(I32, (N_EXPERTS, n_tiles_pad), 1).astype(F32)
    te = jnp.sum(jnp.where(tend_col <= jt, 1.0, 0.0), axis=0, keepdims=True)
    te = jnp.minimum(te, N_EXPERTS - 1.0).astype(I32)
    total = jnp.sum(nt[:, 0:1], axis=0, keepdims=True).astype(I32)
    rid8t = lax.broadcasted_iota(I32, (SUBLANES, n_tiles_pad), 0)
    tinfo_ref[...] = jnp.where(rid8t == 0, jnp.broadcast_to(te, (SUBLANES, n_tiles_pad)),
                               jnp.broadcast_to(total, (SUBLANES, n_tiles_pad)))


def _moe_plan(meta_i, counts, n_tiles_pad, row_tile):
    t = meta_i.shape[1]
    tm = MOE_TOK_TILE
    steps = t // tm
    ltri = jnp.tril(jnp.ones((N_EXPERTS, N_EXPERTS), F32), k=-1).astype(BF16)
    return pl.pallas_call(
        functools.partial(_plan_kernel, n_tiles_pad=n_tiles_pad, row_tile=row_tile),
        grid=(steps,),
        in_specs=[pl.BlockSpec((SUBLANES, tm), lambda i: (0, i)),
                  pl.BlockSpec((N_EXPERTS, LANES), lambda i: (0, 0)),
                  pl.BlockSpec((N_EXPERTS, N_EXPERTS), lambda i: (0, 0))],
        out_specs=[pl.BlockSpec((1, SUBLANES, tm), lambda i: (i, 0, 0)),
                   pl.BlockSpec((SUBLANES, n_tiles_pad), lambda i: (0, 0))],
        out_shape=[jax.ShapeDtypeStruct((steps, SUBLANES, tm), I32),
                   jax.ShapeDtypeStruct((SUBLANES, n_tiles_pad), I32)],
        compiler_params=_cparams(("arbitrary",)),
        name="moe_plan",
    )(meta_i, counts, ltri)


def _to_token_tiles(ref, val):
    tm, d = val.shape
    for s in range(d // LANES):
        ref[pl.ds(s, tm, stride=SUBLANES), :] = val[:, s * LANES:(s + 1) * LANES]


def _from_token_tiles(ref, tm):
    return jnp.concatenate([ref[pl.ds(s, tm, stride=SUBLANES), :] for s in range(SUBLANES)], axis=-1)


def _tile_rows(idx):
    return pl.ds(pl.multiple_of(idx * SUBLANES, SUBLANES), SUBLANES)


def _row_copy_out(buf_ref, slot, r, dst_hbm, p, sem):
    return pltpu.make_async_copy(buf_ref.at[slot, _tile_rows(r), :], dst_hbm.at[_tile_rows(p), :], sem.at[slot])


def _dispatch_kernel(x_ref, sh_ref, sc_ref, gain_ref, pos_ref, xs_in_ref, xs_ref,
                     buf_ref, idx_ref, sem_ref, isem_ref, *, tm, steps):
    del xs_in_ref
    i = pl.program_id(0)
    slot = i % 2

    def drain(s):
        for _ in range(2):
            pltpu.make_async_copy(buf_ref.at[s], xs_ref.at[pl.ds(0, tm * SUBLANES), :], sem_ref.at[s]).wait()

    @pl.when(i >= 2)
    def _():
        drain(slot)

    _to_token_tiles(buf_ref.at[slot], _norm_mod(x_ref[...], gain_ref[...], sc_ref[...], sh_ref[...]))
    cp = pltpu.make_async_copy(pos_ref.at[0], idx_ref, isem_ref)
    cp.start()
    cp.wait()

    def issue(r, carry):
        _row_copy_out(buf_ref, slot, r, xs_ref, idx_ref[0, r], sem_ref).start()
        _row_copy_out(buf_ref, slot, r, xs_ref, idx_ref[1, r], sem_ref).start()
        return carry
    lax.fori_loop(0, tm, issue, 0, unroll=8)

    @pl.when(i == steps - 1)
    def _():
        drain(slot)
        if steps > 1:
            drain(1 - slot)


def _moe_dispatch(x3, mod3, k_sh, k_sc, gain, pos3, xs0):
    n_seq, length, d = x3.shape
    assert d == SUBLANES * LANES
    sb, lb, steps, nl = _tok_tiling(n_seq, length, MOE_TOK_TILE)
    tm = sb * lb
    return pl.pallas_call(
        functools.partial(_dispatch_kernel, tm=tm, steps=steps),
        grid=(steps,),
        in_specs=[pl.BlockSpec((sb, lb, d), lambda i: (i // nl, i % nl, 0)),
                  pl.BlockSpec((sb, 1, d), lambda i: (i // nl, 0, k_sh)),
                  pl.BlockSpec((sb, 1, d), lambda i: (i // nl, 0, k_sc)),
                  pl.BlockSpec((1, d), lambda i: (0, 0)),
                  pl.BlockSpec((1, SUBLANES, tm), lambda i: (i, 0, 0)),
                  pl.BlockSpec(memory_space=pl.ANY)],
        out_specs=pl.BlockSpec(memory_space=pl.ANY),
        out_shape=jax.ShapeDtypeStruct(xs0.shape, F32),
        scratch_shapes=[pltpu.VMEM((2, tm * SUBLANES, LANES), F32),
                        pltpu.SMEM((SUBLANES, tm), I32),
                        pltpu.SemaphoreType.DMA((2,)),
                        pltpu.SemaphoreType.DMA(())],
        input_output_aliases={5: 0},
        compiler_params=_cparams(("arbitrary",), VMEM_LIMIT),
        name="moe_dispatch",
    )(x3, mod3, mod3, gain.reshape(1, d), pos3, xs0)


def _expert_kernel(te_ref, nt_ref, xs_ref, wg_ref, wu_ref, wd_ref, ys_ref, *, row_tile):
    i = pl.program_id(0)

    @pl.when(i < nt_ref[0])
    def _():
        x = _from_token_tiles(xs_ref, row_tile).astype(BF16)
        hid = _silu(_bdot(x, wg_ref[0])) * _bdot(x, wu_ref[0])
        _to_token_tiles(ys_ref, _bdot(hid, wd_ref[0]))

    @pl.when(i >= nt_ref[0])
    def _():
        ys_ref[...] = jnp.zeros_like(ys_ref)


def _moe_experts(xs, tile_expert, n_tiles, wg, wu, wd, row_tile):
    d, f = wg.shape[-2], wg.shape[-1]
    steps = xs.shape[0] // (row_tile * SUBLANES)
    x_map = lambda i, te, nt: (jnp.minimum(i, nt[0] - 1), 0)
    grid_spec = pltpu.PrefetchScalarGridSpec(
        num_scalar_prefetch=2,
        grid=(steps,),
        in_specs=[pl.BlockSpec((row_tile * SUBLANES, LANES), x_map),
                  pl.BlockSpec((1, d, f), lambda i, te, nt: (te[i], 0, 0)),
                  pl.BlockSpec((1, d, f), lambda i, te, nt: (te[i], 0, 0)),
                  pl.BlockSpec((1, f, d), lambda i, te, nt: (te[i], 0, 0))],
        out_specs=pl.BlockSpec((row_tile * SUBLANES, LANES), lambda i, te, nt: (i, 0)),
    )
    return pl.pallas_call(
        functools.partial(_expert_kernel, row_tile=row_tile),
        grid_spec=grid_spec,
        out_shape=jax.ShapeDtypeStruct(xs.shape, F32),
        compiler_params=_cparams(("arbitrary",), VMEM_LIMIT),
        name="moe_experts",
    )(tile_expert, n_tiles, xs, wg, wu, wd)


def _row_copy_in(src_hbm, p, buf_ref, slot, which, r, sem):
    return pltpu.make_async_copy(src_hbm.at[_tile_rows(p), :], buf_ref.at[slot, which, _tile_rows(r), :],
                                 sem.at[slot])


def _combine_kernel(x_ref, g_ref, gcol_ref, pos_ref, posn_ref, ys_ref, gainf_ref, xo_ref,
                    buf_ref, idx_ref, sem_ref, isem_ref, *, tm, steps, sb, lb, final_norm):
    i = pl.program_id(0)
    slot = i % 2

    def issue(pref, s):
        cp = pltpu.make_async_copy(pref.at[0], idx_ref, isem_ref)
        cp.start()
        cp.wait()

        def body(r, carry):
            _row_copy_in(ys_ref, idx_ref[0, r], buf_ref, s, 0, r, sem_ref).start()
            _row_copy_in(ys_ref, idx_ref[1, r], buf_ref, s, 1, r, sem_ref).start()
            return carry
        lax.fori_loop(0, tm, body, 0, unroll=8)

    @pl.when(i == 0)
    def _():
        issue(pos_ref, slot)

    @pl.when(i + 1 < steps)
    def _():
        issue(posn_ref, 1 - slot)

    for which in range(2):
        pltpu.make_async_copy(ys_ref.at[pl.ds(0, tm * SUBLANES), :], buf_ref.at[slot, which],
                              sem_ref.at[slot]).wait()

    gc = gcol_ref[...]
    y = gc[:, 0:1] * _from_token_tiles(buf_ref.at[slot, 0], tm) \
        + gc[:, 1:2] * _from_token_tiles(buf_ref.at[slot, 1], tm)
    d = y.shape[-1]
    xn = x_ref[...] + g_ref[...] * y.reshape(sb, lb, d)
    if final_norm:
        xn = xn * lax.rsqrt(jnp.mean(xn * xn, axis=-1, keepdims=True) + EPS) * gainf_ref[...]
    xo_ref[...] = xn


def _moe_combine(x3, mod3, k_gate, gates_col, pos3, ys, gain_f, final_norm):
    n_seq, length, d = x3.shape
    sb, lb, steps, nl = _tok_tiling(n_seq, length, MOE_TOK_TILE)
    tm = sb * lb
    return pl.pallas_call(
        functools.partial(_combine_kernel, tm=tm, steps=steps, sb=sb, lb=lb, final_norm=final_norm),
        grid=(steps,),
        in_specs=[pl.BlockSpec((sb, lb, d), lambda i: (i // nl, i % nl, 0)),
                  pl.BlockSpec((sb, 1, d), lambda i: (i // nl, 0, k_gate)),
                  pl.BlockSpec((tm, LANES), lambda i: (i, 0)),
                  pl.BlockSpec((1, SUBLANES, tm), lambda i: (i, 0, 0)),
                  pl.BlockSpec((1, SUBLANES, tm), lambda i: (jnp.minimum(i + 1, steps - 1), 0, 0)),
                  pl.BlockSpec(memory_space=pl.ANY),
                  pl.BlockSpec((1, d), lambda i: (0, 0))],
        out_specs=pl.BlockSpec((sb, lb, d), lambda i: (i // nl, i % nl, 0)),
        out_shape=jax.ShapeDtypeStruct((n_seq, length, d), F32),
        scratch_shapes=[pltpu.VMEM((2, 2, tm * SUBLANES, LANES), F32),
                        pltpu.SMEM((SUBLANES, tm), I32),
                        pltpu.SemaphoreType.DMA((2,)),
                        pltpu.SemaphoreType.DMA(())],
        compiler_params=_cparams(("arbitrary",), VMEM_LIMIT),
        name="moe_combine",
    )(x3, mod3, gates_col, pos3, pos3, ys, gain_f.reshape(1, d))


def _hier_moe_residual(x3, mod3, k_sh, k_sc, k_gate, gain, wrt_bf16, bcol, wg, wu, wd, gain_f, final_norm):
    n_seq, length, d = x3.shape
    t = n_seq * length
    row_tile = MOE_ROW_TILE_BIG if (2 * t) // N_EXPERTS >= MOE_ROW_TILE_BIG else MOE_ROW_TILE_SMALL
    max_tiles = (2 * t) // row_tile + N_EXPERTS
    n_tiles_pad = -(-max_tiles // LANES) * LANES
    meta_i, gates_col, counts = _moe_router(x3, mod3, k_sh, k_sc, gain, wrt_bf16, bcol)
    pos3, tinfo = _moe_plan(meta_i, counts, n_tiles_pad, row_tile)
    xs0 = jnp.zeros((max_tiles * row_tile * SUBLANES, LANES), F32)
    xs = _moe_dispatch(x3, mod3, k_sh, k_sc, gain, pos3, xs0)
    ys = _moe_experts(xs, tinfo[0, :max_tiles], tinfo[1, :1], wg, wu, wd, row_tile)
    return _moe_combine(x3, mod3, k_gate, gates_col, pos3, ys, gain_f, final_norm)


def _multiplicity(dist):
    m = jnp.zeros(dist.shape, F32)
    for window, dil in BRANCHES:
        m = m + ((dist >= 0) & (dist <= window) & (dist % dil == 0)).astype(F32)
    return m


def _log_mult(dist):
    m = _multiplicity(dist)
    return jnp.where(m > 0, jnp.log(jnp.maximum(m, 1.0)), NEG_BIG)


def _attn_prompt_kernel(q_ref, k_ref, v_ref, nbias_ref, fbias_ref, o_ref, mf_ref, lf_ref, af_ref, *, length):
    blk = ATT_BLK
    nq = length // blk
    near = ATT_NEAR_BLOCKS
    ncls = BRANCHES[-1][1]
    csz = length // ncls
    has_far = nq > near
    scale = (HD_C ** -0.5) * LOG2_E
    nt = lambda a, b: lax.dot_general(a, b, (((1,), (1,)), ((), ())), preferred_element_type=F32)
    mm = lambda a, b: jnp.dot(a, b, preferred_element_type=F32)

    def split_heads(q):
        head0 = lax.broadcasted_iota(I32, q.shape, 1) < HD_C
        return head0, jnp.where(head0, q, 0.0).astype(BF16), jnp.where(head0, 0.0, q).astype(BF16)

    if has_far:
        fbias = fbias_ref[...]
        group = 4
        for r0 in range(0, ncls, group):
            rs = list(range(r0, r0 + group))
            rows = [pl.ds(r, csz, stride=ncls) for r in rs]
            hq = [split_heads(q_ref[0, rw, :] * scale) for rw in rows]
            kr = [k_ref[0, rw, :].astype(BF16) for rw in rows]
            vr = [v_ref[0, rw, :].astype(BF16) for rw in rows]
            s0 = [nt(h[1], k) + fbias for h, k in zip(hq, kr)]
            s1 = [nt(h[2], k) + fbias for h, k in zip(hq, kr)]
            m0 = [jnp.max(s, axis=-1, keepdims=True) for s in s0]
            m1 = [jnp.max(s, axis=-1, keepdims=True) for s in s1]
            p0 = [jnp.exp2(s - m) for s, m in zip(s0, m0)]
            p1 = [jnp.exp2(s - m) for s, m in zip(s1, m1)]
            a0 = [mm(p.astype(BF16), v) for p, v in zip(p0, vr)]
            a1 = [mm(p.astype(BF16), v) for p, v in zip(p1, vr)]
            for i, rw in enumerate(rows):
                head0 = hq[i][0]
                af_ref[rw, :] = jnp.where(head0, a0[i], a1[i])
                mf_ref[rw, :] = jnp.where(head0, m0[i], m1[i])
                lf_ref[rw, :] = jnp.where(head0, jnp.sum(p0[i], axis=-1, keepdims=True),
                                          jnp.sum(p1[i], axis=-1, keepdims=True))

    for qi in range(nq):
        rq = slice(qi * blk, (qi + 1) * blk)
        head0, q0, q1 = split_heads(q_ref[0, rq, :] * scale)
        js = list(range(max(0, qi - near + 1), qi + 1))
        ks = [k_ref[0, kj * blk:(kj + 1) * blk, :].astype(BF16) for kj in js]
        vs = [v_ref[0, kj * blk:(kj + 1) * blk, :].astype(BF16) for kj in js]
        s0 = [nt(q0, k) + nbias_ref[qi - kj] for k, kj in zip(ks, js)]
        s1 = [nt(q1, k) + nbias_ref[qi - kj] for k, kj in zip(ks, js)]
        far = has_far and qi >= near
        m0 = functools.reduce(jnp.maximum, [jnp.max(s, axis=-1, keepdims=True) for s in s0])
        m1 = functools.reduce(jnp.maximum, [jnp.max(s, axis=-1, keepdims=True) for s in s1])
        if far:
            mf = mf_ref[rq, :]
            mf0, mf1 = mf[:, 0:1], mf[:, HD_C:HD_C + 1]
            m0 = jnp.maximum(m0, mf0)
            m1 = jnp.maximum(m1, mf1)
        p0 = [jnp.exp2(s - m0) for s in s0]
        p1 = [jnp.exp2(s - m1) for s in s1]
        l0 = sum(jnp.sum(p, axis=-1, keepdims=True) for p in p0)
        l1 = sum(jnp.sum(p, axis=-1, keepdims=True) for p in p1)
        a0 = sum(mm(p.astype(BF16), v) for p, v in zip(p0, vs))
        a1 = sum(mm(p.astype(BF16), v) for p, v in zip(p1, vs))
        acc = jnp.where(head0, a0, a1)
        den = jnp.where(head0, l0, l1)
        if far:
            w_far = jnp.where(head0, jnp.exp2(mf0 - m0), jnp.exp2(mf1 - m1))
            acc = acc + af_ref[rq, :] * w_far
            den = den + lf_ref[rq, :] * w_far
        o_ref[0, rq, :] = acc / den


def _attn_prompt(q3, k3, v3):
    n_seq, length, d = q3.shape
    pairs = d // LANES
    blk = ATT_BLK
    near = ATT_NEAR_BLOCKS
    ncls = BRANCHES[-1][1]
    csz = length // ncls
    per = blk // ncls
    assert (near - 1) * blk >= BRANCHES[-2][0] and length % blk == 0 and blk % ncls == 0
    a = jnp.arange(blk)
    dist = (jnp.arange(near)[:, None, None] * blk) + a[None, :, None] - a[None, None, :]
    nbias = _log_mult(dist) * jnp.where(_multiplicity(dist) > 0, LOG2_E, 1.0)
    m = jnp.arange(csz)
    far_ok = ((m[:, None] // per - m[None, :] // per) >= near) & \
        ((m[:, None] - m[None, :]) * ncls <= BRANCHES[-1][0])
    fbias = jnp.where(far_ok, 0.0, NEG_BIG).astype(F32)
    return pl.pallas_call(
        functools.partial(_attn_prompt_kernel, length=length),
        grid=(n_seq, pairs),
        in_specs=[pl.BlockSpec((1, length, LANES), lambda n, p: (n, 0, p)),
                  pl.BlockSpec((1, length, LANES), lambda n, p: (n, 0, p)),
                  pl.BlockSpec((1, length, LANES), lambda n, p: (n, 0, p)),
                  pl.BlockSpec((near, blk, blk), lambda n, p: (0, 0, 0)),
                  pl.BlockSpec((csz, csz), lambda n, p: (0, 0))],
        out_specs=pl.BlockSpec((1, length, LANES), lambda n, p: (n, 0, p)),
        out_shape=jax.ShapeDtypeStruct((n_seq, length, d), F32),
        scratch_shapes=[pltpu.VMEM((length, LANES), F32)] * 3,
        compiler_params=_cparams(("arbitrary", "arbitrary"), VMEM_LIMIT),
        name="attn_prompt",
    )(q3, k3, v3, nbias, fbias)


def _attn_sample_kernel(q_ref, kn_ref, vn_ref, kt_ref, vt_ref, bias_ref, biasn_ref, o_ref, *, heads):
    q = q_ref[0] * (HD_C ** -0.5)
    kn = kn_ref[0]
    vn = vn_ref[0]
    bias = bias_ref[...]
    biasn = biasn_ref[...]
    outs = []
    for h in range(heads):
        lo, hi = h * HD_C, (h + 1) * HD_C
        qh = q[:, lo:hi].astype(BF16)
        s = jnp.dot(qh, kt_ref[0, h].astype(BF16), preferred_element_type=F32) + bias
        sn = _bdot_nt(qh, kn[:, lo:hi]) + biasn
        m = jnp.maximum(jnp.max(s, axis=-1, keepdims=True), jnp.max(sn, axis=-1, keepdims=True))
        p = jnp.exp(s - m)
        pn = jnp.exp(sn - m)
        den = jnp.sum(p, axis=-1, keepdims=True) + jnp.sum(pn, axis=-1, keepdims=True)
        o = _bdot_nt(p, vt_ref[0, h]) + _bdot(pn, vn[:, lo:hi])
        outs.append(o / den)
    o_ref[0] = jnp.concatenate(outs, axis=-1)


def _attn_sample(q3, k3, v3, k_cache, v_cache):
    n_seq, t_len, d = q3.shape
    w_buf = k_cache.shape[1]
    kt = jnp.transpose(k_cache, (0, 2, 3, 1))
    vt = jnp.transpose(v_cache, (0, 2, 3, 1))
    heads = SAMPLE_HEADS_PER_STEP
    hw = heads * HD_C
    nj = H_C // heads
    tpos = w_buf + jnp.arange(t_len)
    bias = _log_mult(tpos[:, None] - jnp.arange(w_buf)[None, :])
    biasn = _log_mult(jnp.arange(t_len)[:, None] - jnp.arange(t_len)[None, :])
    return pl.pallas_call(
        functools.partial(_attn_sample_kernel, heads=heads),
        grid=(n_seq, nj),
        in_specs=[pl.BlockSpec((1, t_len, hw), lambda n, j: (n, 0, j)),
                  pl.BlockSpec((1, t_len, hw), lambda n, j: (n, 0, j)),
                  pl.BlockSpec((1, t_len, hw), lambda n, j: (n, 0, j)),
                  pl.BlockSpec((1, heads, HD_C, w_buf), lambda n, j: (n, j, 0, 0)),
                  pl.BlockSpec((1, heads, HD_C, w_buf), lambda n, j: (n, j, 0, 0)),
                  pl.BlockSpec((t_len, w_buf), lambda n, j: (0, 0)),
                  pl.BlockSpec((t_len, t_len), lambda n, j: (0, 0))],
        out_specs=pl.BlockSpec((1, t_len, hw), lambda n, j: (n, 0, j)),
        out_shape=jax.ShapeDtypeStruct((n_seq, t_len, d), F32),
        compiler_params=_cparams(("arbitrary", "arbitrary"), VMEM_LIMIT),
        name="attn_sample",
    )(q3, k3, v3, kt, vt, bias, biasn)


def _pad_hist(hist, n_seq, width):
    k1 = hist.shape[1]
    return jnp.concatenate([jnp.zeros((n_seq, SUBLANES - k1, width), F32), hist.astype(F32)], axis=1)


def _prep_weights(p):
    d = p['w_in_even'].shape[1]
    out = {}
    w_in = p['w_in_even'][0]
    hv = H_A * DV_A
    o_z = QKV_A
    o_a = o_z + hv
    o_g = o_a + 2 * H_A
    ab_pad = jnp.zeros((d, LANES - 2 * H_A), F32)
    out['w_in'] = jnp.concatenate([w_in[:, :o_a], w_in[:, o_a:o_g], ab_pad, w_in[:, o_g:]], axis=1).astype(BF16)
    out['w_out_even'] = p['w_out_even'][0].astype(BF16)
    out['w_qkv'] = p['w_qkv_odd'][0].astype(BF16)
    out['w_out_odd'] = p['w_out_odd'][0].astype(BF16)
    depth = p['w_router_group'].shape[0]
    wrt, bcol, wg, wu, wd = [], [], [], [], []
    for l in range(depth):
        pad_w = jnp.zeros((SUBLANES - N_GROUPS, d), F32)
        wrt.append(jnp.concatenate([p['w_router_group'][l].T, pad_w, p['w_router_expert'][l].T], axis=0).astype(BF16))
        pad_b = jnp.full((SUBLANES - N_GROUPS,), NEG_BIG, F32)
        bcol.append(jnp.concatenate([p['b_router_group'][l], pad_b, p['b_router_expert'][l]]).reshape(-1, 1))
        f = p['w_exp_gate'].shape[-1]
        wg.append(p['w_exp_gate'][l].reshape(N_EXPERTS, d, f))
        wu.append(p['w_exp_up'][l].reshape(N_EXPERTS, d, f))
        wd.append(p['w_exp_down'][l].reshape(N_EXPERTS, f, d))
    out.update(wrt=wrt, bcol=bcol, wg=wg, wu=wu, wd=wd)
    return out


def _trunk(x3, mod, a_ssm, a_conv, b_conv, kv_k, kv_v, p, w, sample):
    n_seq, length, d = x3.shape
    t = n_seq * length
    hv = H_A * DV_A
    mod3 = [m.reshape(n_seq, 1, 6 * d) for m in mod]

    qkv_pre, z, ab, gcx = _norm_mod_matmul(x3, mod3[0], 0, 1, p['norm1'][0], w['w_in'],
                                           (QKV_A, hv, LANES, 3 * D_B))
    chunk = min(CHUNK_PROMPT, length)
    g_chunks = 4
    qkv_pre3 = qkv_pre.reshape(n_seq, length, QKV_A)
    ub, wm, qd, kd, qk, dec = _delta_prep(qkv_pre3, _pad_hist(a_conv[0], n_seq, QKV_A),
                                          ab.reshape(n_seq, length, LANES), p['conv_a_w'][0],
                                          p['a_log'][0], p['dt_bias'][0], chunk, g_chunks)
    out_a, s_new = _delta_scan(ub, wm, qd, kd, qk, dec, z.reshape(n_seq, length, hv),
                               a_ssm[0].astype(F32), p['o_gain_a'][0], chunk)
    x3, bx_tail = _even_out(gcx.reshape(n_seq, length, 3 * D_B), _pad_hist(b_conv[0], n_seq, D_B),
                            out_a.reshape(t, hv), x3, mod3[0], 2, p['conv_b_w'][0], w['w_out_even'])
    new_aconv = jnp.concatenate([a_conv[0].astype(F32), qkv_pre3], axis=1)[:, -(CONV_A - 1):]
    new_bconv = jnp.concatenate([b_conv[0].astype(F32), bx_tail], axis=1)[:, -(CONV_B - 1):]
    x3 = _hier_moe_residual(x3, mod3[0], 3, 4, 5, p['norm2'][0], w['wrt'][0], w['bcol'][0],
                            w['wg'][0], w['wu'][0], w['wd'][0], p['norm_f'], False)

    q_new, k_new, v_new = (a.reshape(n_seq, length, d) for a in
                           _norm_mod_matmul(x3, mod3[1], 0, 1, p['norm1'][1], w['w_qkv'], (d, d, d)))
    if sample:
        attn = _attn_sample(q_new, k_new, v_new, kv_k[0], kv_v[0])
        keep = length
    else:
        attn = _attn_prompt(q_new, k_new, v_new)
        keep = min(BRANCHES[-1][0], length)
    new_k = k_new[:, length - keep:].reshape(n_seq, keep, H_C, HD_C)
    new_v = v_new[:, length - keep:].reshape(n_seq, keep, H_C, HD_C)
    x3 = _proj_residual(attn.reshape(t, d), x3, mod3[1], 2, w['w_out_odd'])
    y = _hier_moe_residual(x3, mod3[1], 3, 4, 5, p['norm2'][1], w['wrt'][1], w['bcol'][1],
                           w['wg'][1], w['wu'][1], w['wd'][1], p['norm_f'], True)
    return y, s_new[None], new_aconv[None], new_bconv[None], new_k[None], new_v[None]


def kernel(x_prompt, x_sample, state_a_ssm, state_a_conv, state_b_conv, cache_c_k, cache_c_v, c_prompt, c_sample, w_ada, b_ada, norm1, norm2, norm_f, w_in_even, conv_a_w, a_log, dt_bias, o_gain_a, conv_b_w, w_out_even, w_qkv_odd, w_out_odd, w_router_group, b_router_group, w_router_expert, b_router_expert, w_exp_gate, w_exp_up, w_exp_down):
    p = dict(norm1=norm1, norm2=norm2, norm_f=norm_f, w_in_even=w_in_even, conv_a_w=conv_a_w,
             a_log=a_log, dt_bias=dt_bias, o_gain_a=o_gain_a, conv_b_w=conv_b_w, w_out_even=w_out_even,
             w_qkv_odd=w_qkv_odd, w_out_odd=w_out_odd, w_router_group=w_router_group,
             b_router_group=b_router_group, w_router_expert=w_router_expert,
             b_router_expert=b_router_expert, w_exp_gate=w_exp_gate, w_exp_up=w_exp_up,
             w_exp_down=w_exp_down)
    w = _prep_weights(p)
    nb = x_prompt.shape[0]
    ns = x_sample.shape[0]
    mod_all = _ada_mod(jnp.concatenate([c_prompt, c_sample], axis=0), w_ada, b_ada)
    depth = w_ada.shape[0]
    mod_p = [mod_all[l, :nb] for l in range(depth)]
    mod_s = [mod_all[l, nb:] for l in range(depth)]
    n_even = state_a_ssm.shape[0]
    zero_ssm = jnp.zeros((n_even, nb, H_A, DK_A, DV_A), F32)
    zero_aconv = jnp.zeros((n_even, nb, CONV_A - 1, QKV_A), F32)
    zero_bconv = jnp.zeros((n_even, nb, CONV_B - 1, D_B), F32)
    y_p, ssm_p, aconv_p, bconv_p, k_p, v_p = _trunk(x_prompt, mod_p, zero_ssm, zero_aconv, zero_bconv,
                                                    None, None, p, w, False)
    y_s, ssm_s, aconv_s, bconv_s, k_s, v_s = _trunk(x_sample, mod_s, state_a_ssm, state_a_conv,
                                                    state_b_conv, cache_c_k, cache_c_v, p, w, True)
    return (y_p, y_s, ssm_p, ssm_s, aconv_p, aconv_s, bconv_p, bconv_s, k_p, k_s, v_p, v_s)
```

```python
import functools
import math

import jax
import jax.numpy as jnp
from jax import lax
from jax.experimental import pallas as pl
from jax.experimental.pallas import tpu as pltpu

F32 = jnp.float32
BF16 = jnp.bfloat16
I32 = jnp.int32

EPS = 1e-6
NEG_BIG = -1e30

LANES = 128
SUBLANES = 8
VMEM_BYTES_V7X = 64 * 1024 * 1024
VMEM_LIMIT = 56 * 1024 * 1024

H_A, DK_A, DV_A = 4, 128, 128
CONV_A, CONV_B = 4, 3
QKV_A = H_A * (2 * DK_A + DV_A)
D_B = 512
H_C, HD_C = 16, 64
BRANCHES = ((128, 1), (512, 4), (2048, 16))
N_GROUPS, E_PER_GROUP = 4, 8
N_EXPERTS = N_GROUPS * E_PER_GROUP
CHUNK_PROMPT = 64

TOK_TILE = 512
MOE_TOK_TILE = 512
MOE_ROW_TILE_BIG = 512
MOE_ROW_TILE_SMALL = 128
ATT_BLK = 256
ATT_NEAR_BLOCKS = 3
LOG2_E = math.log2(math.e)
SEQ_BLOCK_SCAN = 8
SAMPLE_HEADS_PER_STEP = 16


def _cparams(sem, vmem=None):
    return pltpu.CompilerParams(dimension_semantics=sem, vmem_limit_bytes=vmem)


def _silu(x):
    return x * jax.nn.sigmoid(x)


def _bdot(a, b):
    return jnp.dot(a.astype(BF16), b.astype(BF16), preferred_element_type=F32)


def _bdot_nt(a, b):
    return lax.dot_general(a.astype(BF16), b.astype(BF16), (((1,), (1,)), ((), ())),
                           preferred_element_type=F32)


def _bdot_tn(a, b):
    return lax.dot_general(a.astype(BF16), b.astype(BF16), (((0,), (0,)), ((), ())),
                           preferred_element_type=F32)


def _split_bf16(x):
    hi = x.astype(BF16)
    return hi, (x - hi.astype(F32)).astype(BF16)


def _dot3(a, b):
    ah, al = _split_bf16(a)
    bh, bl = _split_bf16(b)
    dot = functools.partial(jnp.dot, preferred_element_type=F32)
    return dot(ah, bh) + (dot(ah, bl) + dot(al, bh))


def _fdot(a, b):
    return jnp.dot(a, b, preferred_element_type=F32, precision=lax.Precision.HIGHEST)


def _fdot_nt(a, b):
    return lax.dot_general(a, b, (((1,), (1,)), ((), ())), preferred_element_type=F32,
                           precision=lax.Precision.HIGHEST)


def _tok_tiling(n_seq, length, tile):
    if length >= tile:
        sb, lb = 1, tile
    else:
        sb, lb = tile // length, length
    nl = length // lb
    return sb, lb, (n_seq // sb) * nl, nl


def _norm_mod(x, gain, sc, sh):
    sb, lb, d = x.shape
    y = x * lax.rsqrt(jnp.mean(x * x, axis=-1, keepdims=True) + EPS) * gain
    h = y * (1.0 + sc) + sh
    return h.reshape(sb * lb, d)


def _ada_kernel(c_ref, w_ref, b_ref, o_ref):
    c = _silu(c_ref[...])
    o_ref[0] = _bdot(c, w_ref[0]) + b_ref[0]


def _ada_mod(c_all, w_ada, b_ada):
    r, d = c_all.shape
    depth, _, n6 = w_ada.shape
    tn = 1536
    return pl.pallas_call(
        _ada_kernel,
        grid=(depth, n6 // tn),
        in_specs=[pl.BlockSpec((r, d), lambda l, j: (0, 0)),
                  pl.BlockSpec((1, d, tn), lambda l, j: (l, 0, j)),
                  pl.BlockSpec((1, 1, tn), lambda l, j: (l, 0, j))],
        out_specs=pl.BlockSpec((1, r, tn), lambda l, j: (l, 0, j)),
        out_shape=jax.ShapeDtypeStruct((depth, r, n6), F32),
        compiler_params=_cparams(("arbitrary", "arbitrary"), VMEM_LIMIT),
        name="ada_mod",
    )(c_all, w_ada, b_ada.reshape(depth, 1, n6))


def _nmm_kernel(x_ref, sh_ref, sc_ref, gain_ref, w_ref, *o_refs, splits, transposed):
    h = _norm_mod(x_ref[...], gain_ref[...], sc_ref[...], sh_ref[...]).astype(BF16)
    off = 0
    vals = []
    for o_ref, n in zip(o_refs, splits):
        vals.append(jnp.dot(h, w_ref[:, off:off + n], preferred_element_type=F32))
        o_ref[...] = vals[-1]
        off += n
    for o_ref, j in zip(o_refs[len(splits):], transposed):
        o_ref[0] = vals[j].T


def _norm_mod_matmul(x3, mod3, k_sh, k_sc, gain, w_bf16, splits, transposed=()):
    n_seq, length, d = x3.shape
    sb, lb, steps, nl = _tok_tiling(n_seq, length, TOK_TILE)
    rows = sb * lb
    t = n_seq * length
    n_out = w_bf16.shape[1]
    assert sum(splits) == n_out and (not transposed or sb == 1)
    t_specs = [pl.BlockSpec((1, splits[j], lb), lambda i: (i // nl, 0, i % nl)) for j in transposed]
    t_shapes = [jax.ShapeDtypeStruct((n_seq, splits[j], length), F32) for j in transposed]
    return pl.pallas_call(
        functools.partial(_nmm_kernel, splits=splits, transposed=transposed),
        grid=(steps,),
        in_specs=[pl.BlockSpec((sb, lb, d), lambda i: (i // nl, i % nl, 0)),
                  pl.BlockSpec((sb, 1, d), lambda i: (i // nl, 0, k_sh)),
                  pl.BlockSpec((sb, 1, d), lambda i: (i // nl, 0, k_sc)),
                  pl.BlockSpec((1, d), lambda i: (0, 0)),
                  pl.BlockSpec((d, n_out), lambda i: (0, 0))],
        out_specs=[pl.BlockSpec((rows, n), lambda i: (i, 0)) for n in splits] + t_specs,
        out_shape=[jax.ShapeDtypeStruct((t, n), F32) for n in splits] + t_shapes,
        compiler_params=_cparams(("arbitrary",), VMEM_LIMIT),
        name="norm_mod_matmul",
    )(x3, mod3, mod3, gain.reshape(1, d), w_bf16)


def _delta_prep_kernel(qkv_ref, halo_ref, hist_ref, ab_ref, cw_ref, al_ref, dtb_ref, lt_ref, sel_ref,
                       ub_ref, wm_ref, qd_ref, kd_ref, qk_ref, dec_ref, ext_ref,
                       *, g_chunks, chunk, chunks_are_seqs, chunks_per_seq):
    c = chunk
    step = pl.program_id(0)
    cw = cw_ref[...]
    row = lax.broadcasted_iota(I32, (c, c), 0)
    col = lax.broadcasted_iota(I32, (c, c), 1)
    lower_incl = row >= col
    strict = row > col
    eye = (row == col).astype(F32)
    n_fac = max(int(math.ceil(math.log2(c))), 1)

    qkvs, abs_ = [], []
    for g in range(g_chunks):
        cur = qkv_ref[g] if chunks_are_seqs else qkv_ref[0, g * c:(g + 1) * c, :]
        if chunks_are_seqs:
            prev = hist_ref[g]
        elif g == 0:
            first = (step % (chunks_per_seq // g_chunks)) == 0
            prev = jnp.where(first, hist_ref[0], halo_ref[0])
        else:
            prev = qkv_ref[0, g * c - SUBLANES:g * c, :]
        ext_ref[g, 0:SUBLANES, :] = prev
        ext_ref[g, SUBLANES:SUBLANES + c, :] = cur
        conv = ext_ref[g, pl.ds(SUBLANES - (CONV_A - 1), c), :] * cw[0:1, :]
        for j in range(1, CONV_A):
            conv = conv + ext_ref[g, pl.ds(SUBLANES - (CONV_A - 1) + j, c), :] * cw[j:j + 1, :]
        qkvs.append(_silu(conv))
        abs_.append(ab_ref[g] if chunks_are_seqs else ab_ref[0, g * c:(g + 1) * c, :])

    g_alls = [-jnp.exp(al_ref[...]) * jax.nn.softplus(ab + dtb_ref[...]) for ab in abs_]
    sigs = [jax.nn.sigmoid(ab) for ab in abs_]
    gcum_alls = [_fdot(lt_ref[...], ga) for ga in g_alls]
    gcum_rows = [_fdot_nt(sel_ref[...], gc) for gc in gcum_alls]

    chains = [(g, h) for g in range(g_chunks) for h in range(H_A)]
    qs, ks, vs, betas, gcs, gammas, egs = [], [], [], [], [], [], []
    for g, h in chains:
        qkv = qkvs[g]
        q = qkv[:, h * DK_A:(h + 1) * DK_A]
        k = qkv[:, H_A * DK_A + h * DK_A:H_A * DK_A + (h + 1) * DK_A]
        v = qkv[:, 2 * H_A * DK_A + h * DV_A:2 * H_A * DK_A + (h + 1) * DV_A]
        qs.append(q * lax.rsqrt(jnp.sum(q * q, axis=-1, keepdims=True) + EPS) * (DK_A ** -0.5))
        ks.append(k * lax.rsqrt(jnp.sum(k * k, axis=-1, keepdims=True) + EPS))
        vs.append(v)
        betas.append(sigs[g][:, H_A + h:H_A + h + 1])
        gc = gcum_alls[g][:, h:h + 1]
        gr = gcum_rows[g][h:h + 1, :]
        gcs.append(gc)
        gammas.append(jnp.exp(jnp.where(lower_incl, gc - gr, NEG_BIG)))
        egs.append(jnp.exp(gc))

    kq = [_bdot_nt(jnp.concatenate([k, q], axis=0), k) for k, q in zip(ks, qs)]
    a_mats = [jnp.where(strict, b * x[0:c] * gm, 0.0) for b, x, gm in zip(betas, kq, gammas)]
    qkms = [x[c:2 * c] * gm for x, gm in zip(kq, gammas)]
    m_pows = [-a for a in a_mats]
    t_invs = [eye + m for m in m_pows]
    for _ in range(n_fac - 1):
        m_pows = [_bdot(m, m) for m in m_pows]
        t_invs = [t + _bdot(t, m) for t, m in zip(t_invs, m_pows)]
    resids = [eye - t - _dot3(a, t) for a, t in zip(a_mats, t_invs)]
    t_invs = [t + _bdot(t, r) for t, r in zip(t_invs, resids)]
    rhss = [jnp.concatenate([v * b, k * (b * eg)], axis=-1) for v, k, b, eg in zip(vs, ks, betas, egs)]
    sols = [rhs + _dot3(t - eye, rhs) for t, rhs in zip(t_invs, rhss)]

    for g in range(g_chunks):
        idx = [i for i, (gg, _) in enumerate(chains) if gg == g]
        g_last = [gcum_alls[g][c - 1:c, h:h + 1] for h in range(H_A)]

        def put(ref, parts):
            val = jnp.concatenate(parts, axis=-1)
            if chunks_are_seqs:
                ref[g] = val
            else:
                ref[0, g * c:(g + 1) * c, :] = val

        put(ub_ref, [sols[i][:, :DV_A] for i in idx])
        put(wm_ref, [sols[i][:, DV_A:] for i in idx])
        put(qd_ref, [qs[i] * egs[i] for i in idx])
        put(kd_ref, [ks[i] * jnp.exp(g_last[h] - gcs[i]) for h, i in enumerate(idx)])
        put(qk_ref, [qkms[i] for i in idx])
        dec_ref[g] = jnp.concatenate([jnp.broadcast_to(jnp.exp(gl), (SUBLANES, DV_A)) for gl in g_last], axis=-1)


def _delta_prep(qkv_pre3, hist8, ab3, conv_w, a_log, dt_bias, chunk, g_chunks):
    n_seq, length, w = qkv_pre3.shape
    c = chunk
    chunks_are_seqs = (length == c)
    nc = length // c
    if chunks_are_seqs:
        steps = n_seq // g_chunks
        blk = (g_chunks, c, w)
        x_map = lambda i: (i, 0, 0)
        halo_map = lambda i: (i, 0, 0)
        hist_spec = pl.BlockSpec((g_chunks, SUBLANES, w), lambda i: (i, 0, 0))
        ab_spec = pl.BlockSpec((g_chunks, c, LANES), lambda i: (i, 0, 0))
        out_map = lambda i: (i, 0, 0)
        out_rows = (g_chunks, c)
    else:
        rows = g_chunks * c
        spb = nc // g_chunks
        steps = n_seq * spb
        blk = (1, rows, w)
        x_map = lambda i: (i // spb, i % spb, 0)
        halo_map = lambda i: (i // spb, jnp.maximum((i % spb) * (rows // SUBLANES) - 1, 0), 0)
        hist_spec = pl.BlockSpec((1, SUBLANES, w), lambda i: (i // spb, 0, 0))
        ab_spec = pl.BlockSpec((1, rows, LANES), lambda i: (i // spb, i % spb, 0))
        out_map = x_map
        out_rows = (1, rows)
    lt = jnp.tril(jnp.ones((c, c), F32))
    sel = jnp.eye(SUBLANES, LANES, dtype=F32)
    hv = H_A * DV_A
    al = jnp.zeros((1, LANES), F32).at[0, :H_A].set(a_log)
    dtb = jnp.zeros((1, LANES), F32).at[0, :H_A].set(dt_bias)
    kern = functools.partial(_delta_prep_kernel, g_chunks=g_chunks, chunk=c,
                             chunks_are_seqs=chunks_are_seqs, chunks_per_seq=nc)
    big = lambda width: pl.BlockSpec(out_rows + (width,), out_map)
    return pl.pallas_call(
        kern,
        grid=(steps,),
        in_specs=[pl.BlockSpec(blk, x_map),
                  pl.BlockSpec((1, SUBLANES, w), halo_map) if not chunks_are_seqs
                  else pl.BlockSpec((g_chunks, SUBLANES, w), halo_map),
                  hist_spec, ab_spec,
                  pl.BlockSpec((CONV_A, w), lambda i: (0, 0)),
                  pl.BlockSpec((1, LANES), lambda i: (0, 0)),
                  pl.BlockSpec((1, LANES), lambda i: (0, 0)),
                  pl.BlockSpec((c, c), lambda i: (0, 0)),
                  pl.BlockSpec((SUBLANES, LANES), lambda i: (0, 0))],
        out_specs=[big(hv), big(hv), big(hv), big(hv), big(H_A * c),
                   pl.BlockSpec((g_chunks, SUBLANES, hv), lambda i: (i, 0, 0))],
        out_shape=[jax.ShapeDtypeStruct((n_seq, length, hv), F32)] * 4
        + [jax.ShapeDtypeStruct((n_seq, length, H_A * c), F32),
           jax.ShapeDtypeStruct((n_seq * nc, SUBLANES, hv), F32)],
        scratch_shapes=[pltpu.VMEM((g_chunks, SUBLANES + c, w), F32)],
        compiler_params=_cparams(("arbitrary",), VMEM_LIMIT),
        name="delta_prep",
    )(qkv_pre3, hist8 if chunks_are_seqs else qkv_pre3, hist8, ab3, conv_w, al, dtb, lt, sel)


def _delta_scan_kernel(ub_ref, wm_ref, qd_ref, kd_ref, qk_ref, dec_ref, z_ref, s0_ref, og_ref,
                       o_ref, sn_ref, s_ref, *, nb, chunk):
    c = chunk
    j = pl.program_id(1)

    @pl.when(j == 0)
    def _():
        s_ref[...] = s0_ref[...]

    og = og_ref[...]
    chains = [(b, h) for b in range(nb) for h in range(H_A)]
    sl = lambda h: slice(h * DV_A, (h + 1) * DV_A)
    states = [s_ref[b, h] for b, h in chains]
    ws = [_bdot(jnp.concatenate([wm_ref[b, :, sl(h)], qd_ref[b, :, sl(h)]], axis=0), s)
          for (b, h), s in zip(chains, states)]
    us = [ub_ref[b, :, sl(h)] - x[0:c] for (b, h), x in zip(chains, ws)]
    os_ = [x[c:2 * c] + _bdot(qk_ref[b, :, h * c:(h + 1) * c], u) for (b, h), x, u in zip(chains, ws, us)]
    for (b, h), s, u in zip(chains, states, us):
        s_ref[b, h] = s * dec_ref[b, 0, 0:1, sl(h)] + _bdot_tn(kd_ref[b, :, sl(h)], u)
    for b in range(nb):
        outs = []
        for h in range(H_A):
            o = os_[b * H_A + h]
            on = o * lax.rsqrt(jnp.mean(o * o, axis=-1, keepdims=True) + EPS) * og
            outs.append(on * _silu(z_ref[b, :, sl(h)]))
        o_ref[b] = jnp.concatenate(outs, axis=-1)
    sn_ref[...] = s_ref[...]


def _delta_scan(ub, wm, qd, kd, qk, dec, z3, s0, o_gain, chunk):
    n_seq, length, hv = ub.shape
    c = chunk
    nc = length // c
    nb = SEQ_BLOCK_SCAN
    dec4 = dec.reshape(n_seq, nc, SUBLANES, hv)
    tok = lambda width: pl.BlockSpec((nb, c, width), lambda b, j: (b, j, 0))
    st = pl.BlockSpec((nb, H_A, DK_A, DV_A), lambda b, j: (b, 0, 0, 0))
    return pl.pallas_call(
        functools.partial(_delta_scan_kernel, nb=nb, chunk=c),
        grid=(n_seq // nb, nc),
        in_specs=[tok(hv), tok(hv), tok(hv), tok(hv), tok(H_A * c),
                  pl.BlockSpec((nb, 1, SUBLANES, hv), lambda b, j: (b, j, 0, 0)),
                  tok(hv), st, pl.BlockSpec((1, DV_A), lambda b, j: (0, 0))],
        out_specs=[tok(hv), st],
        out_shape=[jax.ShapeDtypeStruct((n_seq, length, hv), F32),
                   jax.ShapeDtypeStruct((n_seq, H_A, DK_A, DV_A), F32)],
        scratch_shapes=[pltpu.VMEM((nb, H_A, DK_A, DV_A), F32)],
        compiler_params=_cparams(("arbitrary", "arbitrary"), VMEM_LIMIT),
        name="delta_scan",
    )(ub, wm, qd, kd, qk, dec4, z3, s0, o_gain.reshape(1, DV_A))


def _even_out_kernel(gcx_ref, halo_ref, hist_ref, oa_ref, x_ref, g1_ref, cw_ref, w_ref,
                     xo_ref, tail_ref, ext_ref, *, sb, lb, nl):
    step = pl.program_id(0)
    gcx = gcx_ref[...]
    b_gate = gcx[:, :, 0:D_B]
    bx = gcx[:, :, D_B:2 * D_B] * gcx[:, :, 2 * D_B:3 * D_B]
    if nl == 1:
        prev = hist_ref[...]
    else:
        hb = halo_ref[...]
        first = (step % nl) == 0
        prev = jnp.where(first, hist_ref[...], hb[:, :, D_B:2 * D_B] * hb[:, :, 2 * D_B:3 * D_B])
    ext_ref[:, 0:SUBLANES, :] = prev
    ext_ref[:, SUBLANES:SUBLANES + lb, :] = bx
    cw = cw_ref[...]
    conv = ext_ref[:, pl.ds(SUBLANES - (CONV_B - 1), lb), :] * cw[0:1, :]
    for j in range(1, CONV_B):
        conv = conv + ext_ref[:, pl.ds(SUBLANES - (CONV_B - 1) + j, lb), :] * cw[j:j + 1, :]
    out_b = (b_gate * conv).reshape(sb * lb, D_B)
    hv = H_A * DV_A
    y = _bdot(oa_ref[...], w_ref[0:hv, :]) + _bdot(out_b, w_ref[hv:hv + D_B, :])
    d = y.shape[-1]
    xo_ref[...] = x_ref[...] + g1_ref[...] * y.reshape(sb, lb, d)
    tail_ref[...] = bx[:, lb - SUBLANES:lb, :]


def _even_out(gcx3, hist8, out_a2, x3, mod3, k_gate, conv_w, w_out_bf16):
    n_seq, length, d = x3.shape
    sb, lb, steps, nl = _tok_tiling(n_seq, length, TOK_TILE)
    rows = sb * lb
    w3 = gcx3.shape[-1]
    hv = H_A * DV_A
    halo_map = lambda i: (i // nl, jnp.maximum((i % nl) * (lb // SUBLANES) - 1, 0), 0)
    return pl.pallas_call(
        functools.partial(_even_out_kernel, sb=sb, lb=lb, nl=nl),
        grid=(steps,),
        in_specs=[pl.BlockSpec((sb, lb, w3), lambda i: (i // nl, i % nl, 0)),
                  pl.BlockSpec((sb, SUBLANES, w3), halo_map if nl > 1 else (lambda i: (i, 0, 0))),
                  pl.BlockSpec((sb, SUBLANES, D_B), lambda i: (i // nl, 0, 0)),
                  pl.BlockSpec((rows, hv), lambda i: (i, 0)),
                  pl.BlockSpec((sb, lb, d), lambda i: (i // nl, i % nl, 0)),
                  pl.BlockSpec((sb, 1, d), lambda i: (i // nl, 0, k_gate)),
                  pl.BlockSpec((CONV_B, D_B), lambda i: (0, 0)),
                  pl.BlockSpec((hv + D_B, d), lambda i: (0, 0))],
        out_specs=[pl.BlockSpec((sb, lb, d), lambda i: (i // nl, i % nl, 0)),
                   pl.BlockSpec((sb, SUBLANES, D_B), lambda i: (i // nl, 0, 0))],
        out_shape=[jax.ShapeDtypeStruct((n_seq, length, d), F32),
                   jax.ShapeDtypeStruct((n_seq, SUBLANES, D_B), F32)],
        scratch_shapes=[pltpu.VMEM((sb, SUBLANES + lb, D_B), F32)],
        compiler_params=_cparams(("arbitrary",), VMEM_LIMIT),
        name="even_out",
    )(gcx3, gcx3, hist8, out_a2, x3, mod3, conv_w, w_out_bf16)


def _proj_res_kernel(a_ref, x_ref, g_ref, w_ref, xo_ref, *, sb, lb):
    y = _bdot(a_ref[...], w_ref[...])
    xo_ref[...] = x_ref[...] + g_ref[...] * y.reshape(sb, lb, y.shape[-1])


def _proj_residual(a2, x3, mod3, k_gate, w_bf16):
    n_seq, length, d = x3.shape
    sb, lb, steps, nl = _tok_tiling(n_seq, length, TOK_TILE)
    rows = sb * lb
    ka = a2.shape[-1]
    return pl.pallas_call(
        functools.partial(_proj_res_kernel, sb=sb, lb=lb),
        grid=(steps,),
        in_specs=[pl.BlockSpec((rows, ka), lambda i: (i, 0)),
                  pl.BlockSpec((sb, lb, d), lambda i: (i // nl, i % nl, 0)),
                  pl.BlockSpec((sb, 1, d), lambda i: (i // nl, 0, k_gate)),
                  pl.BlockSpec((ka, d), lambda i: (0, 0))],
        out_specs=pl.BlockSpec((sb, lb, d), lambda i: (i // nl, i % nl, 0)),
        out_shape=jax.ShapeDtypeStruct((n_seq, length, d), F32),
        compiler_params=_cparams(("arbitrary",), VMEM_LIMIT),
        name="proj_residual",
    )(a2, x3, mod3, w_bf16)


def _router_kernel(x_ref, sh_ref, sc_ref, gain_ref, wrt_ref, bcol_ref, utri_ref,
                   mi_ref, gcol_ref, cnt_ref, xs0_ref, carry_ref):
    i = pl.program_id(0)

    @pl.when(i == 0)
    def _():
        carry_ref[...] = jnp.zeros_like(carry_ref)

    xs0_ref[...] = jnp.zeros_like(xs0_ref)

    h2 = _norm_mod(x_ref[...], gain_ref[...], sc_ref[...], sh_ref[...])
    tm = h2.shape[0]
    logits = _bdot_nt(wrt_ref[...], h2) + bcol_ref[...]
    lg = logits[0:SUBLANES]
    e = jnp.exp(lg - jnp.max(lg, axis=0, keepdims=True))
    pg = e / jnp.sum(e, axis=0, keepdims=True)
    p_top = jnp.max(pg, axis=0, keepdims=True)
    rid8 = lax.broadcasted_iota(I32, (SUBLANES, tm), 0)
    g_top = jnp.min(jnp.where(pg == p_top, rid8, SUBLANES), axis=0, keepdims=True)
    le = logits[SUBLANES:SUBLANES + N_EXPERTS]
    sel = jnp.zeros((E_PER_GROUP, tm), F32)
    for gi in range(N_GROUPS):
        sel = sel + jnp.where(g_top == gi, le[gi * E_PER_GROUP:(gi + 1) * E_PER_GROUP], 0.0)
    e2 = jnp.exp(sel - jnp.max(sel, axis=0, keepdims=True))
    p_in = e2 / jnp.sum(e2, axis=0, keepdims=True)
    w_a = jnp.max(p_in, axis=0, keepdims=True)
    i_a = jnp.min(jnp.where(p_in == w_a, rid8, SUBLANES), axis=0, keepdims=True)
    p_rest = jnp.where(rid8 == i_a, -1.0, p_in)
    w_b = jnp.max(p_rest, axis=0, keepdims=True)
    i_b = jnp.min(jnp.where(p_rest == w_b, rid8, SUBLANES), axis=0, keepdims=True)
    den = w_a + w_b
    gate1 = p_top * (w_a / den)
    gate2 = p_top * (w_b / den)
    ex1 = g_top * E_PER_GROUP + i_a
    ex2 = g_top * E_PER_GROUP + i_b

    rid32 = lax.broadcasted_iota(I32, (N_EXPERTS, tm), 0)
    oh1 = rid32 == ex1
    oh2 = rid32 == ex2
    ohc = jnp.where(oh1, 1.0, jnp.where(oh2, 1.0, 0.0))
    cum = _bdot(ohc, utri_ref[...])
    carry = carry_ref[...]
    base = cum - ohc + carry[:, 0:1]
    r1 = jnp.sum(jnp.where(oh1, base, 0.0), axis=0, keepdims=True)
    r2 = jnp.sum(jnp.where(oh2, base, 0.0), axis=0, keepdims=True)
    new_carry = carry + cum[:, tm - 1:tm]
    carry_ref[...] = new_carry
    cnt_ref[...] = new_carry

    bro = lambda v: jnp.broadcast_to(v, (SUBLANES, tm))
    mi_ref[...] = jnp.where(rid8 == 0, bro(ex1),
                            jnp.where(rid8 == 1, bro(ex2),
                                      jnp.where(rid8 == 2, bro(r1.astype(I32)),
                                                jnp.where(rid8 == 3, bro(r2.astype(I32)), 0))))
    rid128 = lax.broadcasted_iota(I32, (LANES, tm), 0)
    g128 = jnp.where(rid128 == 0, jnp.broadcast_to(gate1, (LANES, tm)),
                     jnp.where(rid128 == 1, jnp.broadcast_to(gate2, (LANES, tm)), 0.0))
    gcol_ref[...] = g128.T


def _moe_router(x3, mod3, k_sh, k_sc, gain, wrt_bf16, bcol, xs_rows):
    n_seq, length, d = x3.shape
    sb, lb, steps, nl = _tok_tiling(n_seq, length, TOK_TILE)
    tm = sb * lb
    t = n_seq * length
    utri = jnp.triu(jnp.ones((tm, tm), F32)).astype(BF16)
    r_rows = wrt_bf16.shape[0]
    assert xs_rows % (steps * SUBLANES) == 0
    zrows = xs_rows // steps
    return pl.pallas_call(
        _router_kernel,
        grid=(steps,),
        in_specs=[pl.BlockSpec((sb, lb, d), lambda i: (i // nl, i % nl, 0)),
                  pl.BlockSpec((sb, 1, d), lambda i: (i // nl, 0, k_sh)),
                  pl.BlockSpec((sb, 1, d), lambda i: (i // nl, 0, k_sc)),
                  pl.BlockSpec((1, d), lambda i: (0, 0)),
                  pl.BlockSpec((r_rows, d), lambda i: (0, 0)),
                  pl.BlockSpec((r_rows, 1), lambda i: (0, 0)),
                  pl.BlockSpec((tm, tm), lambda i: (0, 0))],
        out_specs=[pl.BlockSpec((SUBLANES, tm), lambda i: (0, i)),
                   pl.BlockSpec((tm, LANES), lambda i: (i, 0)),
                   pl.BlockSpec((N_EXPERTS, LANES), lambda i: (0, 0)),
                   pl.BlockSpec((zrows, LANES), lambda i: (i, 0))],
        out_shape=[jax.ShapeDtypeStruct((SUBLANES, t), I32),
                   jax.ShapeDtypeStruct((t, LANES), F32),
                   jax.ShapeDtypeStruct((N_EXPERTS, LANES), F32),
                   jax.ShapeDtypeStruct((xs_rows, LANES), F32)],
        scratch_shapes=[pltpu.VMEM((N_EXPERTS, LANES), F32)],
        compiler_params=_cparams(("arbitrary",), VMEM_LIMIT),
        name="moe_router",
    )(x3, mod3, mod3, gain.reshape(1, d), wrt_bf16, bcol, utri)


def _plan_kernel(mi_ref, cnt_ref, ltri_ref, pos_ref, tinfo_ref, *, n_tiles_pad, row_tile):
    cnt = cnt_ref[...]
    nt = jnp.floor((cnt + (row_tile - 1)) * (1.0 / row_tile))
    tstart = _bdot(ltri_ref[...], nt)
    mi = mi_ref[...]
    tm = mi.shape[1]
    rid32 = lax.broadcasted_iota(I32, (N_EXPERTS, tm), 0)
    ts_col = tstart[:, 0:1]

    def pos_of(ex, rank):
        start = jnp.sum(jnp.where(rid32 == ex, ts_col, 0.0), axis=0, keepdims=True)
        return start.astype(I32) * row_tile + rank

    p1 = pos_of(mi[0:1], mi[2:3])
    p2 = pos_of(mi[1:2], mi[3:4])
    rid8 = lax.broadcasted_iota(I32, (SUBLANES, tm), 0)
    pos_ref[0] = jnp.where(rid8 == 0, jnp.broadcast_to(p1, (SUBLANES, tm)),
                           jnp.where(rid8 == 1, jnp.broadcast_to(p2, (SUBLANES, tm)), 0))
    tend_col = ts_col + nt[:, 0:1]
    jt = lax.broadcasted_iota(I32, (N_EXPERTS, n_tiles_pad), 1).astype(F32)
    te = jnp.sum(jnp.where(tend_col <= jt, 1.0, 0.0), axis=0, keepdims=True)
    te = jnp.minimum(te, N_EXPERTS - 1.0).astype(I32)
    total = jnp.sum(nt[:, 0:1], axis=0, keepdims=True).astype(I32)
    rid8t = lax.broadcasted_iota(I32, (SUBLANES, n_tiles_pad), 0)
    tinfo_ref[...] = jnp.where(rid8t == 0, jnp.broadcast_to(te, (SUBLANES, n_tiles_pad)),
                               jnp.broadcast_to(total, (SUBLANES, n_tiles_pad)))


def _moe_plan(meta_i, counts, n_tiles_pad, row_tile):
    t = meta_i.shape[1]
    tm = MOE_TOK_TILE
    steps = t // tm
    ltri = jnp.tril(jnp.ones((N_EXPERTS, N_EXPERTS), F32), k=-1).astype(BF16)
    return pl.pallas_call(
        functools.partial(_plan_kernel, n_tiles_pad=n_tiles_pad, row_tile=row_tile),
        grid=(steps,),
        in_specs=[pl.BlockSpec((SUBLANES, tm), lambda i: (0, i)),
                  pl.BlockSpec((N_EXPERTS, LANES), lambda i: (0, 0)),
                  pl.BlockSpec((N_EXPERTS, N_EXPERTS), lambda i: (0, 0))],
        out_specs=[pl.BlockSpec((1, SUBLANES, tm), lambda i: (i, 0, 0)),
                   pl.BlockSpec((SUBLANES, n_tiles_pad), lambda i: (0, 0))],
        out_shape=[jax.ShapeDtypeStruct((steps, SUBLANES, tm), I32),
                   jax.ShapeDtypeStruct((SUBLANES, n_tiles_pad), I32)],
        compiler_params=_cparams(("arbitrary",)),
        name="moe_plan",
    )(meta_i, counts, ltri)


def _to_token_tiles(ref, val):
    tm, d = val.shape
    for s in range(d // LANES):
        ref[pl.ds(s, tm, stride=SUBLANES), :] = val[:, s * LANES:(s + 1) * LANES]


def _from_token_tiles(ref, tm):
    return jnp.concatenate([ref[pl.ds(s, tm, stride=SUBLANES), :] for s in range(SUBLANES)], axis=-1)


def _tile_rows(idx):
    return pl.ds(pl.multiple_of(idx * SUBLANES, SUBLANES), SUBLANES)


def _row_copy_out(buf_ref, slot, r, dst_hbm, p, sem):
    return pltpu.make_async_copy(buf_ref.at[slot, _tile_rows(r), :], dst_hbm.at[_tile_rows(p), :], sem.at[slot])


def _dispatch_kernel(x_ref, sh_ref, sc_ref, gain_ref, pos_ref, xs_in_ref, xs_ref,
                     buf_ref, idx_ref, sem_ref, isem_ref, *, tm, steps):
    del xs_in_ref
    i = pl.program_id(0)
    slot = i % 2

    def drain(s):
        for _ in range(2):
            pltpu.make_async_copy(buf_ref.at[s], xs_ref.at[pl.ds(0, tm * SUBLANES), :], sem_ref.at[s]).wait()

    @pl.when(i >= 2)
    def _():
        drain(slot)

    _to_token_tiles(buf_ref.at[slot], _norm_mod(x_ref[...], gain_ref[...], sc_ref[...], sh_ref[...]))
    cp = pltpu.make_async_copy(pos_ref.at[0], idx_ref, isem_ref)
    cp.start()
    cp.wait()

    def issue(r, carry):
        _row_copy_out(buf_ref, slot, r, xs_ref, idx_ref[0, r], sem_ref).start()
        _row_copy_out(buf_ref, slot, r, xs_ref, idx_ref[1, r], sem_ref).start()
        return carry
    lax.fori_loop(0, tm, issue, 0, unroll=8)

    @pl.when(i == steps - 1)
    def _():
        drain(slot)
        if steps > 1:
            drain(1 - slot)


def _moe_dispatch(x3, mod3, k_sh, k_sc, gain, pos3, xs0):
    n_seq, length, d = x3.shape
    assert d == SUBLANES * LANES
    sb, lb, steps, nl = _tok_tiling(n_seq, length, MOE_TOK_TILE)
    tm = sb * lb
    return pl.pallas_call(
        functools.partial(_dispatch_kernel, tm=tm, steps=steps),
        grid=(steps,),
        in_specs=[pl.BlockSpec((sb, lb, d), lambda i: (i // nl, i % nl, 0)),
                  pl.BlockSpec((sb, 1, d), lambda i: (i // nl, 0, k_sh)),
                  pl.BlockSpec((sb, 1, d), lambda i: (i // nl, 0, k_sc)),
                  pl.BlockSpec((1, d), lambda i: (0, 0)),
                  pl.BlockSpec((1, SUBLANES, tm), lambda i: (i, 0, 0)),
                  pl.BlockSpec(memory_space=pl.ANY)],
        out_specs=pl.BlockSpec(memory_space=pl.ANY),
        out_shape=jax.ShapeDtypeStruct(xs0.shape, F32),
        scratch_shapes=[pltpu.VMEM((2, tm * SUBLANES, LANES), F32),
                        pltpu.SMEM((SUBLANES, tm), I32),
                        pltpu.SemaphoreType.DMA((2,)),
                        pltpu.SemaphoreType.DMA(())],
        input_output_aliases={5: 0},
        compiler_params=_cparams(("arbitrary",), VMEM_LIMIT),
        name="moe_dispatch",
    )(x3, mod3, mod3, gain.reshape(1, d), pos3, xs0)


def _expert_kernel(te_ref, nt_ref, xs_ref, wg_ref, wu_ref, wd_ref, ys_ref, *, row_tile):
    i = pl.program_id(0)

    @pl.when(i < nt_ref[0])
    def _():
        x = _from_token_tiles(xs_ref, row_tile).astype(BF16)
        hid = _silu(_bdot(x, wg_ref[...])) * _bdot(x, wu_ref[...])
        _to_token_tiles(ys_ref, _bdot(hid, wd_ref[...]))

    @pl.when(i >= nt_ref[0])
    def _():
        ys_ref[...] = jnp.zeros_like(ys_ref)


def _moe_experts(xs, tile_expert, n_tiles, wg, wu, wd, layer, row_tile):
    d, f = wg.shape[-2], wg.shape[-1]
    steps = xs.shape[0] // (row_tile * SUBLANES)
    x_map = lambda i, te, nt: (jnp.minimum(i, nt[0] - 1), 0)
    w_map = lambda i, te, nt: (layer, te[i] // E_PER_GROUP, te[i] % E_PER_GROUP, 0, 0)
    grid_spec = pltpu.PrefetchScalarGridSpec(
        num_scalar_prefetch=2,
        grid=(steps,),
        in_specs=[pl.BlockSpec((row_tile * SUBLANES, LANES), x_map),
                  pl.BlockSpec((None, None, None, d, f), w_map),
                  pl.BlockSpec((None, None, None, d, f), w_map),
                  pl.BlockSpec((None, None, None, f, d), w_map)],
        out_specs=pl.BlockSpec((row_tile * SUBLANES, LANES), lambda i, te, nt: (i, 0)),
    )
    return pl.pallas_call(
        functools.partial(_expert_kernel, row_tile=row_tile),
        grid_spec=grid_spec,
        out_shape=jax.ShapeDtypeStruct(xs.shape, F32),
        compiler_params=_cparams(("arbitrary",), VMEM_LIMIT),
        name="moe_experts",
    )(tile_expert, n_tiles, xs, wg, wu, wd)


def _row_copy_in(src_hbm, p, buf_ref, slot, which, r, sem):
    return pltpu.make_async_copy(src_hbm.at[_tile_rows(p), :], buf_ref.at[slot, which, _tile_rows(r), :],
                                 sem.at[slot])


def _combine_kernel(x_ref, g_ref, gcol_ref, pos_ref, posn_ref, ys_ref, gainf_ref, xo_ref,
                    buf_ref, idx_ref, sem_ref, isem_ref, *, tm, steps, sb, lb, final_norm):
    i = pl.program_id(0)
    slot = i % 2

    def issue(pref, s):
        cp = pltpu.make_async_copy(pref.at[0], idx_ref, isem_ref)
        cp.start()
        cp.wait()

        def body(r, carry):
            _row_copy_in(ys_ref, idx_ref[0, r], buf_ref, s, 0, r, sem_ref).start()
            _row_copy_in(ys_ref, idx_ref[1, r], buf_ref, s, 1, r, sem_ref).start()
            return carry
        lax.fori_loop(0, tm, body, 0, unroll=8)

    @pl.when(i == 0)
    def _():
        issue(pos_ref, slot)

    @pl.when(i + 1 < steps)
    def _():
        issue(posn_ref, 1 - slot)

    for which in range(2):
        pltpu.make_async_copy(ys_ref.at[pl.ds(0, tm * SUBLANES), :], buf_ref.at[slot, which],
                              sem_ref.at[slot]).wait()

    gc = gcol_ref[...]
    y = gc[:, 0:1] * _from_token_tiles(buf_ref.at[slot, 0], tm) \
        + gc[:, 1:2] * _from_token_tiles(buf_ref.at[slot, 1], tm)
    d = y.shape[-1]
    xn = x_ref[...] + g_ref[...] * y.reshape(sb, lb, d)
    if final_norm:
        xn = xn * lax.rsqrt(jnp.mean(xn * xn, axis=-1, keepdims=True) + EPS) * gainf_ref[...]
    xo_ref[...] = xn


def _moe_combine(x3, mod3, k_gate, gates_col, pos3, ys, gain_f, final_norm):
    n_seq, length, d = x3.shape
    sb, lb, steps, nl = _tok_tiling(n_seq, length, MOE_TOK_TILE)
    tm = sb * lb
    return pl.pallas_call(
        functools.partial(_combine_kernel, tm=tm, steps=steps, sb=sb, lb=lb, final_norm=final_norm),
        grid=(steps,),
        in_specs=[pl.BlockSpec((sb, lb, d), lambda i: (i // nl, i % nl, 0)),
                  pl.BlockSpec((sb, 1, d), lambda i: (i // nl, 0, k_gate)),
                  pl.BlockSpec((tm, LANES), lambda i: (i, 0)),
                  pl.BlockSpec((1, SUBLANES, tm), lambda i: (i, 0, 0)),
                  pl.BlockSpec((1, SUBLANES, tm), lambda i: (jnp.minimum(i + 1, steps - 1), 0, 0)),
                  pl.BlockSpec(memory_space=pl.ANY),
                  pl.BlockSpec((1, d), lambda i: (0, 0))],
        out_specs=pl.BlockSpec((sb, lb, d), lambda i: (i // nl, i % nl, 0)),
        out_shape=jax.ShapeDtypeStruct((n_seq, length, d), F32),
        scratch_shapes=[pltpu.VMEM((2, 2, tm * SUBLANES, LANES), F32),
                        pltpu.SMEM((SUBLANES, tm), I32),
                        pltpu.SemaphoreType.DMA((2,)),
                        pltpu.SemaphoreType.DMA(())],
        compiler_params=_cparams(("arbitrary",), VMEM_LIMIT),
        name="moe_combine",
    )(x3, mod3, gates_col, pos3, pos3, ys, gain_f.reshape(1, d))


def _hier_moe_residual(x3, mod3, k_sh, k_sc, k_gate, gain, wrt_bf16, bcol, wg, wu, wd, layer, gain_f, final_norm):
    n_seq, length, d = x3.shape
    t = n_seq * length
    row_tile = MOE_ROW_TILE_BIG if (2 * t) // N_EXPERTS >= MOE_ROW_TILE_BIG else MOE_ROW_TILE_SMALL
    max_tiles = (2 * t) // row_tile + N_EXPERTS
    n_tiles_pad = -(-max_tiles // LANES) * LANES
    meta_i, gates_col, counts, xs0 = _moe_router(x3, mod3, k_sh, k_sc, gain, wrt_bf16, bcol,
                                                 max_tiles * row_tile * SUBLANES)
    pos3, tinfo = _moe_plan(meta_i, counts, n_tiles_pad, row_tile)
    xs = _moe_dispatch(x3, mod3, k_sh, k_sc, gain, pos3, xs0)
    ys = _moe_experts(xs, tinfo[0, :max_tiles], tinfo[1, :1], wg, wu, wd, layer, row_tile)
    return _moe_combine(x3, mod3, k_gate, gates_col, pos3, ys, gain_f, final_norm)


def _multiplicity(dist):
    m = jnp.zeros(dist.shape, F32)
    for window, dil in BRANCHES:
        m = m + ((dist >= 0) & (dist <= window) & (dist % dil == 0)).astype(F32)
    return m


def _log_mult(dist):
    m = _multiplicity(dist)
    return jnp.where(m > 0, jnp.log(jnp.maximum(m, 1.0)), NEG_BIG)


def _attn_prompt_kernel(q_ref, k_ref, v_ref, nbias_ref, fbias_ref, o_ref, mf_ref, lf_ref, af_ref, *, length):
    blk = ATT_BLK
    nq = length // blk
    near = ATT_NEAR_BLOCKS
    ncls = BRANCHES[-1][1]
    csz = length // ncls
    has_far = nq > near
    scale = (HD_C ** -0.5) * LOG2_E
    nt = lambda a, b: lax.dot_general(a, b, (((1,), (1,)), ((), ())), preferred_element_type=F32)
    mm = lambda a, b: jnp.dot(a, b, preferred_element_type=F32)

    def split_heads(q):
        head0 = lax.broadcasted_iota(I32, q.shape, 1) < HD_C
        return head0, jnp.where(head0, q, 0.0).astype(BF16), jnp.where(head0, 0.0, q).astype(BF16)

    if has_far:
        fbias = fbias_ref[...]
        group = 4
        for r0 in range(0, ncls, group):
            rs = list(range(r0, r0 + group))
            rows = [pl.ds(r, csz, stride=ncls) for r in rs]
            hq = [split_heads(q_ref[0, rw, :] * scale) for rw in rows]
            kr = [k_ref[0, rw, :].astype(BF16) for rw in rows]
            vr = [v_ref[0, rw, :].astype(BF16) for rw in rows]
            s0 = [nt(h[1], k) + fbias for h, k in zip(hq, kr)]
            s1 = [nt(h[2], k) + fbias for h, k in zip(hq, kr)]
            m0 = [jnp.max(s, axis=-1, keepdims=True) for s in s0]
            m1 = [jnp.max(s, axis=-1, keepdims=True) for s in s1]
            p0 = [jnp.exp2(s - m) for s, m in zip(s0, m0)]
            p1 = [jnp.exp2(s - m) for s, m in zip(s1, m1)]
            a0 = [mm(p.astype(BF16), v) for p, v in zip(p0, vr)]
            a1 = [mm(p.astype(BF16), v) for p, v in zip(p1, vr)]
            for i, rw in enumerate(rows):
                head0 = hq[i][0]
                af_ref[rw, :] = jnp.where(head0, a0[i], a1[i])
                mf_ref[rw, :] = jnp.where(head0, m0[i], m1[i])
                lf_ref[rw, :] = jnp.where(head0, jnp.sum(p0[i], axis=-1, keepdims=True),
                                          jnp.sum(p1[i], axis=-1, keepdims=True))

    for qi in range(nq):
        rq = slice(qi * blk, (qi + 1) * blk)
        head0, q0, q1 = split_heads(q_ref[0, rq, :] * scale)
        js = list(range(max(0, qi - near + 1), qi + 1))
        ks = [k_ref[0, kj * blk:(kj + 1) * blk, :].astype(BF16) for kj in js]
        vs = [v_ref[0, kj * blk:(kj + 1) * blk, :].astype(BF16) for kj in js]
        s0 = [nt(q0, k) + nbias_ref[qi - kj] for k, kj in zip(ks, js)]
        s1 = [nt(q1, k) + nbias_ref[qi - kj] for k, kj in zip(ks, js)]
        far = has_far and qi >= near
        m0 = functools.reduce(jnp.maximum, [jnp.max(s, axis=-1, keepdims=True) for s in s0])
        m1 = functools.reduce(jnp.maximum, [jnp.max(s, axis=-1, keepdims=True) for s in s1])
        if far:
            mf = mf_ref[rq, :]
            mf0, mf1 = mf[:, 0:1], mf[:, HD_C:HD_C + 1]
            m0 = jnp.maximum(m0, mf0)
            m1 = jnp.maximum(m1, mf1)
        p0 = [jnp.exp2(s - m0) for s in s0]
        p1 = [jnp.exp2(s - m1) for s in s1]
        l0 = sum(jnp.sum(p, axis=-1, keepdims=True) for p in p0)
        l1 = sum(jnp.sum(p, axis=-1, keepdims=True) for p in p1)
        a0 = sum(mm(p.astype(BF16), v) for p, v in zip(p0, vs))
        a1 = sum(mm(p.astype(BF16), v) for p, v in zip(p1, vs))
        acc = jnp.where(head0, a0, a1)
        den = jnp.where(head0, l0, l1)
        if far:
            w_far = jnp.where(head0, jnp.exp2(mf0 - m0), jnp.exp2(mf1 - m1))
            acc = acc + af_ref[rq, :] * w_far
            den = den + lf_ref[rq, :] * w_far
        o_ref[0, rq, :] = acc / den


def _attn_prompt(q3, k3, v3):
    n_seq, length, d = q3.shape
    pairs = d // LANES
    blk = ATT_BLK
    near = ATT_NEAR_BLOCKS
    ncls = BRANCHES[-1][1]
    csz = length // ncls
    per = blk // ncls
    assert (near - 1) * blk >= BRANCHES[-2][0] and length % blk == 0 and blk % ncls == 0
    a = jnp.arange(blk)
    dist = (jnp.arange(near)[:, None, None] * blk) + a[None, :, None] - a[None, None, :]
    nbias = _log_mult(dist) * jnp.where(_multiplicity(dist) > 0, LOG2_E, 1.0)
    m = jnp.arange(csz)
    far_ok = ((m[:, None] // per - m[None, :] // per) >= near) & \
        ((m[:, None] - m[None, :]) * ncls <= BRANCHES[-1][0])
    fbias = jnp.where(far_ok, 0.0, NEG_BIG).astype(F32)
    return pl.pallas_call(
        functools.partial(_attn_prompt_kernel, length=length),
        grid=(n_seq, pairs),
        in_specs=[pl.BlockSpec((1, length, LANES), lambda n, p: (n, 0, p)),
                  pl.BlockSpec((1, length, LANES), lambda n, p: (n, 0, p)),
                  pl.BlockSpec((1, length, LANES), lambda n, p: (n, 0, p)),
                  pl.BlockSpec((near, blk, blk), lambda n, p: (0, 0, 0)),
                  pl.BlockSpec((csz, csz), lambda n, p: (0, 0))],
        out_specs=pl.BlockSpec((1, length, LANES), lambda n, p: (n, 0, p)),
        out_shape=jax.ShapeDtypeStruct((n_seq, length, d), F32),
        scratch_shapes=[pltpu.VMEM((length, LANES), F32)] * 3,
        compiler_params=_cparams(("arbitrary", "arbitrary"), VMEM_LIMIT),
        name="attn_prompt",
    )(q3, k3, v3, nbias, fbias)


def _attn_sample_kernel(q_ref, kn_ref, vn_ref, kt_ref, vt_ref, bias_ref, biasn_ref, o_ref, *, heads):
    q = q_ref[0] * (HD_C ** -0.5)
    kn = kn_ref[0]
    vn = vn_ref[0]
    bias = bias_ref[...]
    biasn = biasn_ref[...]
    outs = []
    for h in range(heads):
        lo, hi = h * HD_C, (h + 1) * HD_C
        qh = q[:, lo:hi].astype(BF16)
        s = jnp.dot(qh, kt_ref[0, h].astype(BF16), preferred_element_type=F32) + bias
        sn = _bdot_nt(qh, kn[:, lo:hi]) + biasn
        m = jnp.maximum(jnp.max(s, axis=-1, keepdims=True), jnp.max(sn, axis=-1, keepdims=True))
        p = jnp.exp(s - m)
        pn = jnp.exp(sn - m)
        den = jnp.sum(p, axis=-1, keepdims=True) + jnp.sum(pn, axis=-1, keepdims=True)
        o = _bdot_nt(p, vt_ref[0, h]) + _bdot(pn, vn[:, lo:hi])
        outs.append(o / den)
    o_ref[0] = jnp.concatenate(outs, axis=-1)


def _attn_sample(q3, k3, v3, k_cache, v_cache):
    n_seq, t_len, d = q3.shape
    w_buf = k_cache.shape[1]
    kt = jnp.transpose(k_cache, (0, 2, 3, 1))
    vt = jnp.transpose(v_cache, (0, 2, 3, 1))
    heads = SAMPLE_HEADS_PER_STEP
    hw = heads * HD_C
    nj = H_C // heads
    tpos = w_buf + jnp.arange(t_len)
    bias = _log_mult(tpos[:, None] - jnp.arange(w_buf)[None, :])
    biasn = _log_mult(jnp.arange(t_len)[:, None] - jnp.arange(t_len)[None, :])
    return pl.pallas_call(
        functools.partial(_attn_sample_kernel, heads=heads),
        grid=(n_seq, nj),
        in_specs=[pl.BlockSpec((1, t_len, hw), lambda n, j: (n, 0, j)),
                  pl.BlockSpec((1, t_len, hw), lambda n, j: (n, 0, j)),
                  pl.BlockSpec((1, t_len, hw), lambda n, j: (n, 0, j)),
                  pl.BlockSpec((1, heads, HD_C, w_buf), lambda n, j: (n, j, 0, 0)),
                  pl.BlockSpec((1, heads, HD_C, w_buf), lambda n, j: (n, j, 0, 0)),
                  pl.BlockSpec((t_len, w_buf), lambda n, j: (0, 0)),
                  pl.BlockSpec((t_len, t_len), lambda n, j: (0, 0))],
        out_specs=pl.BlockSpec((1, t_len, hw), lambda n, j: (n, 0, j)),
        out_shape=jax.ShapeDtypeStruct((n_seq, t_len, d), F32),
        compiler_params=_cparams(("arbitrary", "arbitrary"), VMEM_LIMIT),
        name="attn_sample",
    )(q3, k3, v3, kt, vt, bias, biasn)


def _pad_hist(hist, n_seq, width):
    k1 = hist.shape[1]
    return jnp.concatenate([jnp.zeros((n_seq, SUBLANES - k1, width), F32), hist.astype(F32)], axis=1)


def _prep_weights(p):
    d = p['w_in_even'].shape[1]
    out = {}
    w_in = p['w_in_even'][0]
    hv = H_A * DV_A
    o_z = QKV_A
    o_a = o_z + hv
    o_g = o_a + 2 * H_A
    ab_pad = jnp.zeros((d, LANES - 2 * H_A), F32)
    out['w_in'] = jnp.concatenate([w_in[:, :o_a], w_in[:, o_a:o_g], ab_pad, w_in[:, o_g:]], axis=1).astype(BF16)
    out['w_out_even'] = p['w_out_even'][0].astype(BF16)
    out['w_qkv'] = p['w_qkv_odd'][0].astype(BF16)
    out['w_out_odd'] = p['w_out_odd'][0].astype(BF16)
    depth = p['w_router_group'].shape[0]
    wrt, bcol = [], []
    for l in range(depth):
        pad_w = jnp.zeros((SUBLANES - N_GROUPS, d), F32)
        wrt.append(jnp.concatenate([p['w_router_group'][l].T, pad_w, p['w_router_expert'][l].T], axis=0).astype(BF16))
        pad_b = jnp.full((SUBLANES - N_GROUPS,), NEG_BIG, F32)
        bcol.append(jnp.concatenate([p['b_router_group'][l], pad_b, p['b_router_expert'][l]]).reshape(-1, 1))
    out.update(wrt=wrt, bcol=bcol, wg=p['w_exp_gate'], wu=p['w_exp_up'], wd=p['w_exp_down'])
    return out


def _trunk(x3, mod, a_ssm, a_conv, b_conv, kv_k, kv_v, p, w, sample):
    n_seq, length, d = x3.shape
    t = n_seq * length
    hv = H_A * DV_A
    mod3 = [m.reshape(n_seq, 1, 6 * d) for m in mod]

    qkv_pre, z, ab, gcx = _norm_mod_matmul(x3, mod3[0], 0, 1, p['norm1'][0], w['w_in'],
                                           (QKV_A, hv, LANES, 3 * D_B))
    chunk = min(CHUNK_PROMPT, length)
    g_chunks = 4
    qkv_pre3 = qkv_pre.reshape(n_seq, length, QKV_A)
    ub, wm, qd, kd, qk, dec = _delta_prep(qkv_pre3, _pad_hist(a_conv[0], n_seq, QKV_A),
                                          ab.reshape(n_seq, length, LANES), p['conv_a_w'][0],
                                          p['a_log'][0], p['dt_bias'][0], chunk, g_chunks)
    out_a, s_new = _delta_scan(ub, wm, qd, kd, qk, dec, z.reshape(n_seq, length, hv),
                               a_ssm[0].astype(F32), p['o_gain_a'][0], chunk)
    x3, bx_tail = _even_out(gcx.reshape(n_seq, length, 3 * D_B), _pad_hist(b_conv[0], n_seq, D_B),
                            out_a.reshape(t, hv), x3, mod3[0], 2, p['conv_b_w'][0], w['w_out_even'])
    new_aconv = jnp.concatenate([a_conv[0].astype(F32), qkv_pre3], axis=1)[:, -(CONV_A - 1):]
    new_bconv = jnp.concatenate([b_conv[0].astype(F32), bx_tail], axis=1)[:, -(CONV_B - 1):]
    x3 = _hier_moe_residual(x3, mod3[0], 3, 4, 5, p['norm2'][0], w['wrt'][0], w['bcol'][0],
                            w['wg'], w['wu'], w['wd'], 0, p['norm_f'], False)

    if sample:
        q_new, k_new, v_new = (a.reshape(n_seq, length, d) for a in
                               _norm_mod_matmul(x3, mod3[1], 0, 1, p['norm1'][1], w['w_qkv'], (d, d, d)))
        attn = _attn_sample(q_new, k_new, v_new, kv_k[0], kv_v[0])
        new_k = k_new.reshape(n_seq, length, H_C, HD_C)
        new_v = v_new.reshape(n_seq, length, H_C, HD_C)
    else:
        q_new, k_new, v_new, k_t, v_t = _norm_mod_matmul(x3, mod3[1], 0, 1, p['norm1'][1], w['w_qkv'],
                                                         (d, d, d), transposed=(1, 2))
        q_new, k_new, v_new = (a.reshape(n_seq, length, d) for a in (q_new, k_new, v_new))
        attn = _attn_prompt(q_new, k_new, v_new)
        keep = min(BRANCHES[-1][0], length)
        new_k = jnp.transpose(k_t[:, :, length - keep:].reshape(n_seq, H_C, HD_C, keep), (0, 3, 1, 2))
        new_v = jnp.transpose(v_t[:, :, length - keep:].reshape(n_seq, H_C, HD_C, keep), (0, 3, 1, 2))
    x3 = _proj_residual(attn.reshape(t, d), x3, mod3[1], 2, w['w_out_odd'])
    y = _hier_moe_residual(x3, mod3[1], 3, 4, 5, p['norm2'][1], w['wrt'][1], w['bcol'][1],
                           w['wg'], w['wu'], w['wd'], 1, p['norm_f'], True)
    return y, s_new[None], new_aconv[None], new_bconv[None], new_k[None], new_v[None]


def kernel(x_prompt, x_sample, state_a_ssm, state_a_conv, state_b_conv, cache_c_k, cache_c_v, c_prompt, c_sample, w_ada, b_ada, norm1, norm2, norm_f, w_in_even, conv_a_w, a_log, dt_bias, o_gain_a, conv_b_w, w_out_even, w_qkv_odd, w_out_odd, w_router_group, b_router_group, w_router_expert, b_router_expert, w_exp_gate, w_exp_up, w_exp_down):
    p = dict(norm1=norm1, norm2=norm2, norm_f=norm_f, w_in_even=w_in_even, conv_a_w=conv_a_w,
             a_log=a_log, dt_bias=dt_bias, o_gain_a=o_gain_a, conv_b_w=conv_b_w, w_out_even=w_out_even,
             w_qkv_odd=w_qkv_odd, w_out_odd=w_out_odd, w_router_group=w_router_group,
             b_router_group=b_router_group, w_router_expert=w_router_expert,
             b_router_expert=b_router_expert, w_exp_gate=w_exp_gate, w_exp_up=w_exp_up,
             w_exp_down=w_exp_down)
    w = _prep_weights(p)
    nb = x_prompt.shape[0]
    ns = x_sample.shape[0]
    mod_all = _ada_mod(jnp.concatenate([c_prompt, c_sample], axis=0), w_ada, b_ada)
    depth = w_ada.shape[0]
    mod_p = [mod_all[l, :nb] for l in range(depth)]
    mod_s = [mod_all[l, nb:] for l in range(depth)]
    n_even = state_a_ssm.shape[0]
    zero_ssm = jnp.zeros((n_even, nb, H_A, DK_A, DV_A), F32)
    zero_aconv = jnp.zeros((n_even, nb, CONV_A - 1, QKV_A), F32)
    zero_bconv = jnp.zeros((n_even, nb, CONV_B - 1, D_B), F32)
    y_p, ssm_p, aconv_p, bconv_p, k_p, v_p = _trunk(x_prompt, mod_p, zero_ssm, zero_aconv, zero_bconv,
                                                    None, None, p, w, False)
    y_s, ssm_s, aconv_s, bconv_s, k_s, v_s = _trunk(x_sample, mod_s, state_a_ssm, state_a_conv,
                                                    state_b_conv, cache_c_k, cache_c_v, p, w, True)
    return (y_p, y_s, ssm_p, ssm_s, aconv_p, aconv_s, bconv_p, bconv_s, k_p, k_s, v_p, v_s)
```

```python
import functools
import math

import jax
import jax.numpy as jnp
from jax import lax
from jax.experimental import pallas as pl
from jax.experimental.pallas import tpu as pltpu

F32 = jnp.float32
BF16 = jnp.bfloat16
I32 = jnp.int32

EPS = 1e-6
NEG_BIG = -1e30

LANES = 128
SUBLANES = 8
VMEM_BYTES_V7X = 64 * 1024 * 1024
VMEM_LIMIT = 56 * 1024 * 1024

H_A, DK_A, DV_A = 4, 128, 128
CONV_A, CONV_B = 4, 3
QKV_A = H_A * (2 * DK_A + DV_A)
D_B = 512
H_C, HD_C = 16, 64
BRANCHES = ((128, 1), (512, 4), (2048, 16))
N_GROUPS, E_PER_GROUP = 4, 8
N_EXPERTS = N_GROUPS * E_PER_GROUP
CHUNK_PROMPT = 64

TOK_TILE = 512
MOE_TOK_TILE = 512
MOE_ROW_TILE_BIG = 512
MOE_ROW_TILE_SMALL = 128
ATT_BLK = 256
ATT_NEAR_BLOCKS = 3
LOG2_E = math.log2(math.e)
SEQ_BLOCK_SCAN = 8
SAMPLE_HEADS_PER_STEP = 16


def _cparams(sem, vmem=None):
    return pltpu.CompilerParams(dimension_semantics=sem, vmem_limit_bytes=vmem)


def _silu(x):
    return x * jax.nn.sigmoid(x)


def _bdot(a, b):
    return jnp.dot(a.astype(BF16), b.astype(BF16), preferred_element_type=F32)


def _bdot_nt(a, b):
    return lax.dot_general(a.astype(BF16), b.astype(BF16), (((1,), (1,)), ((), ())),
                           preferred_element_type=F32)


def _bdot_tn(a, b):
    return lax.dot_general(a.astype(BF16), b.astype(BF16), (((0,), (0,)), ((), ())),
                           preferred_element_type=F32)


def _split_bf16(x):
    hi = x.astype(BF16)
    return hi, (x - hi.astype(F32)).astype(BF16)


def _dot3(a, b):
    ah, al = _split_bf16(a)
    bh, bl = _split_bf16(b)
    dot = functools.partial(jnp.dot, preferred_element_type=F32)
    return dot(ah, bh) + (dot(ah, bl) + dot(al, bh))


def _fdot(a, b):
    return jnp.dot(a, b, preferred_element_type=F32, precision=lax.Precision.HIGHEST)


def _fdot_nt(a, b):
    return lax.dot_general(a, b, (((1,), (1,)), ((), ())), preferred_element_type=F32,
                           precision=lax.Precision.HIGHEST)


def _tok_tiling(n_seq, length, tile):
    if length >= tile:
        sb, lb = 1, tile
    else:
        sb, lb = tile // length, length
    nl = length // lb
    return sb, lb, (n_seq // sb) * nl, nl


def _norm_mod(x, gain, sc, sh):
    sb, lb, d = x.shape
    y = x * lax.rsqrt(jnp.mean(x * x, axis=-1, keepdims=True) + EPS) * gain
    h = y * (1.0 + sc) + sh
    return h.reshape(sb * lb, d)


def _ada_kernel(c_ref, w_ref, b_ref, o_ref):
    c = _silu(c_ref[...])
    o_ref[0] = _bdot(c, w_ref[0]) + b_ref[0]


def _ada_mod(c_all, w_ada, b_ada):
    r, d = c_all.shape
    depth, _, n6 = w_ada.shape
    tn = 1536
    return pl.pallas_call(
        _ada_kernel,
        grid=(depth, n6 // tn),
        in_specs=[pl.BlockSpec((r, d), lambda l, j: (0, 0)),
                  pl.BlockSpec((1, d, tn), lambda l, j: (l, 0, j)),
                  pl.BlockSpec((1, 1, tn), lambda l, j: (l, 0, j))],
        out_specs=pl.BlockSpec((1, r, tn), lambda l, j: (l, 0, j)),
        out_shape=jax.ShapeDtypeStruct((depth, r, n6), F32),
        compiler_params=_cparams(("arbitrary", "arbitrary"), VMEM_LIMIT),
        name="ada_mod",
    )(c_all, w_ada, b_ada.reshape(depth, 1, n6))


def _nmm_kernel(x_ref, sh_ref, sc_ref, gain_ref, w_ref, *o_refs, splits, transposed):
    h = _norm_mod(x_ref[...], gain_ref[...], sc_ref[...], sh_ref[...]).astype(BF16)
    off = 0
    vals = []
    for o_ref, n in zip(o_refs, splits):
        vals.append(jnp.dot(h, w_ref[:, off:off + n], preferred_element_type=F32))
        o_ref[...] = vals[-1]
        off += n
    for o_ref, j in zip(o_refs[len(splits):], transposed):
        o_ref[0] = vals[j].T


def _norm_mod_matmul(x3, mod3, k_sh, k_sc, gain, w_bf16, splits, transposed=()):
    n_seq, length, d = x3.shape
    sb, lb, steps, nl = _tok_tiling(n_seq, length, TOK_TILE)
    rows = sb * lb
    t = n_seq * length
    n_out = w_bf16.shape[1]
    assert sum(splits) == n_out and (not transposed or sb == 1)
    t_specs = [pl.BlockSpec((1, splits[j], lb), lambda i: (i // nl, 0, i % nl)) for j in transposed]
    t_shapes = [jax.ShapeDtypeStruct((n_seq, splits[j], length), F32) for j in transposed]
    return pl.pallas_call(
        functools.partial(_nmm_kernel, splits=splits, transposed=transposed),
        grid=(steps,),
        in_specs=[pl.BlockSpec((sb, lb, d), lambda i: (i // nl, i % nl, 0)),
                  pl.BlockSpec((sb, 1, d), lambda i: (i // nl, 0, k_sh)),
                  pl.BlockSpec((sb, 1, d), lambda i: (i // nl, 0, k_sc)),
                  pl.BlockSpec((1, d), lambda i: (0, 0)),
                  pl.BlockSpec((d, n_out), lambda i: (0, 0))],
        out_specs=[pl.BlockSpec((rows, n), lambda i: (i, 0)) for n in splits] + t_specs,
        out_shape=[jax.ShapeDtypeStruct((t, n), F32) for n in splits] + t_shapes,
        compiler_params=_cparams(("arbitrary",), VMEM_LIMIT),
        name="norm_mod_matmul",
    )(x3, mod3, mod3, gain.reshape(1, d), w_bf16)


def _delta_prep_kernel(qkv_ref, halo_ref, hist_ref, ab_ref, cw_ref, al_ref, dtb_ref, lt_ref, sel_ref,
                       ub_ref, wm_ref, qd_ref, kd_ref, qk_ref, dec_ref,
                       *, g_chunks, chunk, chunks_are_seqs, chunks_per_seq):
    c = chunk
    step = pl.program_id(0)
    cw = cw_ref[...]
    row = lax.broadcasted_iota(I32, (c, c), 0)
    col = lax.broadcasted_iota(I32, (c, c), 1)
    lower_incl = row >= col
    strict = row > col
    eye = (row == col).astype(F32)
    n_fac = max(int(math.ceil(math.log2(c))), 1)

    qkvs, abs_ = [], []
    for g in range(g_chunks):
        cur = qkv_ref[g] if chunks_are_seqs else qkv_ref[0, g * c:(g + 1) * c, :]
        if chunks_are_seqs:
            prev = hist_ref[g]
        elif g == 0:
            first = (step % (chunks_per_seq // g_chunks)) == 0
            prev = jnp.where(first, hist_ref[0], halo_ref[0])
        else:
            prev = qkv_ref[0, g * c - SUBLANES:g * c, :]
        row8 = lax.broadcasted_iota(I32, (SUBLANES, cur.shape[1]), 0)
        conv = cur * cw[CONV_A - 1:CONV_A, :]
        for dback in range(1, CONV_A):
            rolled = pltpu.roll(cur, dback, 0)
            top = jnp.where(row8 < dback, pltpu.roll(prev, dback, 0), rolled[0:SUBLANES])
            shifted = top if c == SUBLANES else jnp.concatenate([top, rolled[SUBLANES:]], axis=0)
            conv = conv + shifted * cw[CONV_A - 1 - dback:CONV_A - dback, :]
        qkvs.append(_silu(conv))
        abs_.append(ab_ref[g] if chunks_are_seqs else ab_ref[0, g * c:(g + 1) * c, :])

    g_alls = [-jnp.exp(al_ref[...]) * jax.nn.softplus(ab + dtb_ref[...]) for ab in abs_]
    sigs = [jax.nn.sigmoid(ab) for ab in abs_]
    gcum_alls = [_fdot(lt_ref[...], ga) for ga in g_alls]
    gcum_rows = [_fdot_nt(sel_ref[...], gc) for gc in gcum_alls]

    chains = [(g, h) for g in range(g_chunks) for h in range(H_A)]
    qs, ks, vs, betas, gcs, gammas, egs = [], [], [], [], [], [], []
    for g, h in chains:
        qkv = qkvs[g]
        q = qkv[:, h * DK_A:(h + 1) * DK_A]
        k = qkv[:, H_A * DK_A + h * DK_A:H_A * DK_A + (h + 1) * DK_A]
        v = qkv[:, 2 * H_A * DK_A + h * DV_A:2 * H_A * DK_A + (h + 1) * DV_A]
        qs.append(q * lax.rsqrt(jnp.sum(q * q, axis=-1, keepdims=True) + EPS) * (DK_A ** -0.5))
        ks.append(k * lax.rsqrt(jnp.sum(k * k, axis=-1, keepdims=True) + EPS))
        vs.append(v)
        betas.append(sigs[g][:, H_A + h:H_A + h + 1])
        gc = gcum_alls[g][:, h:h + 1]
        gr = gcum_rows[g][h:h + 1, :]
        gcs.append(gc)
        gammas.append(jnp.exp(jnp.where(lower_incl, gc - gr, NEG_BIG)))
        egs.append(jnp.exp(gc))

    kq = [_bdot_nt(jnp.concatenate([k, q], axis=0), k) for k, q in zip(ks, qs)]
    a_mats = [jnp.where(strict, b * x[0:c] * gm, 0.0) for b, x, gm in zip(betas, kq, gammas)]
    qkms = [x[c:2 * c] * gm for x, gm in zip(kq, gammas)]
    m_pows = [-a for a in a_mats]
    t_invs = [eye + m for m in m_pows]
    for _ in range(n_fac - 1):
        m_pows = [_bdot(m, m) for m in m_pows]
        t_invs = [t + _bdot(t, m) for t, m in zip(t_invs, m_pows)]
    resids = [eye - t - _dot3(a, t) for a, t in zip(a_mats, t_invs)]
    t_invs = [t + _bdot(t, r) for t, r in zip(t_invs, resids)]
    rhss = [jnp.concatenate([v * b, k * (b * eg)], axis=-1) for v, k, b, eg in zip(vs, ks, betas, egs)]
    sols = [rhs + _dot3(t - eye, rhs) for t, rhs in zip(t_invs, rhss)]

    for g in range(g_chunks):
        idx = [i for i, (gg, _) in enumerate(chains) if gg == g]
        g_last = [gcum_alls[g][c - 1:c, h:h + 1] for h in range(H_A)]

        def put(ref, parts):
            val = jnp.concatenate(parts, axis=-1)
            if chunks_are_seqs:
                ref[g] = val
            else:
                ref[0, g * c:(g + 1) * c, :] = val

        put(ub_ref, [sols[i][:, :DV_A] for i in idx])
        put(wm_ref, [sols[i][:, DV_A:] for i in idx])
        put(qd_ref, [qs[i] * egs[i] for i in idx])
        put(kd_ref, [ks[i] * jnp.exp(g_last[h] - gcs[i]) for h, i in enumerate(idx)])
        put(qk_ref, [qkms[i] for i in idx])
        dec_ref[g] = jnp.concatenate([jnp.broadcast_to(jnp.exp(gl), (SUBLANES, DV_A)) for gl in g_last], axis=-1)


def _delta_prep(qkv_pre3, hist8, ab3, conv_w, a_log, dt_bias, chunk, g_chunks):
    n_seq, length, w = qkv_pre3.shape
    c = chunk
    chunks_are_seqs = (length == c)
    nc = length // c
    if chunks_are_seqs:
        steps = n_seq // g_chunks
        blk = (g_chunks, c, w)
        x_map = lambda i: (i, 0, 0)
        halo_map = lambda i: (i, 0, 0)
        hist_spec = pl.BlockSpec((g_chunks, SUBLANES, w), lambda i: (i, 0, 0))
        ab_spec = pl.BlockSpec((g_chunks, c, LANES), lambda i: (i, 0, 0))
        out_map = lambda i: (i, 0, 0)
        out_rows = (g_chunks, c)
    else:
        rows = g_chunks * c
        spb = nc // g_chunks
        steps = n_seq * spb
        blk = (1, rows, w)
        x_map = lambda i: (i // spb, i % spb, 0)
        halo_map = lambda i: (i // spb, jnp.maximum((i % spb) * (rows // SUBLANES) - 1, 0), 0)
        hist_spec = pl.BlockSpec((1, SUBLANES, w), lambda i: (i // spb, 0, 0))
        ab_spec = pl.BlockSpec((1, rows, LANES), lambda i: (i // spb, i % spb, 0))
        out_map = x_map
        out_rows = (1, rows)
    lt = jnp.tril(jnp.ones((c, c), F32))
    sel = jnp.eye(SUBLANES, LANES, dtype=F32)
    hv = H_A * DV_A
    al = jnp.zeros((1, LANES), F32).at[0, :H_A].set(a_log)
    dtb = jnp.zeros((1, LANES), F32).at[0, :H_A].set(dt_bias)
    kern = functools.partial(_delta_prep_kernel, g_chunks=g_chunks, chunk=c,
                             chunks_are_seqs=chunks_are_seqs, chunks_per_seq=nc)
    big = lambda width: pl.BlockSpec(out_rows + (width,), out_map)
    return pl.pallas_call(
        kern,
        grid=(steps,),
        in_specs=[pl.BlockSpec(blk, x_map),
                  pl.BlockSpec((1, SUBLANES, w), halo_map) if not chunks_are_seqs
                  else pl.BlockSpec((g_chunks, SUBLANES, w), halo_map),
                  hist_spec, ab_spec,
                  pl.BlockSpec((CONV_A, w), lambda i: (0, 0)),
                  pl.BlockSpec((1, LANES), lambda i: (0, 0)),
                  pl.BlockSpec((1, LANES), lambda i: (0, 0)),
                  pl.BlockSpec((c, c), lambda i: (0, 0)),
                  pl.BlockSpec((SUBLANES, LANES), lambda i: (0, 0))],
        out_specs=[big(hv), big(hv), big(hv), big(hv), big(H_A * c),
                   pl.BlockSpec((g_chunks, SUBLANES, hv), lambda i: (i, 0, 0))],
        out_shape=[jax.ShapeDtypeStruct((n_seq, length, hv), F32)] * 4
        + [jax.ShapeDtypeStruct((n_seq, length, H_A * c), F32),
           jax.ShapeDtypeStruct((n_seq * nc, SUBLANES, hv), F32)],
        compiler_params=_cparams(("arbitrary",), VMEM_LIMIT),
        name="delta_prep",
    )(qkv_pre3, hist8 if chunks_are_seqs else qkv_pre3, hist8, ab3, conv_w, al, dtb, lt, sel)


def _delta_scan_kernel(ub_ref, wm_ref, qd_ref, kd_ref, qk_ref, dec_ref, z_ref, s0_ref, og_ref,
                       o_ref, sn_ref, s_ref, *, nb, chunk):
    c = chunk
    j = pl.program_id(1)

    @pl.when(j == 0)
    def _():
        s_ref[...] = s0_ref[...]

    og = og_ref[...]
    chains = [(b, h) for b in range(nb) for h in range(H_A)]
    sl = lambda h: slice(h * DV_A, (h + 1) * DV_A)
    states = [s_ref[b, h] for b, h in chains]
    ws = [_bdot(jnp.concatenate([wm_ref[b, :, sl(h)], qd_ref[b, :, sl(h)]], axis=0), s)
          for (b, h), s in zip(chains, states)]
    us = [ub_ref[b, :, sl(h)] - x[0:c] for (b, h), x in zip(chains, ws)]
    os_ = [x[c:2 * c] + _bdot(qk_ref[b, :, h * c:(h + 1) * c], u) for (b, h), x, u in zip(chains, ws, us)]
    for (b, h), s, u in zip(chains, states, us):
        s_ref[b, h] = s * dec_ref[b, 0, 0:1, sl(h)] + _bdot_tn(kd_ref[b, :, sl(h)], u)
    for b in range(nb):
        outs = []
        for h in range(H_A):
            o = os_[b * H_A + h]
            on = o * lax.rsqrt(jnp.mean(o * o, axis=-1, keepdims=True) + EPS) * og
            outs.append(on * _silu(z_ref[b, :, sl(h)]))
        o_ref[b] = jnp.concatenate(outs, axis=-1)
    sn_ref[...] = s_ref[...]


def _delta_scan(ub, wm, qd, kd, qk, dec, z3, s0, o_gain, chunk):
    n_seq, length, hv = ub.shape
    c = chunk
    nc = length // c
    nb = SEQ_BLOCK_SCAN
    dec4 = dec.reshape(n_seq, nc, SUBLANES, hv)
    tok = lambda width: pl.BlockSpec((nb, c, width), lambda b, j: (b, j, 0))
    st = pl.BlockSpec((nb, H_A, DK_A, DV_A), lambda b, j: (b, 0, 0, 0))
    return pl.pallas_call(
        functools.partial(_delta_scan_kernel, nb=nb, chunk=c),
        grid=(n_seq // nb, nc),
        in_specs=[tok(hv), tok(hv), tok(hv), tok(hv), tok(H_A * c),
                  pl.BlockSpec((nb, 1, SUBLANES, hv), lambda b, j: (b, j, 0, 0)),
                  tok(hv), st, pl.BlockSpec((1, DV_A), lambda b, j: (0, 0))],
        out_specs=[tok(hv), st],
        out_shape=[jax.ShapeDtypeStruct((n_seq, length, hv), F32),
                   jax.ShapeDtypeStruct((n_seq, H_A, DK_A, DV_A), F32)],
        scratch_shapes=[pltpu.VMEM((nb, H_A, DK_A, DV_A), F32)],
        compiler_params=_cparams(("arbitrary", "arbitrary"), VMEM_LIMIT),
        name="delta_scan",
    )(ub, wm, qd, kd, qk, dec4, z3, s0, o_gain.reshape(1, DV_A))


def _even_out_kernel(gcx_ref, halo_ref, hist_ref, oa_ref, x_ref, g1_ref, cw_ref, w_ref,
                     xo_ref, tail_ref, ext_ref, *, sb, lb, nl):
    step = pl.program_id(0)
    gcx = gcx_ref[...]
    b_gate = gcx[:, :, 0:D_B]
    bx = gcx[:, :, D_B:2 * D_B] * gcx[:, :, 2 * D_B:3 * D_B]
    if nl == 1:
        prev = hist_ref[...]
    else:
        hb = halo_ref[...]
        first = (step % nl) == 0
        prev = jnp.where(first, hist_ref[...], hb[:, :, D_B:2 * D_B] * hb[:, :, 2 * D_B:3 * D_B])
    ext_ref[:, 0:SUBLANES, :] = prev
    ext_ref[:, SUBLANES:SUBLANES + lb, :] = bx
    cw = cw_ref[...]
    conv = ext_ref[:, pl.ds(SUBLANES - (CONV_B - 1), lb), :] * cw[0:1, :]
    for j in range(1, CONV_B):
        conv = conv + ext_ref[:, pl.ds(SUBLANES - (CONV_B - 1) + j, lb), :] * cw[j:j + 1, :]
    out_b = (b_gate * conv).reshape(sb * lb, D_B)
    hv = H_A * DV_A
    y = _bdot(oa_ref[...], w_ref[0:hv, :]) + _bdot(out_b, w_ref[hv:hv + D_B, :])
    d = y.shape[-1]
    xo_ref[...] = x_ref[...] + g1_ref[...] * y.reshape(sb, lb, d)
    tail_ref[...] = bx[:, lb - SUBLANES:lb, :]


def _even_out(gcx3, hist8, out_a2, x3, mod3, k_gate, conv_w, w_out_bf16):
    n_seq, length, d = x3.shape
    sb, lb, steps, nl = _tok_tiling(n_seq, length, TOK_TILE)
    rows = sb * lb
    w3 = gcx3.shape[-1]
    hv = H_A * DV_A
    halo_map = lambda i: (i // nl, jnp.maximum((i % nl) * (lb // SUBLANES) - 1, 0), 0)
    return pl.pallas_call(
        functools.partial(_even_out_kernel, sb=sb, lb=lb, nl=nl),
        grid=(steps,),
        in_specs=[pl.BlockSpec((sb, lb, w3), lambda i: (i // nl, i % nl, 0)),
                  pl.BlockSpec((sb, SUBLANES, w3), halo_map if nl > 1 else (lambda i: (i, 0, 0))),
                  pl.BlockSpec((sb, SUBLANES, D_B), lambda i: (i // nl, 0, 0)),
                  pl.BlockSpec((rows, hv), lambda i: (i, 0)),
                  pl.BlockSpec((sb, lb, d), lambda i: (i // nl, i % nl, 0)),
                  pl.BlockSpec((sb, 1, d), lambda i: (i // nl, 0, k_gate)),
                  pl.BlockSpec((CONV_B, D_B), lambda i: (0, 0)),
                  pl.BlockSpec((hv + D_B, d), lambda i: (0, 0))],
        out_specs=[pl.BlockSpec((sb, lb, d), lambda i: (i // nl, i % nl, 0)),
                   pl.BlockSpec((sb, SUBLANES, D_B), lambda i: (i // nl, 0, 0))],
        out_shape=[jax.ShapeDtypeStruct((n_seq, length, d), F32),
                   jax.ShapeDtypeStruct((n_seq, SUBLANES, D_B), F32)],
        scratch_shapes=[pltpu.VMEM((sb, SUBLANES + lb, D_B), F32)],
        compiler_params=_cparams(("arbitrary",), VMEM_LIMIT),
        name="even_out",
    )(gcx3, gcx3, hist8, out_a2, x3, mod3, conv_w, w_out_bf16)


def _proj_res_kernel(a_ref, x_ref, g_ref, w_ref, xo_ref, *, sb, lb):
    y = _bdot(a_ref[...], w_ref[...])
    xo_ref[...] = x_ref[...] + g_ref[...] * y.reshape(sb, lb, y.shape[-1])


def _proj_residual(a2, x3, mod3, k_gate, w_bf16):
    n_seq, length, d = x3.shape
    sb, lb, steps, nl = _tok_tiling(n_seq, length, TOK_TILE)
    rows = sb * lb
    ka = a2.shape[-1]
    return pl.pallas_call(
        functools.partial(_proj_res_kernel, sb=sb, lb=lb),
        grid=(steps,),
        in_specs=[pl.BlockSpec((rows, ka), lambda i: (i, 0)),
                  pl.BlockSpec((sb, lb, d), lambda i: (i // nl, i % nl, 0)),
                  pl.BlockSpec((sb, 1, d), lambda i: (i // nl, 0, k_gate)),
                  pl.BlockSpec((ka, d), lambda i: (0, 0))],
        out_specs=pl.BlockSpec((sb, lb, d), lambda i: (i // nl, i % nl, 0)),
        out_shape=jax.ShapeDtypeStruct((n_seq, length, d), F32),
        compiler_params=_cparams(("arbitrary",), VMEM_LIMIT),
        name="proj_residual",
    )(a2, x3, mod3, w_bf16)


def _router_kernel(x_ref, sh_ref, sc_ref, gain_ref, wrt_ref, bcol_ref, utri_ref,
                   mi_ref, gcol_ref, cnt_ref, *rest):
    carry_ref = rest[-1]
    i = pl.program_id(0)

    @pl.when(i == 0)
    def _():
        carry_ref[...] = jnp.zeros_like(carry_ref)

    if len(rest) == 2:
        rest[0][...] = jnp.zeros_like(rest[0])

    h2 = _norm_mod(x_ref[...], gain_ref[...], sc_ref[...], sh_ref[...])
    tm = h2.shape[0]
    logits = _bdot_nt(wrt_ref[...], h2) + bcol_ref[...]
    lg = logits[0:SUBLANES]
    e = jnp.exp(lg - jnp.max(lg, axis=0, keepdims=True))
    pg = e / jnp.sum(e, axis=0, keepdims=True)
    p_top = jnp.max(pg, axis=0, keepdims=True)
    rid8 = lax.broadcasted_iota(I32, (SUBLANES, tm), 0)
    g_top = jnp.min(jnp.where(pg == p_top, rid8, SUBLANES), axis=0, keepdims=True)
    le = logits[SUBLANES:SUBLANES + N_EXPERTS]
    sel = jnp.zeros((E_PER_GROUP, tm), F32)
    for gi in range(N_GROUPS):
        sel = sel + jnp.where(g_top == gi, le[gi * E_PER_GROUP:(gi + 1) * E_PER_GROUP], 0.0)
    e2 = jnp.exp(sel - jnp.max(sel, axis=0, keepdims=True))
    p_in = e2 / jnp.sum(e2, axis=0, keepdims=True)
    w_a = jnp.max(p_in, axis=0, keepdims=True)
    i_a = jnp.min(jnp.where(p_in == w_a, rid8, SUBLANES), axis=0, keepdims=True)
    p_rest = jnp.where(rid8 == i_a, -1.0, p_in)
    w_b = jnp.max(p_rest, axis=0, keepdims=True)
    i_b = jnp.min(jnp.where(p_rest == w_b, rid8, SUBLANES), axis=0, keepdims=True)
    den = w_a + w_b
    gate1 = p_top * (w_a / den)
    gate2 = p_top * (w_b / den)
    ex1 = g_top * E_PER_GROUP + i_a
    ex2 = g_top * E_PER_GROUP + i_b

    rid32 = lax.broadcasted_iota(I32, (N_EXPERTS, tm), 0)
    oh1 = rid32 == ex1
    oh2 = rid32 == ex2
    ohc = jnp.where(oh1, 1.0, jnp.where(oh2, 1.0, 0.0))
    cum = _bdot(ohc, utri_ref[...])
    carry = carry_ref[...]
    base = cum - ohc + carry[:, 0:1]
    r1 = jnp.sum(jnp.where(oh1, base, 0.0), axis=0, keepdims=True)
    r2 = jnp.sum(jnp.where(oh2, base, 0.0), axis=0, keepdims=True)
    new_carry = carry + cum[:, tm - 1:tm]
    carry_ref[...] = new_carry
    cnt_ref[...] = new_carry

    bro = lambda v: jnp.broadcast_to(v, (SUBLANES, tm))
    mi_ref[...] = jnp.where(rid8 == 0, bro(ex1),
                            jnp.where(rid8 == 1, bro(ex2),
                                      jnp.where(rid8 == 2, bro(r1.astype(I32)),
                                                jnp.where(rid8 == 3, bro(r2.astype(I32)), 0))))
    rid128 = lax.broadcasted_iota(I32, (LANES, tm), 0)
    g128 = jnp.where(rid128 == 0, jnp.broadcast_to(gate1, (LANES, tm)),
                     jnp.where(rid128 == 1, jnp.broadcast_to(gate2, (LANES, tm)), 0.0))
    gcol_ref[...] = g128.T


def _moe_router(x3, mod3, k_sh, k_sc, gain, wrt_bf16, bcol, xs_rows):
    n_seq, length, d = x3.shape
    sb, lb, steps, nl = _tok_tiling(n_seq, length, TOK_TILE)
    tm = sb * lb
    t = n_seq * length
    utri = jnp.triu(jnp.ones((tm, tm), F32)).astype(BF16)
    r_rows = wrt_bf16.shape[0]
    assert xs_rows % (steps * SUBLANES) == 0
    zrows = xs_rows // steps
    z_specs = [pl.BlockSpec((zrows, LANES), lambda i: (i, 0))] if xs_rows else []
    z_shapes = [jax.ShapeDtypeStruct((xs_rows, LANES), F32)] if xs_rows else []
    return pl.pallas_call(
        _router_kernel,
        grid=(steps,),
        in_specs=[pl.BlockSpec((sb, lb, d), lambda i: (i // nl, i % nl, 0)),
                  pl.BlockSpec((sb, 1, d), lambda i: (i // nl, 0, k_sh)),
                  pl.BlockSpec((sb, 1, d), lambda i: (i // nl, 0, k_sc)),
                  pl.BlockSpec((1, d), lambda i: (0, 0)),
                  pl.BlockSpec((r_rows, d), lambda i: (0, 0)),
                  pl.BlockSpec((r_rows, 1), lambda i: (0, 0)),
                  pl.BlockSpec((tm, tm), lambda i: (0, 0))],
        out_specs=[pl.BlockSpec((SUBLANES, tm), lambda i: (0, i)),
                   pl.BlockSpec((tm, LANES), lambda i: (i, 0)),
                   pl.BlockSpec((N_EXPERTS, LANES), lambda i: (0, 0))] + z_specs,
        out_shape=[jax.ShapeDtypeStruct((SUBLANES, t), I32),
                   jax.ShapeDtypeStruct((t, LANES), F32),
                   jax.ShapeDtypeStruct((N_EXPERTS, LANES), F32)] + z_shapes,
        scratch_shapes=[pltpu.VMEM((N_EXPERTS, LANES), F32)],
        compiler_params=_cparams(("arbitrary",), VMEM_LIMIT),
        name="moe_router",
    )(x3, mod3, mod3, gain.reshape(1, d), wrt_bf16, bcol, utri)


def _plan_kernel(mi_ref, cnt_ref, cnt_other_ref, ltri_ref, pos_ref, tinfo_ref, *, n_tiles_pad, row_tile, second):
    cnt = cnt_ref[...] + cnt_other_ref[...]
    nt = jnp.floor((cnt + (row_tile - 1)) * (1.0 / row_tile))
    tstart = _bdot(ltri_ref[...], nt)
    mi = mi_ref[...]
    tm = mi.shape[1]
    rid32 = lax.broadcasted_iota(I32, (N_EXPERTS, tm), 0)
    ts_col = tstart[:, 0:1]
    base_col = cnt_other_ref[:, 0:1] if second else jnp.zeros((N_EXPERTS, 1), F32)

    def pos_of(ex, rank):
        hit = rid32 == ex
        start = jnp.sum(jnp.where(hit, ts_col, 0.0), axis=0, keepdims=True)
        base = jnp.sum(jnp.where(hit, base_col, 0.0), axis=0, keepdims=True)
        return start.astype(I32) * row_tile + base.astype(I32) + rank

    p1 = pos_of(mi[0:1], mi[2:3])
    p2 = pos_of(mi[1:2], mi[3:4])
    rid8 = lax.broadcasted_iota(I32, (SUBLANES, tm), 0)
    pos_ref[0] = jnp.where(rid8 == 0, jnp.broadcast_to(p1, (SUBLANES, tm)),
                           jnp.where(rid8 == 1, jnp.broadcast_to(p2, (SUBLANES, tm)), 0))
    tend_col = ts_col + nt[:, 0:1]
    jt = lax.broadcasted_iota(I32, (N_EXPERTS, n_tiles_pad), 1).astype(F32)
    te = jnp.sum(jnp.where(tend_col <= jt, 1.0, 0.0), axis=0, keepdims=True)
    te = jnp.minimum(te, N_EXPERTS - 1.0).astype(I32)
    total = jnp.sum(nt[:, 0:1], axis=0, keepdims=True).astype(I32)
    rid8t = lax.broadcasted_iota(I32, (SUBLANES, n_tiles_pad), 0)
    tinfo_ref[...] = jnp.where(rid8t == 0, jnp.broadcast_to(te, (SUBLANES, n_tiles_pad)),
                               jnp.broadcast_to(total, (SUBLANES, n_tiles_pad)))


def _moe_plan(meta_i, counts, counts_other, n_tiles_pad, row_tile, second):
    t = meta_i.shape[1]
    tm = MOE_TOK_TILE
    steps = t // tm
    ltri = jnp.tril(jnp.ones((N_EXPERTS, N_EXPERTS), F32), k=-1).astype(BF16)
    return pl.pallas_call(
        functools.partial(_plan_kernel, n_tiles_pad=n_tiles_pad, row_tile=row_tile, second=second),
        grid=(steps,),
        in_specs=[pl.BlockSpec((SUBLANES, tm), lambda i: (0, i)),
                  pl.BlockSpec((N_EXPERTS, LANES), lambda i: (0, 0)),
                  pl.BlockSpec((N_EXPERTS, LANES), lambda i: (0, 0)),
                  pl.BlockSpec((N_EXPERTS, N_EXPERTS), lambda i: (0, 0))],
        out_specs=[pl.BlockSpec((1, SUBLANES, tm), lambda i: (i, 0, 0)),
                   pl.BlockSpec((SUBLANES, n_tiles_pad), lambda i: (0, 0))],
        out_shape=[jax.ShapeDtypeStruct((steps, SUBLANES, tm), I32),
                   jax.ShapeDtypeStruct((SUBLANES, n_tiles_pad), I32)],
        compiler_params=_cparams(("arbitrary",)),
        name="moe_plan",
    )(meta_i, counts, counts_other, ltri)


def _to_token_tiles(ref, val):
    tm, d = val.shape
    for s in range(d // LANES):
        ref[pl.ds(s, tm, stride=SUBLANES), :] = val[:, s * LANES:(s + 1) * LANES]


def _from_token_tiles(ref, tm):
    return jnp.concatenate([ref[pl.ds(s, tm, stride=SUBLANES), :] for s in range(SUBLANES)], axis=-1)


def _tile_rows(idx):
    return pl.ds(pl.multiple_of(idx * SUBLANES, SUBLANES), SUBLANES)


def _row_copy_out(buf_ref, slot, r, dst_hbm, p, sem):
    return pltpu.make_async_copy(buf_ref.at[slot, _tile_rows(r), :], dst_hbm.at[_tile_rows(p), :], sem.at[slot])


def _dispatch_kernel(x_ref, sh_ref, sc_ref, gain_ref, pos_ref, xs_in_ref, xs_ref,
                     buf_ref, idx_ref, sem_ref, isem_ref, *, tm, steps):
    del xs_in_ref
    i = pl.program_id(0)
    slot = i % 2

    def drain(s):
        for _ in range(2):
            pltpu.make_async_copy(buf_ref.at[s], xs_ref.at[pl.ds(0, tm * SUBLANES), :], sem_ref.at[s]).wait()

    @pl.when(i >= 2)
    def _():
        drain(slot)

    _to_token_tiles(buf_ref.at[slot], _norm_mod(x_ref[...], gain_ref[...], sc_ref[...], sh_ref[...]))
    cp = pltpu.make_async_copy(pos_ref.at[0], idx_ref, isem_ref)
    cp.start()
    cp.wait()

    def issue(r, carry):
        _row_copy_out(buf_ref, slot, r, xs_ref, idx_ref[0, r], sem_ref).start()
        _row_copy_out(buf_ref, slot, r, xs_ref, idx_ref[1, r], sem_ref).start()
        return carry
    lax.fori_loop(0, tm, issue, 0, unroll=8)

    @pl.when(i == steps - 1)
    def _():
        drain(slot)
        if steps > 1:
            drain(1 - slot)


def _moe_dispatch(x3, mod3, k_sh, k_sc, gain, pos3, xs0):
    n_seq, length, d = x3.shape
    assert d == SUBLANES * LANES
    sb, lb, steps, nl = _tok_tiling(n_seq, length, MOE_TOK_TILE)
    tm = sb * lb
    return pl.pallas_call(
        functools.partial(_dispatch_kernel, tm=tm, steps=steps),
        grid=(steps,),
        in_specs=[pl.BlockSpec((sb, lb, d), lambda i: (i // nl, i % nl, 0)),
                  pl.BlockSpec((sb, 1, d), lambda i: (i // nl, 0, k_sh)),
                  pl.BlockSpec((sb, 1, d), lambda i: (i // nl, 0, k_sc)),
                  pl.BlockSpec((1, d), lambda i: (0, 0)),
                  pl.BlockSpec((1, SUBLANES, tm), lambda i: (i, 0, 0)),
                  pl.BlockSpec(memory_space=pl.ANY)],
        out_specs=pl.BlockSpec(memory_space=pl.ANY),
        out_shape=jax.ShapeDtypeStruct(xs0.shape, F32),
        scratch_shapes=[pltpu.VMEM((2, tm * SUBLANES, LANES), F32),
                        pltpu.SMEM((SUBLANES, tm), I32),
                        pltpu.SemaphoreType.DMA((2,)),
                        pltpu.SemaphoreType.DMA(())],
        input_output_aliases={5: 0},
        compiler_params=_cparams(("arbitrary",), VMEM_LIMIT),
        name="moe_dispatch",
    )(x3, mod3, mod3, gain.reshape(1, d), pos3, xs0)


def _expert_kernel(te_ref, nt_ref, xs_ref, wg_ref, wu_ref, wd_ref, ys_ref, *, row_tile):
    i = pl.program_id(0)

    @pl.when(i < nt_ref[0])
    def _():
        x = _from_token_tiles(xs_ref, row_tile).astype(BF16)
        hid = _silu(_bdot(x, wg_ref[...])) * _bdot(x, wu_ref[...])
        _to_token_tiles(ys_ref, _bdot(hid, wd_ref[...]))

    @pl.when(i >= nt_ref[0])
    def _():
        ys_ref[...] = jnp.zeros_like(ys_ref)


def _moe_experts(xs, tile_expert, n_tiles, wg, wu, wd, layer, row_tile):
    d, f = wg.shape[-2], wg.shape[-1]
    steps = xs.shape[0] // (row_tile * SUBLANES)
    x_map = lambda i, te, nt: (jnp.minimum(i, nt[0] - 1), 0)
    w_map = lambda i, te, nt: (layer, te[i] // E_PER_GROUP, te[i] % E_PER_GROUP, 0, 0)
    grid_spec = pltpu.PrefetchScalarGridSpec(
        num_scalar_prefetch=2,
        grid=(steps,),
        in_specs=[pl.BlockSpec((row_tile * SUBLANES, LANES), x_map),
                  pl.BlockSpec((None, None, None, d, f), w_map),
                  pl.BlockSpec((None, None, None, d, f), w_map),
                  pl.BlockSpec((None, None, None, f, d), w_map)],
        out_specs=pl.BlockSpec((row_tile * SUBLANES, LANES), lambda i, te, nt: (i, 0)),
    )
    return pl.pallas_call(
        functools.partial(_expert_kernel, row_tile=row_tile),
        grid_spec=grid_spec,
        out_shape=jax.ShapeDtypeStruct(xs.shape, F32),
        compiler_params=_cparams(("arbitrary",), VMEM_LIMIT),
        name="moe_experts",
    )(tile_expert, n_tiles, xs, wg, wu, wd)


def _row_copy_in(src_hbm, p, buf_ref, slot, which, r, sem):
    return pltpu.make_async_copy(src_hbm.at[_tile_rows(p), :], buf_ref.at[slot, which, _tile_rows(r), :],
                                 sem.at[slot])


def _combine_kernel(x_ref, g_ref, gcol_ref, pos_ref, posn_ref, ys_ref, gainf_ref, xo_ref,
                    buf_ref, idx_ref, sem_ref, isem_ref, *, tm, steps, sb, lb, final_norm):
    i = pl.program_id(0)
    slot = i % 2

    def issue(pref, s):
        cp = pltpu.make_async_copy(pref.at[0], idx_ref, isem_ref)
        cp.start()
        cp.wait()

        def body(r, carry):
            _row_copy_in(ys_ref, idx_ref[0, r], buf_ref, s, 0, r, sem_ref).start()
            _row_copy_in(ys_ref, idx_ref[1, r], buf_ref, s, 1, r, sem_ref).start()
            return carry
        lax.fori_loop(0, tm, body, 0, unroll=8)

    @pl.when(i == 0)
    def _():
        issue(pos_ref, slot)

    @pl.when(i + 1 < steps)
    def _():
        issue(posn_ref, 1 - slot)

    for which in range(2):
        pltpu.make_async_copy(ys_ref.at[pl.ds(0, tm * SUBLANES), :], buf_ref.at[slot, which],
                              sem_ref.at[slot]).wait()

    gc = gcol_ref[...]
    y = gc[:, 0:1] * _from_token_tiles(buf_ref.at[slot, 0], tm) \
        + gc[:, 1:2] * _from_token_tiles(buf_ref.at[slot, 1], tm)
    d = y.shape[-1]
    xn = x_ref[...] + g_ref[...] * y.reshape(sb, lb, d)
    if final_norm:
        xn = xn * lax.rsqrt(jnp.mean(xn * xn, axis=-1, keepdims=True) + EPS) * gainf_ref[...]
    xo_ref[...] = xn


def _moe_combine(x3, mod3, k_gate, gates_col, pos3, ys, gain_f, final_norm):
    n_seq, length, d = x3.shape
    sb, lb, steps, nl = _tok_tiling(n_seq, length, MOE_TOK_TILE)
    tm = sb * lb
    return pl.pallas_call(
        functools.partial(_combine_kernel, tm=tm, steps=steps, sb=sb, lb=lb, final_norm=final_norm),
        grid=(steps,),
        in_specs=[pl.BlockSpec((sb, lb, d), lambda i: (i // nl, i % nl, 0)),
                  pl.BlockSpec((sb, 1, d), lambda i: (i // nl, 0, k_gate)),
                  pl.BlockSpec((tm, LANES), lambda i: (i, 0)),
                  pl.BlockSpec((1, SUBLANES, tm), lambda i: (i, 0, 0)),
                  pl.BlockSpec((1, SUBLANES, tm), lambda i: (jnp.minimum(i + 1, steps - 1), 0, 0)),
                  pl.BlockSpec(memory_space=pl.ANY),
                  pl.BlockSpec((1, d), lambda i: (0, 0))],
        out_specs=pl.BlockSpec((sb, lb, d), lambda i: (i // nl, i % nl, 0)),
        out_shape=jax.ShapeDtypeStruct((n_seq, length, d), F32),
        scratch_shapes=[pltpu.VMEM((2, 2, tm * SUBLANES, LANES), F32),
                        pltpu.SMEM((SUBLANES, tm), I32),
                        pltpu.SemaphoreType.DMA((2,)),
                        pltpu.SemaphoreType.DMA(())],
        compiler_params=_cparams(("arbitrary",), VMEM_LIMIT),
        name="moe_combine",
    )(x3, mod3, gates_col, pos3, pos3, ys, gain_f.reshape(1, d))


def _hier_moe_residual(groups, k_sh, k_sc, k_gate, gain, wrt_bf16, bcol, wg, wu, wd, layer, gain_f, final_norm):
    assert len(groups) == 2
    t_all = sum(x3.shape[0] * x3.shape[1] for x3, _ in groups)
    row_tile = MOE_ROW_TILE_BIG if (2 * t_all) // N_EXPERTS >= MOE_ROW_TILE_BIG else MOE_ROW_TILE_SMALL
    max_tiles = -(-(2 * t_all) // row_tile) + N_EXPERTS
    n_tiles_pad = -(-max_tiles // LANES) * LANES
    xs_rows = max_tiles * row_tile * SUBLANES
    routed = [_moe_router(x3, mod3, k_sh, k_sc, gain, wrt_bf16, bcol, xs_rows if gi == 0 else 0)
              for gi, (x3, mod3) in enumerate(groups)]
    xs = routed[0][3]
    plans = [_moe_plan(routed[gi][0], routed[gi][2], routed[1 - gi][2], n_tiles_pad, row_tile, gi == 1)
             for gi in range(2)]
    for (x3, mod3), (pos3, _) in zip(groups, plans):
        xs = _moe_dispatch(x3, mod3, k_sh, k_sc, gain, pos3, xs)
    tinfo = plans[0][1]
    ys = _moe_experts(xs, tinfo[0, :max_tiles], tinfo[1, :1], wg, wu, wd, layer, row_tile)
    return [_moe_combine(x3, mod3, k_gate, r[1], pos3, ys, gain_f, final_norm)
            for (x3, mod3), r, (pos3, _) in zip(groups, routed, plans)]


def _multiplicity(dist):
    m = jnp.zeros(dist.shape, F32)
    for window, dil in BRANCHES:
        m = m + ((dist >= 0) & (dist <= window) & (dist % dil == 0)).astype(F32)
    return m


def _log_mult(dist):
    m = _multiplicity(dist)
    return jnp.where(m > 0, jnp.log(jnp.maximum(m, 1.0)), NEG_BIG)


def _attn_prompt_kernel(q_ref, k_ref, v_ref, nbias_ref, fbias_ref, o_ref, mf_ref, lf_ref, af_ref, *, length):
    blk = ATT_BLK
    nq = length // blk
    near = ATT_NEAR_BLOCKS
    ncls = BRANCHES[-1][1]
    csz = length // ncls
    has_far = nq > near
    scale = (HD_C ** -0.5) * LOG2_E
    nt = lambda a, b: lax.dot_general(a, b, (((1,), (1,)), ((), ())), preferred_element_type=F32)
    mm = lambda a, b: jnp.dot(a, b, preferred_element_type=F32)

    def split_heads(q):
        head0 = lax.broadcasted_iota(I32, q.shape, 1) < HD_C
        return head0, jnp.where(head0, q, 0.0).astype(BF16), jnp.where(head0, 0.0, q).astype(BF16)

    if has_far:
        fbias = fbias_ref[...]
        group = 4
        for r0 in range(0, ncls, group):
            rs = list(range(r0, r0 + group))
            rows = [pl.ds(r, csz, stride=ncls) for r in rs]
            hq = [split_heads(q_ref[0, rw, :] * scale) for rw in rows]
            kr = [k_ref[0, rw, :].astype(BF16) for rw in rows]
            vr = [v_ref[0, rw, :].astype(BF16) for rw in rows]
            s0 = [nt(h[1], k) + fbias for h, k in zip(hq, kr)]
            s1 = [nt(h[2], k) + fbias for h, k in zip(hq, kr)]
            m0 = [jnp.max(s, axis=-1, keepdims=True) for s in s0]
            m1 = [jnp.max(s, axis=-1, keepdims=True) for s in s1]
            p0 = [jnp.exp2(s - m) for s, m in zip(s0, m0)]
            p1 = [jnp.exp2(s - m) for s, m in zip(s1, m1)]
            a0 = [mm(p.astype(BF16), v) for p, v in zip(p0, vr)]
            a1 = [mm(p.astype(BF16), v) for p, v in zip(p1, vr)]
            for i, rw in enumerate(rows):
                head0 = hq[i][0]
                af_ref[rw, :] = jnp.where(head0, a0[i], a1[i])
                mf_ref[rw, :] = jnp.where(head0, m0[i], m1[i])
                lf_ref[rw, :] = jnp.where(head0, jnp.sum(p0[i], axis=-1, keepdims=True),
                                          jnp.sum(p1[i], axis=-1, keepdims=True))

    for qi in range(nq):
        rq = slice(qi * blk, (qi + 1) * blk)
        head0, q0, q1 = split_heads(q_ref[0, rq, :] * scale)
        js = list(range(max(0, qi - near + 1), qi + 1))
        ks = [k_ref[0, kj * blk:(kj + 1) * blk, :].astype(BF16) for kj in js]
        vs = [v_ref[0, kj * blk:(kj + 1) * blk, :].astype(BF16) for kj in js]
        far = has_far and qi >= near

        def fold(x, op):
            parts = [x[:, c * LANES:(c + 1) * LANES] for c in range(blk // LANES)]
            return functools.reduce(op, parts)

        q0d, q1d = q0 * 2, q1 * 2
        mr0 = functools.reduce(jnp.maximum, [fold(nt(q0d, k) * 0.5 + nbias_ref[qi - kj], jnp.maximum)
                                             for k, kj in zip(ks, js)])
        mr1 = functools.reduce(jnp.maximum, [fold(nt(q1d, k) * 0.5 + nbias_ref[qi - kj], jnp.maximum)
                                             for k, kj in zip(ks, js)])
        m0 = jnp.max(mr0, axis=-1, keepdims=True)
        m1 = jnp.max(mr1, axis=-1, keepdims=True)
        if far:
            mf = mf_ref[rq, :]
            mf0, mf1 = mf[:, 0:1], mf[:, HD_C:HD_C + 1]
            m0 = jnp.maximum(m0, mf0)
            m1 = jnp.maximum(m1, mf1)
        lr0 = lr1 = a0 = a1 = None
        for k, v, kj in zip(ks, vs, js):
            bias = nbias_ref[qi - kj]
            p0 = jnp.exp2(nt(q0, k) + bias - m0)
            p1 = jnp.exp2(nt(q1, k) + bias - m1)
            f0, f1 = fold(p0, jnp.add), fold(p1, jnp.add)
            t0, t1 = mm(p0.astype(BF16), v), mm(p1.astype(BF16), v)
            lr0, lr1 = (f0, f1) if lr0 is None else (lr0 + f0, lr1 + f1)
            a0, a1 = (t0, t1) if a0 is None else (a0 + t0, a1 + t1)
        l0 = jnp.sum(lr0, axis=-1, keepdims=True)
        l1 = jnp.sum(lr1, axis=-1, keepdims=True)
        acc = jnp.where(head0, a0, a1)
        den = jnp.where(head0, l0, l1)
        if far:
            w_far = jnp.where(head0, jnp.exp2(mf0 - m0), jnp.exp2(mf1 - m1))
            acc = acc + af_ref[rq, :] * w_far
            den = den + lf_ref[rq, :] * w_far
        o_ref[0, rq, :] = acc / den


def _attn_prompt(q3, k3, v3):
    n_seq, length, d = q3.shape
    pairs = d // LANES
    blk = ATT_BLK
    near = ATT_NEAR_BLOCKS
    ncls = BRANCHES[-1][1]
    csz = length // ncls
    per = blk // ncls
    assert (near - 1) * blk >= BRANCHES[-2][0] and length % blk == 0 and blk % ncls == 0
    a = jnp.arange(blk)
    dist = (jnp.arange(near)[:, None, None] * blk) + a[None, :, None] - a[None, None, :]
    nbias = _log_mult(dist) * jnp.where(_multiplicity(dist) > 0, LOG2_E, 1.0)
    m = jnp.arange(csz)
    far_ok = ((m[:, None] // per - m[None, :] // per) >= near) & \
        ((m[:, None] - m[None, :]) * ncls <= BRANCHES[-1][0])
    fbias = jnp.where(far_ok, 0.0, NEG_BIG).astype(F32)
    return pl.pallas_call(
        functools.partial(_attn_prompt_kernel, length=length),
        grid=(n_seq, pairs),
        in_specs=[pl.BlockSpec((1, length, LANES), lambda n, p: (n, 0, p)),
                  pl.BlockSpec((1, length, LANES), lambda n, p: (n, 0, p)),
                  pl.BlockSpec((1, length, LANES), lambda n, p: (n, 0, p)),
                  pl.BlockSpec((near, blk, blk), lambda n, p: (0, 0, 0)),
                  pl.BlockSpec((csz, csz), lambda n, p: (0, 0))],
        out_specs=pl.BlockSpec((1, length, LANES), lambda n, p: (n, 0, p)),
        out_shape=jax.ShapeDtypeStruct((n_seq, length, d), F32),
        scratch_shapes=[pltpu.VMEM((length, LANES), F32)] * 3,
        compiler_params=_cparams(("arbitrary", "arbitrary"), VMEM_LIMIT),
        name="attn_prompt",
    )(q3, k3, v3, nbias, fbias)


def _attn_sample_kernel(q_ref, kn_ref, vn_ref, kt_ref, vt_ref, bias_ref, biasn_ref, o_ref, *, heads):
    q = q_ref[0] * (HD_C ** -0.5)
    kn = kn_ref[0]
    vn = vn_ref[0]
    bias = bias_ref[...]
    biasn = biasn_ref[...]
    outs = []
    for h in range(heads):
        lo, hi = h * HD_C, (h + 1) * HD_C
        qh = q[:, lo:hi].astype(BF16)
        s = jnp.dot(qh, kt_ref[0, h].astype(BF16), preferred_element_type=F32) + bias
        sn = _bdot_nt(qh, kn[:, lo:hi]) + biasn
        m = jnp.maximum(jnp.max(s, axis=-1, keepdims=True), jnp.max(sn, axis=-1, keepdims=True))
        p = jnp.exp(s - m)
        pn = jnp.exp(sn - m)
        den = jnp.sum(p, axis=-1, keepdims=True) + jnp.sum(pn, axis=-1, keepdims=True)
        o = _bdot_nt(p, vt_ref[0, h]) + _bdot(pn, vn[:, lo:hi])
        outs.append(o / den)
    o_ref[0] = jnp.concatenate(outs, axis=-1)


def _attn_sample(q3, k3, v3, k_cache, v_cache):
    n_seq, t_len, d = q3.shape
    w_buf = k_cache.shape[1]
    kt = jnp.transpose(k_cache, (0, 2, 3, 1))
    vt = jnp.transpose(v_cache, (0, 2, 3, 1))
    heads = SAMPLE_HEADS_PER_STEP
    hw = heads * HD_C
    nj = H_C // heads
    tpos = w_buf + jnp.arange(t_len)
    bias = _log_mult(tpos[:, None] - jnp.arange(w_buf)[None, :])
    biasn = _log_mult(jnp.arange(t_len)[:, None] - jnp.arange(t_len)[None, :])
    return pl.pallas_call(
        functools.partial(_attn_sample_kernel, heads=heads),
        grid=(n_seq, nj),
        in_specs=[pl.BlockSpec((1, t_len, hw), lambda n, j: (n, 0, j)),
                  pl.BlockSpec((1, t_len, hw), lambda n, j: (n, 0, j)),
                  pl.BlockSpec((1, t_len, hw), lambda n, j: (n, 0, j)),
                  pl.BlockSpec((1, heads, HD_C, w_buf), lambda n, j: (n, j, 0, 0)),
                  pl.BlockSpec((1, heads, HD_C, w_buf), lambda n, j: (n, j, 0, 0)),
                  pl.BlockSpec((t_len, w_buf), lambda n, j: (0, 0)),
                  pl.BlockSpec((t_len, t_len), lambda n, j: (0, 0))],
        out_specs=pl.BlockSpec((1, t_len, hw), lambda n, j: (n, 0, j)),
        out_shape=jax.ShapeDtypeStruct((n_seq, t_len, d), F32),
        compiler_params=_cparams(("arbitrary", "arbitrary"), VMEM_LIMIT),
        name="attn_sample",
    )(q3, k3, v3, kt, vt, bias, biasn)


def _pad_hist(hist, n_seq, width):
    k1 = hist.shape[1]
    return jnp.concatenate([jnp.zeros((n_seq, SUBLANES - k1, width), F32), hist.astype(F32)], axis=1)


def _prep_weights(p):
    d = p['w_in_even'].shape[1]
    out = {}
    w_in = p['w_in_even'][0]
    hv = H_A * DV_A
    o_z = QKV_A
    o_a = o_z + hv
    o_g = o_a + 2 * H_A
    ab_pad = jnp.zeros((d, LANES - 2 * H_A), F32)
    out['w_in'] = jnp.concatenate([w_in[:, :o_a], w_in[:, o_a:o_g], ab_pad, w_in[:, o_g:]], axis=1).astype(BF16)
    out['w_out_even'] = p['w_out_even'][0].astype(BF16)
    out['w_qkv'] = p['w_qkv_odd'][0].astype(BF16)
    out['w_out_odd'] = p['w_out_odd'][0].astype(BF16)
    depth = p['w_router_group'].shape[0]
    wrt, bcol = [], []
    for l in range(depth):
        pad_w = jnp.zeros((SUBLANES - N_GROUPS, d), F32)
        wrt.append(jnp.concatenate([p['w_router_group'][l].T, pad_w, p['w_router_expert'][l].T], axis=0).astype(BF16))
        pad_b = jnp.full((SUBLANES - N_GROUPS,), NEG_BIG, F32)
        bcol.append(jnp.concatenate([p['b_router_group'][l], pad_b, p['b_router_expert'][l]]).reshape(-1, 1))
    out.update(wrt=wrt, bcol=bcol, wg=p['w_exp_gate'], wu=p['w_exp_up'], wd=p['w_exp_down'])
    return out


def _layer0_mixers(x3, mod3, a_ssm, a_conv, b_conv, p, w):
    n_seq, length, d = x3.shape
    t = n_seq * length
    hv = H_A * DV_A
    qkv_pre, z, ab, gcx = _norm_mod_matmul(x3, mod3, 0, 1, p['norm1'][0], w['w_in'],
                                           (QKV_A, hv, LANES, 3 * D_B))
    chunk = min(CHUNK_PROMPT, length)
    g_chunks = 4
    qkv_pre3 = qkv_pre.reshape(n_seq, length, QKV_A)
    ub, wm, qd, kd, qk, dec = _delta_prep(qkv_pre3, _pad_hist(a_conv[0], n_seq, QKV_A),
                                          ab.reshape(n_seq, length, LANES), p['conv_a_w'][0],
                                          p['a_log'][0], p['dt_bias'][0], chunk, g_chunks)
    out_a, s_new = _delta_scan(ub, wm, qd, kd, qk, dec, z.reshape(n_seq, length, hv),
                               a_ssm[0].astype(F32), p['o_gain_a'][0], chunk)
    x3, bx_tail = _even_out(gcx.reshape(n_seq, length, 3 * D_B), _pad_hist(b_conv[0], n_seq, D_B),
                            out_a.reshape(t, hv), x3, mod3, 2, p['conv_b_w'][0], w['w_out_even'])
    new_aconv = jnp.concatenate([a_conv[0].astype(F32), qkv_pre3], axis=1)[:, -(CONV_A - 1):]
    new_bconv = jnp.concatenate([b_conv[0].astype(F32), bx_tail], axis=1)[:, -(CONV_B - 1):]
    return x3, s_new[None], new_aconv[None], new_bconv[None]


def _layer1_attention(x3, mod3, kv_k, kv_v, p, w, sample):
    n_seq, length, d = x3.shape
    t = n_seq * length
    if sample:
        q_new, k_new, v_new = (a.reshape(n_seq, length, d) for a in
                               _norm_mod_matmul(x3, mod3, 0, 1, p['norm1'][1], w['w_qkv'], (d, d, d)))
        attn = _attn_sample(q_new, k_new, v_new, kv_k[0], kv_v[0])
        new_k = k_new.reshape(n_seq, length, H_C, HD_C)
        new_v = v_new.reshape(n_seq, length, H_C, HD_C)
    else:
        q_new, k_new, v_new, k_t, v_t = _norm_mod_matmul(x3, mod3, 0, 1, p['norm1'][1], w['w_qkv'],
                                                         (d, d, d), transposed=(1, 2))
        q_new, k_new, v_new = (a.reshape(n_seq, length, d) for a in (q_new, k_new, v_new))
        attn = _attn_prompt(q_new, k_new, v_new)
        keep = min(BRANCHES[-1][0], length)
        new_k = jnp.transpose(k_t[:, :, length - keep:].reshape(n_seq, H_C, HD_C, keep), (0, 3, 1, 2))
        new_v = jnp.transpose(v_t[:, :, length - keep:].reshape(n_seq, H_C, HD_C, keep), (0, 3, 1, 2))
    x3 = _proj_residual(attn.reshape(t, d), x3, mod3, 2, w['w_out_odd'])
    return x3, new_k[None], new_v[None]


def kernel(x_prompt, x_sample, state_a_ssm, state_a_conv, state_b_conv, cache_c_k, cache_c_v, c_prompt, c_sample, w_ada, b_ada, norm1, norm2, norm_f, w_in_even, conv_a_w, a_log, dt_bias, o_gain_a, conv_b_w, w_out_even, w_qkv_odd, w_out_odd, w_router_group, b_router_group, w_router_expert, b_router_expert, w_exp_gate, w_exp_up, w_exp_down):
    p = dict(norm1=norm1, norm2=norm2, norm_f=norm_f, w_in_even=w_in_even, conv_a_w=conv_a_w,
             a_log=a_log, dt_bias=dt_bias, o_gain_a=o_gain_a, conv_b_w=conv_b_w, w_out_even=w_out_even,
             w_qkv_odd=w_qkv_odd, w_out_odd=w_out_odd, w_router_group=w_router_group,
             b_router_group=b_router_group, w_router_expert=w_router_expert,
             b_router_expert=b_router_expert, w_exp_gate=w_exp_gate, w_exp_up=w_exp_up,
             w_exp_down=w_exp_down)
    w = _prep_weights(p)
    nb = x_prompt.shape[0]
    ns = x_sample.shape[0]
    mod_all = _ada_mod(jnp.concatenate([c_prompt, c_sample], axis=0), w_ada, b_ada)
    depth = w_ada.shape[0]
    mod_p = [mod_all[l, :nb] for l in range(depth)]
    mod_s = [mod_all[l, nb:] for l in range(depth)]
    n_even = state_a_ssm.shape[0]
    zero_ssm = jnp.zeros((n_even, nb, H_A, DK_A, DV_A), F32)
    zero_aconv = jnp.zeros((n_even, nb, CONV_A - 1, QKV_A), F32)
    zero_bconv = jnp.zeros((n_even, nb, CONV_B - 1, D_B), F32)
    d = x_prompt.shape[-1]
    m3_p = [m.reshape(nb, 1, 6 * d) for m in mod_p]
    m3_s = [m.reshape(ns, 1, 6 * d) for m in mod_s]

    def moe(xp, xs_, layer, final_norm):
        return _hier_moe_residual([(xp, m3_p[layer]), (xs_, m3_s[layer])], 3, 4, 5, norm2[layer],
                                  w['wrt'][layer], w['bcol'][layer], w['wg'], w['wu'], w['wd'], layer,
                                  norm_f, final_norm)

    xp, ssm_p, aconv_p, bconv_p = _layer0_mixers(x_prompt, m3_p[0], zero_ssm, zero_aconv, zero_bconv, p, w)
    xs_, ssm_s, aconv_s, bconv_s = _layer0_mixers(x_sample, m3_s[0], state_a_ssm, state_a_conv,
                                                  state_b_conv, p, w)
    xp, xs_ = moe(xp, xs_, 0, False)
    xp, k_p, v_p = _layer1_attention(xp, m3_p[1], None, None, p, w, False)
    xs_, k_s, v_s = _layer1_attention(xs_, m3_s[1], cache_c_k, cache_c_v, p, w, True)
    y_p, y_s = moe(xp, xs_, 1, True)
    return (y_p, y_s, ssm_p, ssm_s, aconv_p, aconv_s, bconv_p, bconv_s, k_p, k_s, v_p, v_s)
```

```python
import functools
import math

import jax
import jax.numpy as jnp
from jax import lax
from jax.experimental import pallas as pl
from jax.experimental.pallas import tpu as pltpu

F32 = jnp.float32
BF16 = jnp.bfloat16
I32 = jnp.int32

EPS = 1e-6
NEG_BIG = -1e30

LANES = 128
SUBLANES = 8
VMEM_BYTES_V7X = 64 * 1024 * 1024
VMEM_LIMIT = 56 * 1024 * 1024

H_A, DK_A, DV_A = 4, 128, 128
CONV_A, CONV_B = 4, 3
QKV_A = H_A * (2 * DK_A + DV_A)
D_B = 512
H_C, HD_C = 16, 64
BRANCHES = ((128, 1), (512, 4), (2048, 16))
N_GROUPS, E_PER_GROUP = 4, 8
N_EXPERTS = N_GROUPS * E_PER_GROUP
CHUNK_PROMPT = 64
DELTA_CHUNKS_PER_STEP = 4
DELTA_SEQS_PER_STEP = 8

TOK_TILE = 512
MOE_TOK_TILE = 512
MOE_ROW_TILE_BIG = 512
MOE_ROW_TILE_SMALL = 128
ATT_BLK = 256
ATT_NEAR_BLOCKS = 3
LOG2_E = math.log2(math.e)
SEQ_BLOCK_SCAN = 8
SAMPLE_HEADS_PER_STEP = 16


def _cparams(sem, vmem=None):
    return pltpu.CompilerParams(dimension_semantics=sem, vmem_limit_bytes=vmem)


def _silu(x):
    return x * jax.nn.sigmoid(x)


def _bdot(a, b):
    return jnp.dot(a.astype(BF16), b.astype(BF16), preferred_element_type=F32)


def _bdot_nt(a, b):
    return lax.dot_general(a.astype(BF16), b.astype(BF16), (((1,), (1,)), ((), ())),
                           preferred_element_type=F32)


def _bdot_tn(a, b):
    return lax.dot_general(a.astype(BF16), b.astype(BF16), (((0,), (0,)), ((), ())),
                           preferred_element_type=F32)


def _split_bf16(x):
    hi = x.astype(BF16)
    return hi, (x - hi.astype(F32)).astype(BF16)


def _dot3(a, b):
    ah, al = _split_bf16(a)
    bh, bl = _split_bf16(b)
    dot = functools.partial(jnp.dot, preferred_element_type=F32)
    return dot(ah, bh) + (dot(ah, bl) + dot(al, bh))


def _fdot(a, b):
    return jnp.dot(a, b, preferred_element_type=F32, precision=lax.Precision.HIGHEST)


def _fdot_nt(a, b):
    return lax.dot_general(a, b, (((1,), (1,)), ((), ())), preferred_element_type=F32,
                           precision=lax.Precision.HIGHEST)


def _tok_tiling(n_seq, length, tile):
    if length >= tile:
        sb, lb = 1, tile
    else:
        sb, lb = tile // length, length
    nl = length // lb
    return sb, lb, (n_seq // sb) * nl, nl


def _norm_mod(x, gain, sc, sh):
    sb, lb, d = x.shape
    y = x * lax.rsqrt(jnp.mean(x * x, axis=-1, keepdims=True) + EPS) * gain
    h = y * (1.0 + sc) + sh
    return h.reshape(sb * lb, d)


def _ada_kernel(c_ref, w_ref, b_ref, o_ref):
    c = _silu(c_ref[...])
    o_ref[0] = _bdot(c, w_ref[0]) + b_ref[0]


def _ada_mod(c_all, w_ada, b_ada):
    r, d = c_all.shape
    depth, _, n6 = w_ada.shape
    tn = 1536
    return pl.pallas_call(
        _ada_kernel,
        grid=(depth, n6 // tn),
        in_specs=[pl.BlockSpec((r, d), lambda l, j: (0, 0)),
                  pl.BlockSpec((1, d, tn), lambda l, j: (l, 0, j)),
                  pl.BlockSpec((1, 1, tn), lambda l, j: (l, 0, j))],
        out_specs=pl.BlockSpec((1, r, tn), lambda l, j: (l, 0, j)),
        out_shape=jax.ShapeDtypeStruct((depth, r, n6), F32),
        compiler_params=_cparams(("arbitrary", "arbitrary"), VMEM_LIMIT),
        name="ada_mod",
    )(c_all, w_ada, b_ada.reshape(depth, 1, n6))


def _nmm_kernel(x_ref, sh_ref, sc_ref, gain_ref, w_ref, *o_refs, splits, transposed):
    h = _norm_mod(x_ref[...], gain_ref[...], sc_ref[...], sh_ref[...]).astype(BF16)
    off = 0
    vals = []
    for o_ref, n in zip(o_refs, splits):
        vals.append(jnp.dot(h, w_ref[:, off:off + n], preferred_element_type=F32))
        o_ref[...] = vals[-1]
        off += n
    for o_ref, j in zip(o_refs[len(splits):], transposed):
        o_ref[0] = vals[j].T


def _norm_mod_matmul(x3, mod3, k_sh, k_sc, gain, w_bf16, splits, transposed=()):
    n_seq, length, d = x3.shape
    sb, lb, steps, nl = _tok_tiling(n_seq, length, TOK_TILE)
    rows = sb * lb
    t = n_seq * length
    n_out = w_bf16.shape[1]
    assert sum(splits) == n_out and (not transposed or sb == 1)
    t_specs = [pl.BlockSpec((1, splits[j], lb), lambda i: (i // nl, 0, i % nl)) for j in transposed]
    t_shapes = [jax.ShapeDtypeStruct((n_seq, splits[j], length), F32) for j in transposed]
    return pl.pallas_call(
        functools.partial(_nmm_kernel, splits=splits, transposed=transposed),
        grid=(steps,),
        in_specs=[pl.BlockSpec((sb, lb, d), lambda i: (i // nl, i % nl, 0)),
                  pl.BlockSpec((sb, 1, d), lambda i: (i // nl, 0, k_sh)),
                  pl.BlockSpec((sb, 1, d), lambda i: (i // nl, 0, k_sc)),
                  pl.BlockSpec((1, d), lambda i: (0, 0)),
                  pl.BlockSpec((d, n_out), lambda i: (0, 0))],
        out_specs=[pl.BlockSpec((rows, n), lambda i: (i, 0)) for n in splits] + t_specs,
        out_shape=[jax.ShapeDtypeStruct((t, n), F32) for n in splits] + t_shapes,
        compiler_params=_cparams(("arbitrary",), VMEM_LIMIT),
        name="norm_mod_matmul",
    )(x3, mod3, mod3, gain.reshape(1, d), w_bf16)


def _delta_prep_kernel(qkv_ref, halo_ref, hist_ref, ab_ref, cw_ref, al_ref, dtb_ref, lt_ref, sel_ref,
                       ub_ref, wm_ref, qd_ref, kd_ref, qk_ref, dec_ref,
                       *, g_chunks, chunk, chunks_are_seqs, chunks_per_seq):
    c = chunk
    step = pl.program_id(0)
    cw = cw_ref[...]
    row = lax.broadcasted_iota(I32, (c, c), 0)
    col = lax.broadcasted_iota(I32, (c, c), 1)
    lower_incl = row >= col
    strict = row > col
    eye = (row == col).astype(F32)
    n_fac = max(int(math.ceil(math.log2(c))), 1)

    qkvs, abs_ = [], []
    for g in range(g_chunks):
        cur = qkv_ref[g] if chunks_are_seqs else qkv_ref[0, g * c:(g + 1) * c, :]
        if chunks_are_seqs:
            prev = hist_ref[g]
        elif g == 0:
            first = (step % (chunks_per_seq // g_chunks)) == 0
            prev = jnp.where(first, hist_ref[0], halo_ref[0])
        else:
            prev = qkv_ref[0, g * c - SUBLANES:g * c, :]
        row8 = lax.broadcasted_iota(I32, (SUBLANES, cur.shape[1]), 0)
        conv = cur * cw[CONV_A - 1:CONV_A, :]
        for dback in range(1, CONV_A):
            rolled = pltpu.roll(cur, dback, 0)
            top = jnp.where(row8 < dback, pltpu.roll(prev, dback, 0), rolled[0:SUBLANES])
            shifted = top if c == SUBLANES else jnp.concatenate([top, rolled[SUBLANES:]], axis=0)
            conv = conv + shifted * cw[CONV_A - 1 - dback:CONV_A - dback, :]
        qkvs.append(_silu(conv))
        abs_.append(ab_ref[g] if chunks_are_seqs else ab_ref[0, g * c:(g + 1) * c, :])

    g_alls = [-jnp.exp(al_ref[...]) * jax.nn.softplus(ab + dtb_ref[...]) for ab in abs_]
    sigs = [jax.nn.sigmoid(ab) for ab in abs_]
    gcum_alls = [_fdot(lt_ref[...], ga) for ga in g_alls]
    gcum_rows = [_fdot_nt(sel_ref[...], gc) for gc in gcum_alls]

    chains = [(g, h) for g in range(g_chunks) for h in range(H_A)]
    qs, ks, vs, betas, gcs, gammas, egs = [], [], [], [], [], [], []
    for g, h in chains:
        qkv = qkvs[g]
        q = qkv[:, h * DK_A:(h + 1) * DK_A]
        k = qkv[:, H_A * DK_A + h * DK_A:H_A * DK_A + (h + 1) * DK_A]
        v = qkv[:, 2 * H_A * DK_A + h * DV_A:2 * H_A * DK_A + (h + 1) * DV_A]
        qs.append(q * lax.rsqrt(jnp.sum(q * q, axis=-1, keepdims=True) + EPS) * (DK_A ** -0.5))
        ks.append(k * lax.rsqrt(jnp.sum(k * k, axis=-1, keepdims=True) + EPS))
        vs.append(v)
        betas.append(sigs[g][:, H_A + h:H_A + h + 1])
        gc = gcum_alls[g][:, h:h + 1]
        gr = gcum_rows[g][h:h + 1, :]
        gcs.append(gc)
        gammas.append(jnp.exp(jnp.where(lower_incl, gc - gr, NEG_BIG)))
        egs.append(jnp.exp(gc))

    kq = [_bdot_nt(jnp.concatenate([k, q], axis=0), k) for k, q in zip(ks, qs)]
    a_mats = [jnp.where(strict, b * x[0:c] * gm, 0.0) for b, x, gm in zip(betas, kq, gammas)]
    qkms = [x[c:2 * c] * gm for x, gm in zip(kq, gammas)]
    m_pows = [-a for a in a_mats]
    t_invs = [eye + m for m in m_pows]
    for _ in range(n_fac - 1):
        m_pows = [_bdot(m, m) for m in m_pows]
        t_invs = [t + _bdot(t, m) for t, m in zip(t_invs, m_pows)]
    resids = [eye - t - _dot3(a, t) for a, t in zip(a_mats, t_invs)]
    t_invs = [t + _bdot(t, r) for t, r in zip(t_invs, resids)]
    rhss = [jnp.concatenate([v * b, k * (b * eg)], axis=-1) for v, k, b, eg in zip(vs, ks, betas, egs)]
    sols = [rhs + _dot3(t - eye, rhs) for t, rhs in zip(t_invs, rhss)]

    for g in range(g_chunks):
        idx = [i for i, (gg, _) in enumerate(chains) if gg == g]
        g_last = [gcum_alls[g][c - 1:c, h:h + 1] for h in range(H_A)]

        def put(ref, parts):
            val = jnp.concatenate(parts, axis=-1).astype(ref.dtype)
            if chunks_are_seqs:
                ref[g] = val
            else:
                ref[0, g * c:(g + 1) * c, :] = val

        put(ub_ref, [sols[i][:, :DV_A] for i in idx])
        put(wm_ref, [sols[i][:, DV_A:] for i in idx])
        put(qd_ref, [qs[i] * egs[i] for i in idx])
        put(kd_ref, [ks[i] * jnp.exp(g_last[h] - gcs[i]) for h, i in enumerate(idx)])
        put(qk_ref, [qkms[i] for i in idx])
        dec_ref[g] = jnp.concatenate([jnp.broadcast_to(jnp.exp(gl), (SUBLANES, DV_A)) for gl in g_last], axis=-1)


def _delta_prep(qkv_pre3, hist8, ab3, conv_w, a_log, dt_bias, chunk, g_chunks):
    n_seq, length, w = qkv_pre3.shape
    c = chunk
    chunks_are_seqs = (length == c)
    nc = length // c
    if chunks_are_seqs:
        steps = n_seq // g_chunks
        blk = (g_chunks, c, w)
        x_map = lambda i: (i, 0, 0)
        halo_map = lambda i: (i, 0, 0)
        hist_spec = pl.BlockSpec((g_chunks, SUBLANES, w), lambda i: (i, 0, 0))
        ab_spec = pl.BlockSpec((g_chunks, c, LANES), lambda i: (i, 0, 0))
        out_map = lambda i: (i, 0, 0)
        out_rows = (g_chunks, c)
    else:
        rows = g_chunks * c
        spb = nc // g_chunks
        steps = n_seq * spb
        blk = (1, rows, w)
        x_map = lambda i: (i // spb, i % spb, 0)
        halo_map = lambda i: (i // spb, jnp.maximum((i % spb) * (rows // SUBLANES) - 1, 0), 0)
        hist_spec = pl.BlockSpec((1, SUBLANES, w), lambda i: (i // spb, 0, 0))
        ab_spec = pl.BlockSpec((1, rows, LANES), lambda i: (i // spb, i % spb, 0))
        out_map = x_map
        out_rows = (1, rows)
    op_dtype = BF16 if c % (2 * SUBLANES) == 0 else F32
    lt = jnp.tril(jnp.ones((c, c), F32))
    sel = jnp.eye(SUBLANES, LANES, dtype=F32)
    hv = H_A * DV_A
    al = jnp.zeros((1, LANES), F32).at[0, :H_A].set(a_log)
    dtb = jnp.zeros((1, LANES), F32).at[0, :H_A].set(dt_bias)
    kern = functools.partial(_delta_prep_kernel, g_chunks=g_chunks, chunk=c,
                             chunks_are_seqs=chunks_are_seqs, chunks_per_seq=nc)
    big = lambda width: pl.BlockSpec(out_rows + (width,), out_map)
    return pl.pallas_call(
        kern,
        grid=(steps,),
        in_specs=[pl.BlockSpec(blk, x_map),
                  pl.BlockSpec((1, SUBLANES, w), halo_map) if not chunks_are_seqs
                  else pl.BlockSpec((g_chunks, SUBLANES, w), halo_map),
                  hist_spec, ab_spec,
                  pl.BlockSpec((CONV_A, w), lambda i: (0, 0)),
                  pl.BlockSpec((1, LANES), lambda i: (0, 0)),
                  pl.BlockSpec((1, LANES), lambda i: (0, 0)),
                  pl.BlockSpec((c, c), lambda i: (0, 0)),
                  pl.BlockSpec((SUBLANES, LANES), lambda i: (0, 0))],
        out_specs=[big(hv), big(hv), big(hv), big(hv), big(H_A * c),
                   pl.BlockSpec((g_chunks, SUBLANES, hv), lambda i: (i, 0, 0))],
        out_shape=[jax.ShapeDtypeStruct((n_seq, length, hv), F32)]
        + [jax.ShapeDtypeStruct((n_seq, length, hv), op_dtype)] * 3
        + [jax.ShapeDtypeStruct((n_seq, length, H_A * c), op_dtype),
           jax.ShapeDtypeStruct((n_seq * nc, SUBLANES, hv), F32)],
        compiler_params=_cparams(("arbitrary",), VMEM_LIMIT),
        name="delta_prep",
    )(qkv_pre3, hist8 if chunks_are_seqs else qkv_pre3, hist8, ab3, conv_w, al, dtb, lt, sel)


def _delta_scan_kernel(ub_ref, wm_ref, qd_ref, kd_ref, qk_ref, dec_ref, z_ref, s0_ref, og_ref,
                       o_ref, sn_ref, s_ref, *, nb, chunk):
    c = chunk
    j = pl.program_id(1)

    @pl.when(j == 0)
    def _():
        s_ref[...] = s0_ref[...]

    og = og_ref[...]
    chains = [(b, h) for b in range(nb) for h in range(H_A)]
    sl = lambda h: slice(h * DV_A, (h + 1) * DV_A)
    states = [s_ref[b, h] for b, h in chains]
    ws = [_bdot(jnp.concatenate([wm_ref[b, :, sl(h)], qd_ref[b, :, sl(h)]], axis=0), s)
          for (b, h), s in zip(chains, states)]
    us = [ub_ref[b, :, sl(h)] - x[0:c] for (b, h), x in zip(chains, ws)]
    os_ = [x[c:2 * c] + _bdot(qk_ref[b, :, h * c:(h + 1) * c], u) for (b, h), x, u in zip(chains, ws, us)]
    for (b, h), s, u in zip(chains, states, us):
        s_ref[b, h] = s * dec_ref[b, 0, 0:1, sl(h)] + _bdot_tn(kd_ref[b, :, sl(h)], u)
    for b in range(nb):
        outs = []
        for h in range(H_A):
            o = os_[b * H_A + h]
            on = o * lax.rsqrt(jnp.mean(o * o, axis=-1, keepdims=True) + EPS) * og
            outs.append(on * _silu(z_ref[b, :, sl(h)]))
        o_ref[b] = jnp.concatenate(outs, axis=-1)
    sn_ref[...] = s_ref[...]


def _delta_scan(ub, wm, qd, kd, qk, dec, z3, s0, o_gain, chunk):
    n_seq, length, hv = ub.shape
    c = chunk
    nc = length // c
    nb = SEQ_BLOCK_SCAN
    dec4 = dec.reshape(n_seq, nc, SUBLANES, hv)
    tok = lambda width: pl.BlockSpec((nb, c, width), lambda b, j: (b, j, 0))
    st = pl.BlockSpec((nb, H_A, DK_A, DV_A), lambda b, j: (b, 0, 0, 0))
    return pl.pallas_call(
        functools.partial(_delta_scan_kernel, nb=nb, chunk=c),
        grid=(n_seq // nb, nc),
        in_specs=[tok(hv), tok(hv), tok(hv), tok(hv), tok(H_A * c),
                  pl.BlockSpec((nb, 1, SUBLANES, hv), lambda b, j: (b, j, 0, 0)),
                  tok(hv), st, pl.BlockSpec((1, DV_A), lambda b, j: (0, 0))],
        out_specs=[tok(hv), st],
        out_shape=[jax.ShapeDtypeStruct((n_seq, length, hv), F32),
                   jax.ShapeDtypeStruct((n_seq, H_A, DK_A, DV_A), F32)],
        scratch_shapes=[pltpu.VMEM((nb, H_A, DK_A, DV_A), F32)],
        compiler_params=_cparams(("arbitrary", "arbitrary"), VMEM_LIMIT),
        name="delta_scan",
    )(ub, wm, qd, kd, qk, dec4, z3, s0, o_gain.reshape(1, DV_A))


def _even_out_kernel(gcx_ref, halo_ref, hist_ref, oa_ref, x_ref, g1_ref, cw_ref, w_ref,
                     xo_ref, tail_ref, ext_ref, *, sb, lb, nl):
    step = pl.program_id(0)
    gcx = gcx_ref[...]
    b_gate = gcx[:, :, 0:D_B]
    bx = gcx[:, :, D_B:2 * D_B] * gcx[:, :, 2 * D_B:3 * D_B]
    if nl == 1:
        prev = hist_ref[...]
    else:
        hb = halo_ref[...]
        first = (step % nl) == 0
        prev = jnp.where(first, hist_ref[...], hb[:, :, D_B:2 * D_B] * hb[:, :, 2 * D_B:3 * D_B])
    ext_ref[:, 0:SUBLANES, :] = prev
    ext_ref[:, SUBLANES:SUBLANES + lb, :] = bx
    cw = cw_ref[...]
    conv = ext_ref[:, pl.ds(SUBLANES - (CONV_B - 1), lb), :] * cw[0:1, :]
    for j in range(1, CONV_B):
        conv = conv + ext_ref[:, pl.ds(SUBLANES - (CONV_B - 1) + j, lb), :] * cw[j:j + 1, :]
    out_b = (b_gate * conv).reshape(sb * lb, D_B)
    hv = H_A * DV_A
    y = _bdot(oa_ref[...], w_ref[0:hv, :]) + _bdot(out_b, w_ref[hv:hv + D_B, :])
    d = y.shape[-1]
    xo_ref[...] = x_ref[...] + g1_ref[...] * y.reshape(sb, lb, d)
    tail_ref[...] = bx[:, lb - SUBLANES:lb, :]


def _even_out(gcx3, hist8, out_a2, x3, mod3, k_gate, conv_w, w_out_bf16):
    n_seq, length, d = x3.shape
    sb, lb, steps, nl = _tok_tiling(n_seq, length, TOK_TILE)
    rows = sb * lb
    w3 = gcx3.shape[-1]
    hv = H_A * DV_A
    halo_map = lambda i: (i // nl, jnp.maximum((i % nl) * (lb // SUBLANES) - 1, 0), 0)
    return pl.pallas_call(
        functools.partial(_even_out_kernel, sb=sb, lb=lb, nl=nl),
        grid=(steps,),
        in_specs=[pl.BlockSpec((sb, lb, w3), lambda i: (i // nl, i % nl, 0)),
                  pl.BlockSpec((sb, SUBLANES, w3), halo_map if nl > 1 else (lambda i: (i, 0, 0))),
                  pl.BlockSpec((sb, SUBLANES, D_B), lambda i: (i // nl, 0, 0)),
                  pl.BlockSpec((rows, hv), lambda i: (i, 0)),
                  pl.BlockSpec((sb, lb, d), lambda i: (i // nl, i % nl, 0)),
                  pl.BlockSpec((sb, 1, d), lambda i: (i // nl, 0, k_gate)),
                  pl.BlockSpec((CONV_B, D_B), lambda i: (0, 0)),
                  pl.BlockSpec((hv + D_B, d), lambda i: (0, 0))],
        out_specs=[pl.BlockSpec((sb, lb, d), lambda i: (i // nl, i % nl, 0)),
                   pl.BlockSpec((sb, SUBLANES, D_B), lambda i: (i // nl, 0, 0))],
        out_shape=[jax.ShapeDtypeStruct((n_seq, length, d), F32),
                   jax.ShapeDtypeStruct((n_seq, SUBLANES, D_B), F32)],
        scratch_shapes=[pltpu.VMEM((sb, SUBLANES + lb, D_B), F32)],
        compiler_params=_cparams(("arbitrary",), VMEM_LIMIT),
        name="even_out",
    )(gcx3, gcx3, hist8, out_a2, x3, mod3, conv_w, w_out_bf16)


def _proj_res_kernel(a_ref, x_ref, g_ref, w_ref, xo_ref, *, sb, lb):
    y = _bdot(a_ref[...], w_ref[...])
    xo_ref[...] = x_ref[...] + g_ref[...] * y.reshape(sb, lb, y.shape[-1])


def _proj_residual(a2, x3, mod3, k_gate, w_bf16):
    n_seq, length, d = x3.shape
    sb, lb, steps, nl = _tok_tiling(n_seq, length, TOK_TILE)
    rows = sb * lb
    ka = a2.shape[-1]
    return pl.pallas_call(
        functools.partial(_proj_res_kernel, sb=sb, lb=lb),
        grid=(steps,),
        in_specs=[pl.BlockSpec((rows, ka), lambda i: (i, 0)),
                  pl.BlockSpec((sb, lb, d), lambda i: (i // nl, i % nl, 0)),
                  pl.BlockSpec((sb, 1, d), lambda i: (i // nl, 0, k_gate)),
                  pl.BlockSpec((ka, d), lambda i: (0, 0))],
        out_specs=pl.BlockSpec((sb, lb, d), lambda i: (i // nl, i % nl, 0)),
        out_shape=jax.ShapeDtypeStruct((n_seq, length, d), F32),
        compiler_params=_cparams(("arbitrary",), VMEM_LIMIT),
        name="proj_residual",
    )(a2, x3, mod3, w_bf16)


def _router_kernel(x_ref, sh_ref, sc_ref, gain_ref, wrt_ref, bcol_ref, utri_ref,
                   mi_ref, gcol_ref, cnt_ref, *rest):
    carry_ref = rest[-1]
    i = pl.program_id(0)

    @pl.when(i == 0)
    def _():
        carry_ref[...] = jnp.zeros_like(carry_ref)

    if len(rest) == 2:
        rest[0][...] = jnp.zeros_like(rest[0])

    h2 = _norm_mod(x_ref[...], gain_ref[...], sc_ref[...], sh_ref[...])
    tm = h2.shape[0]
    logits = _bdot_nt(wrt_ref[...], h2) + bcol_ref[...]
    lg = logits[0:SUBLANES]
    e = jnp.exp(lg - jnp.max(lg, axis=0, keepdims=True))
    pg = e / jnp.sum(e, axis=0, keepdims=True)
    p_top = jnp.max(pg, axis=0, keepdims=True)
    rid8 = lax.broadcasted_iota(I32, (SUBLANES, tm), 0)
    g_top = jnp.min(jnp.where(pg == p_top, rid8, SUBLANES), axis=0, keepdims=True)
    le = logits[SUBLANES:SUBLANES + N_EXPERTS]
    sel = jnp.zeros((E_PER_GROUP, tm), F32)
    for gi in range(N_GROUPS):
        sel = sel + jnp.where(g_top == gi, le[gi * E_PER_GROUP:(gi + 1) * E_PER_GROUP], 0.0)
    e2 = jnp.exp(sel - jnp.max(sel, axis=0, keepdims=True))
    p_in = e2 / jnp.sum(e2, axis=0, keepdims=True)
    w_a = jnp.max(p_in, axis=0, keepdims=True)
    i_a = jnp.min(jnp.where(p_in == w_a, rid8, SUBLANES), axis=0, keepdims=True)
    p_rest = jnp.where(rid8 == i_a, -1.0, p_in)
    w_b = jnp.max(p_rest, axis=0, keepdims=True)
    i_b = jnp.min(jnp.where(p_rest == w_b, rid8, SUBLANES), axis=0, keepdims=True)
    den = w_a + w_b
    gate1 = p_top * (w_a / den)
    gate2 = p_top * (w_b / den)
    ex1 = g_top * E_PER_GROUP + i_a
    ex2 = g_top * E_PER_GROUP + i_b

    rid32 = lax.broadcasted_iota(I32, (N_EXPERTS, tm), 0)
    oh1 = rid32 == ex1
    oh2 = rid32 == ex2
    ohc = jnp.where(oh1, 1.0, jnp.where(oh2, 1.0, 0.0))
    cum = _bdot(ohc, utri_ref[...])
    carry = carry_ref[...]
    base = cum - ohc + carry[:, 0:1]
    r1 = jnp.sum(jnp.where(oh1, base, 0.0), axis=0, keepdims=True)
    r2 = jnp.sum(jnp.where(oh2, base, 0.0), axis=0, keepdims=True)
    new_carry = carry + cum[:, tm - 1:tm]
    carry_ref[...] = new_carry
    cnt_ref[...] = new_carry

    bro = lambda v: jnp.broadcast_to(v, (SUBLANES, tm))
    mi_ref[...] = jnp.where(rid8 == 0, bro(ex1),
                            jnp.where(rid8 == 1, bro(ex2),
                                      jnp.where(rid8 == 2, bro(r1.astype(I32)),
                                                jnp.where(rid8 == 3, bro(r2.astype(I32)), 0))))
    rid128 = lax.broadcasted_iota(I32, (LANES, tm), 0)
    g128 = jnp.where(rid128 == 0, jnp.broadcast_to(gate1, (LANES, tm)),
                     jnp.where(rid128 == 1, jnp.broadcast_to(gate2, (LANES, tm)), 0.0))
    gcol_ref[...] = g128.T


def _moe_router(x3, mod3, k_sh, k_sc, gain, wrt_bf16, bcol, xs_rows):
    n_seq, length, d = x3.shape
    sb, lb, steps, nl = _tok_tiling(n_seq, length, TOK_TILE)
    tm = sb * lb
    t = n_seq * length
    utri = jnp.triu(jnp.ones((tm, tm), F32)).astype(BF16)
    r_rows = wrt_bf16.shape[0]
    assert xs_rows % (steps * SUBLANES) == 0
    zrows = xs_rows // steps
    z_specs = [pl.BlockSpec((zrows, LANES), lambda i: (i, 0))] if xs_rows else []
    z_shapes = [jax.ShapeDtypeStruct((xs_rows, LANES), F32)] if xs_rows else []
    return pl.pallas_call(
        _router_kernel,
        grid=(steps,),
        in_specs=[pl.BlockSpec((sb, lb, d), lambda i: (i // nl, i % nl, 0)),
                  pl.BlockSpec((sb, 1, d), lambda i: (i // nl, 0, k_sh)),
                  pl.BlockSpec((sb, 1, d), lambda i: (i // nl, 0, k_sc)),
                  pl.BlockSpec((1, d), lambda i: (0, 0)),
                  pl.BlockSpec((r_rows, d), lambda i: (0, 0)),
                  pl.BlockSpec((r_rows, 1), lambda i: (0, 0)),
                  pl.BlockSpec((tm, tm), lambda i: (0, 0))],
        out_specs=[pl.BlockSpec((SUBLANES, tm), lambda i: (0, i)),
                   pl.BlockSpec((tm, LANES), lambda i: (i, 0)),
                   pl.BlockSpec((N_EXPERTS, LANES), lambda i: (0, 0))] + z_specs,
        out_shape=[jax.ShapeDtypeStruct((SUBLANES, t), I32),
                   jax.ShapeDtypeStruct((t, LANES), F32),
                   jax.ShapeDtypeStruct((N_EXPERTS, LANES), F32)] + z_shapes,
        scratch_shapes=[pltpu.VMEM((N_EXPERTS, LANES), F32)],
        compiler_params=_cparams(("arbitrary",), VMEM_LIMIT),
        name="moe_router",
    )(x3, mod3, mod3, gain.reshape(1, d), wrt_bf16, bcol, utri)


def _plan_kernel(mi_ref, cnt_ref, cnt_other_ref, ltri_ref, pos_ref, tinfo_ref, *, n_tiles_pad, row_tile, second):
    cnt = cnt_ref[...] + cnt_other_ref[...]
    nt = jnp.floor((cnt + (row_tile - 1)) * (1.0 / row_tile))
    tstart = _bdot(ltri_ref[...], nt)
    mi = mi_ref[...]
    tm = mi.shape[1]
    rid32 = lax.broadcasted_iota(I32, (N_EXPERTS, tm), 0)
    ts_col = tstart[:, 0:1]
    base_col = cnt_other_ref[:, 0:1] if second else jnp.zeros((N_EXPERTS, 1), F32)

    def pos_of(ex, rank):
        hit = rid32 == ex
        start = jnp.sum(jnp.where(hit, ts_col, 0.0), axis=0, keepdims=True)
        base = jnp.sum(jnp.where(hit, base_col, 0.0), axis=0, keepdims=True)
        return start.astype(I32) * row_tile + base.astype(I32) + rank

    p1 = pos_of(mi[0:1], mi[2:3])
    p2 = pos_of(mi[1:2], mi[3:4])
    rid8 = lax.broadcasted_iota(I32, (SUBLANES, tm), 0)
    pos_ref[0] = jnp.where(rid8 == 0, jnp.broadcast_to(p1, (SUBLANES, tm)),
                           jnp.where(rid8 == 1, jnp.broadcast_to(p2, (SUBLANES, tm)), 0))
    tend_col = ts_col + nt[:, 0:1]
    jt = lax.broadcasted_iota(I32, (N_EXPERTS, n_tiles_pad), 1).astype(F32)
    te = jnp.sum(jnp.where(tend_col <= jt, 1.0, 0.0), axis=0, keepdims=True)
    te = jnp.minimum(te, N_EXPERTS - 1.0).astype(I32)
    total = jnp.sum(nt[:, 0:1], axis=0, keepdims=True).astype(I32)
    rid8t = lax.broadcasted_iota(I32, (SUBLANES, n_tiles_pad), 0)
    tinfo_ref[...] = jnp.where(rid8t == 0, jnp.broadcast_to(te, (SUBLANES, n_tiles_pad)),
                               jnp.broadcast_to(total, (SUBLANES, n_tiles_pad)))


def _moe_plan(meta_i, counts, counts_other, n_tiles_pad, row_tile, second):
    t = meta_i.shape[1]
    tm = MOE_TOK_TILE
    steps = t // tm
    ltri = jnp.tril(jnp.ones((N_EXPERTS, N_EXPERTS), F32), k=-1).astype(BF16)
    return pl.pallas_call(
        functools.partial(_plan_kernel, n_tiles_pad=n_tiles_pad, row_tile=row_tile, second=second),
        grid=(steps,),
        in_specs=[pl.BlockSpec((SUBLANES, tm), lambda i: (0, i)),
                  pl.BlockSpec((N_EXPERTS, LANES), lambda i: (0, 0)),
                  pl.BlockSpec((N_EXPERTS, LANES), lambda i: (0, 0)),
                  pl.BlockSpec((N_EXPERTS, N_EXPERTS), lambda i: (0, 0))],
        out_specs=[pl.BlockSpec((1, SUBLANES, tm), lambda i: (i, 0, 0)),
                   pl.BlockSpec((SUBLANES, n_tiles_pad), lambda i: (0, 0))],
        out_shape=[jax.ShapeDtypeStruct((steps, SUBLANES, tm), I32),
                   jax.ShapeDtypeStruct((SUBLANES, n_tiles_pad), I32)],
        compiler_params=_cparams(("arbitrary",)),
        name="moe_plan",
    )(meta_i, counts, counts_other, ltri)


def _to_token_tiles(ref, val):
    tm, d = val.shape
    for s in range(d // LANES):
        ref[pl.ds(s, tm, stride=SUBLANES), :] = val[:, s * LANES:(s + 1) * LANES]


def _from_token_tiles(ref, tm):
    return jnp.concatenate([ref[pl.ds(s, tm, stride=SUBLANES), :] for s in range(SUBLANES)], axis=-1)


def _tile_rows(idx):
    return pl.ds(pl.multiple_of(idx * SUBLANES, SUBLANES), SUBLANES)


def _row_copy_out(buf_ref, slot, r, dst_hbm, p, sem):
    return pltpu.make_async_copy(buf_ref.at[slot, _tile_rows(r), :], dst_hbm.at[_tile_rows(p), :], sem.at[slot])


def _dispatch_kernel(x_ref, sh_ref, sc_ref, gain_ref, pos_ref, xs_in_ref, xs_ref,
                     buf_ref, idx_ref, sem_ref, isem_ref, *, tm, steps):
    del xs_in_ref
    i = pl.program_id(0)
    slot = i % 2

    def drain(s):
        for _ in range(2):
            pltpu.make_async_copy(buf_ref.at[s], xs_ref.at[pl.ds(0, tm * SUBLANES), :], sem_ref.at[s]).wait()

    @pl.when(i >= 2)
    def _():
        drain(slot)

    _to_token_tiles(buf_ref.at[slot], _norm_mod(x_ref[...], gain_ref[...], sc_ref[...], sh_ref[...]))
    cp = pltpu.make_async_copy(pos_ref.at[0], idx_ref, isem_ref)
    cp.start()
    cp.wait()

    def issue(r, carry):
        _row_copy_out(buf_ref, slot, r, xs_ref, idx_ref[0, r], sem_ref).start()
        _row_copy_out(buf_ref, slot, r, xs_ref, idx_ref[1, r], sem_ref).start()
        return carry
    lax.fori_loop(0, tm, issue, 0, unroll=8)

    @pl.when(i == steps - 1)
    def _():
        drain(slot)
        if steps > 1:
            drain(1 - slot)


def _moe_dispatch(x3, mod3, k_sh, k_sc, gain, pos3, xs0):
    n_seq, length, d = x3.shape
    assert d == SUBLANES * LANES
    sb, lb, steps, nl = _tok_tiling(n_seq, length, MOE_TOK_TILE)
    tm = sb * lb
    return pl.pallas_call(
        functools.partial(_dispatch_kernel, tm=tm, steps=steps),
        grid=(steps,),
        in_specs=[pl.BlockSpec((sb, lb, d), lambda i: (i // nl, i % nl, 0)),
                  pl.BlockSpec((sb, 1, d), lambda i: (i // nl, 0, k_sh)),
                  pl.BlockSpec((sb, 1, d), lambda i: (i // nl, 0, k_sc)),
                  pl.BlockSpec((1, d), lambda i: (0, 0)),
                  pl.BlockSpec((1, SUBLANES, tm), lambda i: (i, 0, 0)),
                  pl.BlockSpec(memory_space=pl.ANY)],
        out_specs=pl.BlockSpec(memory_space=pl.ANY),
        out_shape=jax.ShapeDtypeStruct(xs0.shape, F32),
        scratch_shapes=[pltpu.VMEM((2, tm * SUBLANES, LANES), F32),
                        pltpu.SMEM((SUBLANES, tm), I32),
                        pltpu.SemaphoreType.DMA((2,)),
                        pltpu.SemaphoreType.DMA(())],
        input_output_aliases={5: 0},
        compiler_params=_cparams(("arbitrary",), VMEM_LIMIT),
        name="moe_dispatch",
    )(x3, mod3, mod3, gain.reshape(1, d), pos3, xs0)


def _expert_kernel(te_ref, nt_ref, xs_ref, wg_ref, wu_ref, wd_ref, ys_ref, *, row_tile):
    i = pl.program_id(0)

    @pl.when(i < nt_ref[0])
    def _():
        x = _from_token_tiles(xs_ref, row_tile).astype(BF16)
        hid = _silu(_bdot(x, wg_ref[...])) * _bdot(x, wu_ref[...])
        _to_token_tiles(ys_ref, _bdot(hid, wd_ref[...]))

    @pl.when(i >= nt_ref[0])
    def _():
        ys_ref[...] = jnp.zeros_like(ys_ref)


def _moe_experts(xs, tile_expert, n_tiles, wg, wu, wd, layer, row_tile):
    d, f = wg.shape[-2], wg.shape[-1]
    steps = xs.shape[0] // (row_tile * SUBLANES)
    x_map = lambda i, te, nt: (jnp.minimum(i, nt[0] - 1), 0)
    w_map = lambda i, te, nt: (layer, te[i] // E_PER_GROUP, te[i] % E_PER_GROUP, 0, 0)
    grid_spec = pltpu.PrefetchScalarGridSpec(
        num_scalar_prefetch=2,
        grid=(steps,),
        in_specs=[pl.BlockSpec((row_tile * SUBLANES, LANES), x_map),
                  pl.BlockSpec((None, None, None, d, f), w_map),
                  pl.BlockSpec((None, None, None, d, f), w_map),
                  pl.BlockSpec((None, None, None, f, d), w_map)],
        out_specs=pl.BlockSpec((row_tile * SUBLANES, LANES), lambda i, te, nt: (i, 0)),
    )
    return pl.pallas_call(
        functools.partial(_expert_kernel, row_tile=row_tile),
        grid_spec=grid_spec,
        out_shape=jax.ShapeDtypeStruct(xs.shape, F32),
        compiler_params=_cparams(("arbitrary",), VMEM_LIMIT),
        name="moe_experts",
    )(tile_expert, n_tiles, xs, wg, wu, wd)


def _row_copy_in(src_hbm, p, buf_ref, slot, which, r, sem):
    return pltpu.make_async_copy(src_hbm.at[_tile_rows(p), :], buf_ref.at[slot, which, _tile_rows(r), :],
                                 sem.at[slot])


def _combine_kernel(x_ref, g_ref, gcol_ref, pos_ref, posn_ref, ys_ref, gainf_ref, xo_ref,
                    buf_ref, idx_ref, sem_ref, isem_ref, *, tm, steps, sb, lb, final_norm):
    i = pl.program_id(0)
    slot = i % 2

    def issue(pref, s):
        cp = pltpu.make_async_copy(pref.at[0], idx_ref, isem_ref)
        cp.start()
        cp.wait()

        def body(r, carry):
            _row_copy_in(ys_ref, idx_ref[0, r], buf_ref, s, 0, r, sem_ref).start()
            _row_copy_in(ys_ref, idx_ref[1, r], buf_ref, s, 1, r, sem_ref).start()
            return carry
        lax.fori_loop(0, tm, body, 0, unroll=8)

    @pl.when(i == 0)
    def _():
        issue(pos_ref, slot)

    @pl.when(i + 1 < steps)
    def _():
        issue(posn_ref, 1 - slot)

    for which in range(2):
        pltpu.make_async_copy(ys_ref.at[pl.ds(0, tm * SUBLANES), :], buf_ref.at[slot, which],
                              sem_ref.at[slot]).wait()

    gc = gcol_ref[...]
    y = gc[:, 0:1] * _from_token_tiles(buf_ref.at[slot, 0], tm) \
        + gc[:, 1:2] * _from_token_tiles(buf_ref.at[slot, 1], tm)
    d = y.shape[-1]
    xn = x_ref[...] + g_ref[...] * y.reshape(sb, lb, d)
    if final_norm:
        xn = xn * lax.rsqrt(jnp.mean(xn * xn, axis=-1, keepdims=True) + EPS) * gainf_ref[...]
    xo_ref[...] = xn


def _moe_combine(x3, mod3, k_gate, gates_col, pos3, ys, gain_f, final_norm):
    n_seq, length, d = x3.shape
    sb, lb, steps, nl = _tok_tiling(n_seq, length, MOE_TOK_TILE)
    tm = sb * lb
    return pl.pallas_call(
        functools.partial(_combine_kernel, tm=tm, steps=steps, sb=sb, lb=lb, final_norm=final_norm),
        grid=(steps,),
        in_specs=[pl.BlockSpec((sb, lb, d), lambda i: (i // nl, i % nl, 0)),
                  pl.BlockSpec((sb, 1, d), lambda i: (i // nl, 0, k_gate)),
                  pl.BlockSpec((tm, LANES), lambda i: (i, 0)),
                  pl.BlockSpec((1, SUBLANES, tm), lambda i: (i, 0, 0)),
                  pl.BlockSpec((1, SUBLANES, tm), lambda i: (jnp.minimum(i + 1, steps - 1), 0, 0)),
                  pl.BlockSpec(memory_space=pl.ANY),
                  pl.BlockSpec((1, d), lambda i: (0, 0))],
        out_specs=pl.BlockSpec((sb, lb, d), lambda i: (i // nl, i % nl, 0)),
        out_shape=jax.ShapeDtypeStruct((n_seq, length, d), F32),
        scratch_shapes=[pltpu.VMEM((2, 2, tm * SUBLANES, LANES), F32),
                        pltpu.SMEM((SUBLANES, tm), I32),
                        pltpu.SemaphoreType.DMA((2,)),
                        pltpu.SemaphoreType.DMA(())],
        compiler_params=_cparams(("arbitrary",), VMEM_LIMIT),
        name="moe_combine",
    )(x3, mod3, gates_col, pos3, pos3, ys, gain_f.reshape(1, d))


def _hier_moe_residual(groups, k_sh, k_sc, k_gate, gain, wrt_bf16, bcol, wg, wu, wd, layer, gain_f, final_norm):
    assert len(groups) == 2
    t_all = sum(x3.shape[0] * x3.shape[1] for x3, _ in groups)
    row_tile = MOE_ROW_TILE_BIG if (2 * t_all) // N_EXPERTS >= MOE_ROW_TILE_BIG else MOE_ROW_TILE_SMALL
    max_tiles = -(-(2 * t_all) // row_tile) + N_EXPERTS
    n_tiles_pad = -(-max_tiles // LANES) * LANES
    xs_rows = max_tiles * row_tile * SUBLANES
    routed = [_moe_router(x3, mod3, k_sh, k_sc, gain, wrt_bf16, bcol, xs_rows if gi == 0 else 0)
              for gi, (x3, mod3) in enumerate(groups)]
    xs = routed[0][3]
    plans = [_moe_plan(routed[gi][0], routed[gi][2], routed[1 - gi][2], n_tiles_pad, row_tile, gi == 1)
             for gi in range(2)]
    for (x3, mod3), (pos3, _) in zip(groups, plans):
        xs = _moe_dispatch(x3, mod3, k_sh, k_sc, gain, pos3, xs)
    tinfo = plans[0][1]
    ys = _moe_experts(xs, tinfo[0, :max_tiles], tinfo[1, :1], wg, wu, wd, layer, row_tile)
    return [_moe_combine(x3, mod3, k_gate, r[1], pos3, ys, gain_f, final_norm)
            for (x3, mod3), r, (pos3, _) in zip(groups, routed, plans)]


def _multiplicity(dist):
    m = jnp.zeros(dist.shape, F32)
    for window, dil in BRANCHES:
        m = m + ((dist >= 0) & (dist <= window) & (dist % dil == 0)).astype(F32)
    return m


def _log_mult(dist):
    m = _multiplicity(dist)
    return jnp.where(m > 0, jnp.log(jnp.maximum(m, 1.0)), NEG_BIG)


def _attn_prompt_kernel(q_ref, k_ref, v_ref, nbias_ref, fbias_ref, o_ref, mf_ref, lf_ref, af_ref, *, length):
    blk = ATT_BLK
    nq = length // blk
    near = ATT_NEAR_BLOCKS
    ncls = BRANCHES[-1][1]
    csz = length // ncls
    has_far = nq > near
    scale = (HD_C ** -0.5) * LOG2_E
    nt = lambda a, b: lax.dot_general(a, b, (((1,), (1,)), ((), ())), preferred_element_type=F32)
    mm = lambda a, b: jnp.dot(a, b, preferred_element_type=F32)

    def split_heads(q):
        head0 = lax.broadcasted_iota(I32, q.shape, 1) < HD_C
        return head0, jnp.where(head0, q, 0.0).astype(BF16), jnp.where(head0, 0.0, q).astype(BF16)

    if has_far:
        fbias = fbias_ref[...]
        group = 4
        for r0 in range(0, ncls, group):
            rs = list(range(r0, r0 + group))
            rows = [pl.ds(r, csz, stride=ncls) for r in rs]
            hq = [split_heads(q_ref[0, rw, :] * scale) for rw in rows]
            kr = [k_ref[0, rw, :].astype(BF16) for rw in rows]
            vr = [v_ref[0, rw, :].astype(BF16) for rw in rows]
            s0 = [nt(h[1], k) + fbias for h, k in zip(hq, kr)]
            s1 = [nt(h[2], k) + fbias for h, k in zip(hq, kr)]
            m0 = [jnp.max(s, axis=-1, keepdims=True) for s in s0]
            m1 = [jnp.max(s, axis=-1, keepdims=True) for s in s1]
            p0 = [jnp.exp2(s - m) for s, m in zip(s0, m0)]
            p1 = [jnp.exp2(s - m) for s, m in zip(s1, m1)]
            a0 = [mm(p.astype(BF16), v) for p, v in zip(p0, vr)]
            a1 = [mm(p.astype(BF16), v) for p, v in zip(p1, vr)]
            for i, rw in enumerate(rows):
                head0 = hq[i][0]
                af_ref[rw, :] = jnp.where(head0, a0[i], a1[i])
                mf_ref[rw, :] = jnp.where(head0, m0[i], m1[i])
                lf_ref[rw, :] = jnp.where(head0, jnp.sum(p0[i], axis=-1, keepdims=True),
                                          jnp.sum(p1[i], axis=-1, keepdims=True))

    for qi in range(nq):
        rq = slice(qi * blk, (qi + 1) * blk)
        head0, q0, q1 = split_heads(q_ref[0, rq, :] * scale)
        js = list(range(max(0, qi - near + 1), qi + 1))
        ks = [k_ref[0, kj * blk:(kj + 1) * blk, :].astype(BF16) for kj in js]
        vs = [v_ref[0, kj * blk:(kj + 1) * blk, :].astype(BF16) for kj in js]
        far = has_far and qi >= near

        def fold(x, op):
            parts = [x[:, c * LANES:(c + 1) * LANES] for c in range(blk // LANES)]
            return functools.reduce(op, parts)

        q0d, q1d = q0 * 2, q1 * 2
        mr0 = functools.reduce(jnp.maximum, [fold(nt(q0d, k) * 0.5 + nbias_ref[qi - kj], jnp.maximum)
                                             for k, kj in zip(ks, js)])
        mr1 = functools.reduce(jnp.maximum, [fold(nt(q1d, k) * 0.5 + nbias_ref[qi - kj], jnp.maximum)
                                             for k, kj in zip(ks, js)])
        m0 = jnp.max(mr0, axis=-1, keepdims=True)
        m1 = jnp.max(mr1, axis=-1, keepdims=True)
        if far:
            mf = mf_ref[rq, :]
            mf0, mf1 = mf[:, 0:1], mf[:, HD_C:HD_C + 1]
            m0 = jnp.maximum(m0, mf0)
            m1 = jnp.maximum(m1, mf1)
        lr0 = lr1 = a0 = a1 = None
        for k, v, kj in zip(ks, vs, js):
            bias = nbias_ref[qi - kj]
            p0 = jnp.exp2(nt(q0, k) + bias - m0)
            p1 = jnp.exp2(nt(q1, k) + bias - m1)
            f0, f1 = fold(p0, jnp.add), fold(p1, jnp.add)
            t0, t1 = mm(p0.astype(BF16), v), mm(p1.astype(BF16), v)
            lr0, lr1 = (f0, f1) if lr0 is None else (lr0 + f0, lr1 + f1)
            a0, a1 = (t0, t1) if a0 is None else (a0 + t0, a1 + t1)
        l0 = jnp.sum(lr0, axis=-1, keepdims=True)
        l1 = jnp.sum(lr1, axis=-1, keepdims=True)
        acc = jnp.where(head0, a0, a1)
        den = jnp.where(head0, l0, l1)
        if far:
            w_far = jnp.where(head0, jnp.exp2(mf0 - m0), jnp.exp2(mf1 - m1))
            acc = acc + af_ref[rq, :] * w_far
            den = den + lf_ref[rq, :] * w_far
        o_ref[0, rq, :] = acc / den


def _attn_prompt(q3, k3, v3):
    n_seq, length, d = q3.shape
    pairs = d // LANES
    blk = ATT_BLK
    near = ATT_NEAR_BLOCKS
    ncls = BRANCHES[-1][1]
    csz = length // ncls
    per = blk // ncls
    assert (near - 1) * blk >= BRANCHES[-2][0] and length % blk == 0 and blk % ncls == 0
    a = jnp.arange(blk)
    dist = (jnp.arange(near)[:, None, None] * blk) + a[None, :, None] - a[None, None, :]
    nbias = _log_mult(dist) * jnp.where(_multiplicity(dist) > 0, LOG2_E, 1.0)
    m = jnp.arange(csz)
    far_ok = ((m[:, None] // per - m[None, :] // per) >= near) & \
        ((m[:, None] - m[None, :]) * ncls <= BRANCHES[-1][0])
    fbias = jnp.where(far_ok, 0.0, NEG_BIG).astype(F32)
    return pl.pallas_call(
        functools.partial(_attn_prompt_kernel, length=length),
        grid=(n_seq, pairs),
        in_specs=[pl.BlockSpec((1, length, LANES), lambda n, p: (n, 0, p)),
                  pl.BlockSpec((1, length, LANES), lambda n, p: (n, 0, p)),
                  pl.BlockSpec((1, length, LANES), lambda n, p: (n, 0, p)),
                  pl.BlockSpec((near, blk, blk), lambda n, p: (0, 0, 0)),
                  pl.BlockSpec((csz, csz), lambda n, p: (0, 0))],
        out_specs=pl.BlockSpec((1, length, LANES), lambda n, p: (n, 0, p)),
        out_shape=jax.ShapeDtypeStruct((n_seq, length, d), F32),
        scratch_shapes=[pltpu.VMEM((length, LANES), F32)] * 3,
        compiler_params=_cparams(("arbitrary", "arbitrary"), VMEM_LIMIT),
        name="attn_prompt",
    )(q3, k3, v3, nbias, fbias)


def _attn_sample_kernel(q_ref, kn_ref, vn_ref, kt_ref, vt_ref, bias_ref, biasn_ref, o_ref, *, heads):
    q = q_ref[0] * (HD_C ** -0.5)
    kn = kn_ref[0]
    vn = vn_ref[0]
    bias = bias_ref[...]
    biasn = biasn_ref[...]
    outs = []
    for h in range(heads):
        lo, hi = h * HD_C, (h + 1) * HD_C
        qh = q[:, lo:hi].astype(BF16)
        s = jnp.dot(qh, kt_ref[0, h].astype(BF16), preferred_element_type=F32) + bias
        sn = _bdot_nt(qh, kn[:, lo:hi]) + biasn
        m = jnp.maximum(jnp.max(s, axis=-1, keepdims=True), jnp.max(sn, axis=-1, keepdims=True))
        p = jnp.exp(s - m)
        pn = jnp.exp(sn - m)
        den = jnp.sum(p, axis=-1, keepdims=True) + jnp.sum(pn, axis=-1, keepdims=True)
        o = _bdot_nt(p, vt_ref[0, h]) + _bdot(pn, vn[:, lo:hi])
        outs.append(o / den)
    o_ref[0] = jnp.concatenate(outs, axis=-1)


def _attn_sample(q3, k3, v3, k_cache, v_cache):
    n_seq, t_len, d = q3.shape
    w_buf = k_cache.shape[1]
    kt = jnp.transpose(k_cache, (0, 2, 3, 1))
    vt = jnp.transpose(v_cache, (0, 2, 3, 1))
    heads = SAMPLE_HEADS_PER_STEP
    hw = heads * HD_C
    nj = H_C // heads
    tpos = w_buf + jnp.arange(t_len)
    bias = _log_mult(tpos[:, None] - jnp.arange(w_buf)[None, :])
    biasn = _log_mult(jnp.arange(t_len)[:, None] - jnp.arange(t_len)[None, :])
    return pl.pallas_call(
        functools.partial(_attn_sample_kernel, heads=heads),
        grid=(n_seq, nj),
        in_specs=[pl.BlockSpec((1, t_len, hw), lambda n, j: (n, 0, j)),
                  pl.BlockSpec((1, t_len, hw), lambda n, j: (n, 0, j)),
                  pl.BlockSpec((1, t_len, hw), lambda n, j: (n, 0, j)),
                  pl.BlockSpec((1, heads, HD_C, w_buf), lambda n, j: (n, j, 0, 0)),
                  pl.BlockSpec((1, heads, HD_C, w_buf), lambda n, j: (n, j, 0, 0)),
                  pl.BlockSpec((t_len, w_buf), lambda n, j: (0, 0)),
                  pl.BlockSpec((t_len, t_len), lambda n, j: (0, 0))],
        out_specs=pl.BlockSpec((1, t_len, hw), lambda n, j: (n, 0, j)),
        out_shape=jax.ShapeDtypeStruct((n_seq, t_len, d), F32),
        compiler_params=_cparams(("arbitrary", "arbitrary"), VMEM_LIMIT),
        name="attn_sample",
    )(q3, k3, v3, kt, vt, bias, biasn)


def _pad_hist(hist, n_seq, width):
    k1 = hist.shape[1]
    return jnp.concatenate([jnp.zeros((n_seq, SUBLANES - k1, width), F32), hist.astype(F32)], axis=1)


def _prep_weights(p):
    d = p['w_in_even'].shape[1]
    out = {}
    w_in = p['w_in_even'][0]
    hv = H_A * DV_A
    o_z = QKV_A
    o_a = o_z + hv
    o_g = o_a + 2 * H_A
    ab_pad = jnp.zeros((d, LANES - 2 * H_A), F32)
    out['w_in'] = jnp.concatenate([w_in[:, :o_a], w_in[:, o_a:o_g], ab_pad, w_in[:, o_g:]], axis=1).astype(BF16)
    out['w_out_even'] = p['w_out_even'][0].astype(BF16)
    out['w_qkv'] = p['w_qkv_odd'][0].astype(BF16)
    out['w_out_odd'] = p['w_out_odd'][0].astype(BF16)
    depth = p['w_router_group'].shape[0]
    wrt, bcol = [], []
    for l in range(depth):
        pad_w = jnp.zeros((SUBLANES - N_GROUPS, d), F32)
        wrt.append(jnp.concatenate([p['w_router_group'][l].T, pad_w, p['w_router_expert'][l].T], axis=0).astype(BF16))
        pad_b = jnp.full((SUBLANES - N_GROUPS,), NEG_BIG, F32)
        bcol.append(jnp.concatenate([p['b_router_group'][l], pad_b, p['b_router_expert'][l]]).reshape(-1, 1))
    out.update(wrt=wrt, bcol=bcol, wg=p['w_exp_gate'], wu=p['w_exp_up'], wd=p['w_exp_down'])
    return out


def _layer0_mixers(x3, mod3, a_ssm, a_conv, b_conv, p, w):
    n_seq, length, d = x3.shape
    t = n_seq * length
    hv = H_A * DV_A
    qkv_pre, z, ab, gcx = _norm_mod_matmul(x3, mod3, 0, 1, p['norm1'][0], w['w_in'],
                                           (QKV_A, hv, LANES, 3 * D_B))
    chunk = min(CHUNK_PROMPT, length)
    g_chunks = DELTA_CHUNKS_PER_STEP if length > chunk else DELTA_SEQS_PER_STEP
    qkv_pre3 = qkv_pre.reshape(n_seq, length, QKV_A)
    ub, wm, qd, kd, qk, dec = _delta_prep(qkv_pre3, _pad_hist(a_conv[0], n_seq, QKV_A),
                                          ab.reshape(n_seq, length, LANES), p['conv_a_w'][0],
                                          p['a_log'][0], p['dt_bias'][0], chunk, g_chunks)
    out_a, s_new = _delta_scan(ub, wm, qd, kd, qk, dec, z.reshape(n_seq, length, hv),
                               a_ssm[0].astype(F32), p['o_gain_a'][0], chunk)
    x3, bx_tail = _even_out(gcx.reshape(n_seq, length, 3 * D_B), _pad_hist(b_conv[0], n_seq, D_B),
                            out_a.reshape(t, hv), x3, mod3, 2, p['conv_b_w'][0], w['w_out_even'])
    new_aconv = jnp.concatenate([a_conv[0].astype(F32), qkv_pre3], axis=1)[:, -(CONV_A - 1):]
    new_bconv = jnp.concatenate([b_conv[0].astype(F32), bx_tail], axis=1)[:, -(CONV_B - 1):]
    return x3, s_new[None], new_aconv[None], new_bconv[None]


def _layer1_attention(x3, mod3, kv_k, kv_v, p, w, sample):
    n_seq, length, d = x3.shape
    t = n_seq * length
    if sample:
        q_new, k_new, v_new = (a.reshape(n_seq, length, d) for a in
                               _norm_mod_matmul(x3, mod3, 0, 1, p['norm1'][1], w['w_qkv'], (d, d, d)))
        attn = _attn_sample(q_new, k_new, v_new, kv_k[0], kv_v[0])
        new_k = k_new.reshape(n_seq, length, H_C, HD_C)
        new_v = v_new.reshape(n_seq, length, H_C, HD_C)
    else:
        q_new, k_new, v_new, k_t, v_t = _norm_mod_matmul(x3, mod3, 0, 1, p['norm1'][1], w['w_qkv'],
                                                         (d, d, d), transposed=(1, 2))
        q_new, k_new, v_new = (a.reshape(n_seq, length, d) for a in (q_new, k_new, v_new))
        attn = _attn_prompt(q_new, k_new, v_new)
        keep = min(BRANCHES[-1][0], length)
        new_k = jnp.transpose(k_t[:, :, length - keep:].reshape(n_seq, H_C, HD_C, keep), (0, 3, 1, 2))
        new_v = jnp.transpose(v_t[:, :, length - keep:].reshape(n_seq, H_C, HD_C, keep), (0, 3, 1, 2))
    x3 = _proj_residual(attn.reshape(t, d), x3, mod3, 2, w['w_out_odd'])
    return x3, new_k[None], new_v[None]


def kernel(x_prompt, x_sample, state_a_ssm, state_a_conv, state_b_conv, cache_c_k, cache_c_v, c_prompt, c_sample, w_ada, b_ada, norm1, norm2, norm_f, w_in_even, conv_a_w, a_log, dt_bias, o_gain_a, conv_b_w, w_out_even, w_qkv_odd, w_out_odd, w_router_group, b_router_group, w_router_expert, b_router_expert, w_exp_gate, w_exp_up, w_exp_down):
    p = dict(norm1=norm1, norm2=norm2, norm_f=norm_f, w_in_even=w_in_even, conv_a_w=conv_a_w,
             a_log=a_log, dt_bias=dt_bias, o_gain_a=o_gain_a, conv_b_w=conv_b_w, w_out_even=w_out_even,
             w_qkv_odd=w_qkv_odd, w_out_odd=w_out_odd, w_router_group=w_router_group,
             b_router_group=b_router_group, w_router_expert=w_router_expert,
             b_router_expert=b_router_expert, w_exp_gate=w_exp_gate, w_exp_up=w_exp_up,
             w_exp_down=w_exp_down)
    w = _prep_weights(p)
    nb = x_prompt.shape[0]
    ns = x_sample.shape[0]
    mod_all = _ada_mod(jnp.concatenate([c_prompt, c_sample], axis=0), w_ada, b_ada)
    depth = w_ada.shape[0]
    mod_p = [mod_all[l, :nb] for l in range(depth)]
    mod_s = [mod_all[l, nb:] for l in range(depth)]
    n_even = state_a_ssm.shape[0]
    zero_ssm = jnp.zeros((n_even, nb, H_A, DK_A, DV_A), F32)
    zero_aconv = jnp.zeros((n_even, nb, CONV_A - 1, QKV_A), F32)
    zero_bconv = jnp.zeros((n_even, nb, CONV_B - 1, D_B), F32)
    d = x_prompt.shape[-1]
    m3_p = [m.reshape(nb, 1, 6 * d) for m in mod_p]
    m3_s = [m.reshape(ns, 1, 6 * d) for m in mod_s]

    def moe(xp, xs_, layer, final_norm):
        return _hier_moe_residual([(xp, m3_p[layer]), (xs_, m3_s[layer])], 3, 4, 5, norm2[layer],
                                  w['wrt'][layer], w['bcol'][layer], w['wg'], w['wu'], w['wd'], layer,
                                  norm_f, final_norm)

    xp, ssm_p, aconv_p, bconv_p = _layer0_mixers(x_prompt, m3_p[0], zero_ssm, zero_aconv, zero_bconv, p, w)
    xs_, ssm_s, aconv_s, bconv_s = _layer0_mixers(x_sample, m3_s[0], state_a_ssm, state_a_conv,
                                                  state_b_conv, p, w)
    xp, xs_ = moe(xp, xs_, 0, False)
    xp, k_p, v_p = _layer1_attention(xp, m3_p[1], None, None, p, w, False)
    xs_, k_s, v_s = _layer1_attention(xs_, m3_s[1], cache_c_k, cache_c_v, p, w, True)
    y_p, y_s = moe(xp, xs_, 1, True)
    return (y_p, y_s, ssm_p, ssm_s, aconv_p, aconv_s, bconv_p, bconv_s, k_p, k_s, v_p, v_s)
```

```python
import functools
import math

import jax
import jax.numpy as jnp
from jax import lax
from jax.experimental import pallas as pl
from jax.experimental.pallas import tpu as pltpu

F32 = jnp.float32
BF16 = jnp.bfloat16
I32 = jnp.int32

EPS = 1e-6
NEG_BIG = -1e30

LANES = 128
SUBLANES = 8
VMEM_BYTES_V7X = 64 * 1024 * 1024
VMEM_LIMIT = 56 * 1024 * 1024

H_A, DK_A, DV_A = 4, 128, 128
CONV_A, CONV_B = 4, 3
QKV_A = H_A * (2 * DK_A + DV_A)
D_B = 512
H_C, HD_C = 16, 64
BRANCHES = ((128, 1), (512, 4), (2048, 16))
N_GROUPS, E_PER_GROUP = 4, 8
N_EXPERTS = N_GROUPS * E_PER_GROUP
CHUNK_PROMPT = 64
DELTA_CHUNKS_PER_STEP = 4
DELTA_SEQS_PER_STEP = 8

TOK_TILE = 512
MOE_TOK_TILE = 1024
MOE_ROW_TILE_BIG = 512
MOE_ROW_TILE_SMALL = 128
ATT_BLK = 256
ATT_NEAR_BLOCKS = 3
LOG2_E = math.log2(math.e)
SEQ_BLOCK_SCAN = 8
SAMPLE_HEADS_PER_STEP = 16


def _cparams(sem, vmem=None):
    return pltpu.CompilerParams(dimension_semantics=sem, vmem_limit_bytes=vmem)


def _silu(x):
    return x * jax.nn.sigmoid(x)


def _bdot(a, b):
    return jnp.dot(a.astype(BF16), b.astype(BF16), preferred_element_type=F32)


def _bdot_nt(a, b):
    return lax.dot_general(a.astype(BF16), b.astype(BF16), (((1,), (1,)), ((), ())),
                           preferred_element_type=F32)


def _bdot_tn(a, b):
    return lax.dot_general(a.astype(BF16), b.astype(BF16), (((0,), (0,)), ((), ())),
                           preferred_element_type=F32)


def _split_bf16(x):
    hi = x.astype(BF16)
    return hi, (x - hi.astype(F32)).astype(BF16)


def _dot3(a, b):
    ah, al = _split_bf16(a)
    bh, bl = _split_bf16(b)
    dot = functools.partial(jnp.dot, preferred_element_type=F32)
    return dot(ah, bh) + (dot(ah, bl) + dot(al, bh))


def _fdot(a, b):
    return jnp.dot(a, b, preferred_element_type=F32, precision=lax.Precision.HIGHEST)


def _fdot_nt(a, b):
    return lax.dot_general(a, b, (((1,), (1,)), ((), ())), preferred_element_type=F32,
                           precision=lax.Precision.HIGHEST)


def _tok_tiling(n_seq, length, tile):
    assert (n_seq * length) % tile == 0 and (length % tile == 0 or tile % length == 0)
    if length >= tile:
        sb, lb = 1, tile
    else:
        sb, lb = tile // length, length
    nl = length // lb
    return sb, lb, (n_seq // sb) * nl, nl


def _norm_mod(x, gain, sc, sh):
    sb, lb, d = x.shape
    y = x * lax.rsqrt(jnp.mean(x * x, axis=-1, keepdims=True) + EPS) * gain
    h = y * (1.0 + sc) + sh
    return h.reshape(sb * lb, d)


def _ada_kernel(c_ref, w_ref, b_ref, o_ref):
    c = _silu(c_ref[...])
    o_ref[0] = _bdot(c, w_ref[0]) + b_ref[0]


def _ada_mod(c_all, w_ada, b_ada):
    r, d = c_all.shape
    depth, _, n6 = w_ada.shape
    tn = 1536
    return pl.pallas_call(
        _ada_kernel,
        grid=(depth, n6 // tn),
        in_specs=[pl.BlockSpec((r, d), lambda l, j: (0, 0)),
                  pl.BlockSpec((1, d, tn), lambda l, j: (l, 0, j)),
                  pl.BlockSpec((1, 1, tn), lambda l, j: (l, 0, j))],
        out_specs=pl.BlockSpec((1, r, tn), lambda l, j: (l, 0, j)),
        out_shape=jax.ShapeDtypeStruct((depth, r, n6), F32),
        compiler_params=_cparams(("arbitrary", "arbitrary"), VMEM_LIMIT),
        name="ada_mod",
    )(c_all, w_ada, b_ada.reshape(depth, 1, n6))


def _nmm_kernel(x_ref, sh_ref, sc_ref, gain_ref, w_ref, *o_refs, splits, transposed):
    h = _norm_mod(x_ref[...], gain_ref[...], sc_ref[...], sh_ref[...]).astype(BF16)
    off = 0
    vals = []
    for o_ref, n in zip(o_refs, splits):
        vals.append(jnp.dot(h, w_ref[:, off:off + n], preferred_element_type=F32))
        o_ref[...] = vals[-1]
        off += n
    for o_ref, j in zip(o_refs[len(splits):], transposed):
        o_ref[0] = vals[j].T


def _norm_mod_matmul(x3, mod3, k_sh, k_sc, gain, w_bf16, splits, transposed=()):
    n_seq, length, d = x3.shape
    sb, lb, steps, nl = _tok_tiling(n_seq, length, TOK_TILE)
    rows = sb * lb
    t = n_seq * length
    n_out = w_bf16.shape[1]
    assert sum(splits) == n_out and (not transposed or sb == 1)
    t_specs = [pl.BlockSpec((1, splits[j], lb), lambda i: (i // nl, 0, i % nl)) for j in transposed]
    t_shapes = [jax.ShapeDtypeStruct((n_seq, splits[j], length), F32) for j in transposed]
    return pl.pallas_call(
        functools.partial(_nmm_kernel, splits=splits, transposed=transposed),
        grid=(steps,),
        in_specs=[pl.BlockSpec((sb, lb, d), lambda i: (i // nl, i % nl, 0)),
                  pl.BlockSpec((sb, 1, d), lambda i: (i // nl, 0, k_sh)),
                  pl.BlockSpec((sb, 1, d), lambda i: (i // nl, 0, k_sc)),
                  pl.BlockSpec((1, d), lambda i: (0, 0)),
                  pl.BlockSpec((d, n_out), lambda i: (0, 0))],
        out_specs=[pl.BlockSpec((rows, n), lambda i: (i, 0)) for n in splits] + t_specs,
        out_shape=[jax.ShapeDtypeStruct((t, n), F32) for n in splits] + t_shapes,
        compiler_params=_cparams(("arbitrary",), VMEM_LIMIT),
        name="norm_mod_matmul",
    )(x3, mod3, mod3, gain.reshape(1, d), w_bf16)


def _delta_prep_kernel(qkv_ref, halo_ref, hist_ref, ab_ref, cw_ref, al_ref, dtb_ref, lt_ref, sel_ref,
                       ub_ref, wm_ref, qd_ref, kd_ref, qk_ref, dec_ref,
                       *, g_chunks, chunk, chunks_are_seqs, chunks_per_seq):
    c = chunk
    step = pl.program_id(0)
    cw = cw_ref[...]
    row = lax.broadcasted_iota(I32, (c, c), 0)
    col = lax.broadcasted_iota(I32, (c, c), 1)
    lower_incl = row >= col
    strict = row > col
    eye = (row == col).astype(F32)
    n_fac = max(int(math.ceil(math.log2(c))), 1)

    qkvs, abs_ = [], []
    for g in range(g_chunks):
        cur = qkv_ref[g] if chunks_are_seqs else qkv_ref[0, g * c:(g + 1) * c, :]
        if chunks_are_seqs:
            prev = hist_ref[g]
        elif g == 0:
            first = (step % (chunks_per_seq // g_chunks)) == 0
            prev = jnp.where(first, hist_ref[0], halo_ref[0])
        else:
            prev = qkv_ref[0, g * c - SUBLANES:g * c, :]
        row8 = lax.broadcasted_iota(I32, (SUBLANES, cur.shape[1]), 0)
        conv = cur * cw[CONV_A - 1:CONV_A, :]
        for dback in range(1, CONV_A):
            rolled = pltpu.roll(cur, dback, 0)
            top = jnp.where(row8 < dback, pltpu.roll(prev, dback, 0), rolled[0:SUBLANES])
            shifted = top if c == SUBLANES else jnp.concatenate([top, rolled[SUBLANES:]], axis=0)
            conv = conv + shifted * cw[CONV_A - 1 - dback:CONV_A - dback, :]
        qkvs.append(_silu(conv))
        abs_.append(ab_ref[g] if chunks_are_seqs else ab_ref[0, g * c:(g + 1) * c, :])

    g_alls = [-jnp.exp(al_ref[...]) * jax.nn.softplus(ab + dtb_ref[...]) for ab in abs_]
    sigs = [jax.nn.sigmoid(ab) for ab in abs_]
    gcum_alls = [_fdot(lt_ref[...], ga) for ga in g_alls]
    gcum_rows = [_fdot_nt(sel_ref[...], gc) for gc in gcum_alls]

    chains = [(g, h) for g in range(g_chunks) for h in range(H_A)]
    qs, ks, vs, betas, gcs, gammas, egs = [], [], [], [], [], [], []
    for g, h in chains:
        qkv = qkvs[g]
        q = qkv[:, h * DK_A:(h + 1) * DK_A]
        k = qkv[:, H_A * DK_A + h * DK_A:H_A * DK_A + (h + 1) * DK_A]
        v = qkv[:, 2 * H_A * DK_A + h * DV_A:2 * H_A * DK_A + (h + 1) * DV_A]
        qs.append(q * lax.rsqrt(jnp.sum(q * q, axis=-1, keepdims=True) + EPS) * (DK_A ** -0.5))
        ks.append(k * lax.rsqrt(jnp.sum(k * k, axis=-1, keepdims=True) + EPS))
        vs.append(v)
        betas.append(sigs[g][:, H_A + h:H_A + h + 1])
        gc = gcum_alls[g][:, h:h + 1]
        gr = gcum_rows[g][h:h + 1, :]
        gcs.append(gc)
        gammas.append(jnp.exp(jnp.where(lower_incl, gc - gr, NEG_BIG)))
        egs.append(jnp.exp(gc))

    kq = [_bdot_nt(jnp.concatenate([k, q], axis=0), k) for k, q in zip(ks, qs)]
    a_mats = [jnp.where(strict, b * x[0:c] * gm, 0.0) for b, x, gm in zip(betas, kq, gammas)]
    qkms = [x[c:2 * c] * gm for x, gm in zip(kq, gammas)]
    m_pows = [-a for a in a_mats]
    t_invs = [eye + m for m in m_pows]
    for _ in range(n_fac - 1):
        m_pows = [_bdot(m, m) for m in m_pows]
        t_invs = [t + _bdot(t, m) for t, m in zip(t_invs, m_pows)]
    resids = [eye - t - _dot3(a, t) for a, t in zip(a_mats, t_invs)]
    t_invs = [t + _bdot(t, r) for t, r in zip(t_invs, resids)]
    rhss = [jnp.concatenate([v * b, k * (b * eg)], axis=-1) for v, k, b, eg in zip(vs, ks, betas, egs)]
    sols = [rhs + _dot3(t - eye, rhs) for t, rhs in zip(t_invs, rhss)]

    for g in range(g_chunks):
        idx = [i for i, (gg, _) in enumerate(chains) if gg == g]
        g_last = [gcum_alls[g][c - 1:c, h:h + 1] for h in range(H_A)]

        def put(ref, parts):
            val = jnp.concatenate(parts, axis=-1).astype(ref.dtype)
            if chunks_are_seqs:
                ref[g] = val
            else:
                ref[0, g * c:(g + 1) * c, :] = val

        put(ub_ref, [sols[i][:, :DV_A] for i in idx])
        put(wm_ref, [sols[i][:, DV_A:] for i in idx])
        put(qd_ref, [qs[i] * egs[i] for i in idx])
        put(kd_ref, [ks[i] * jnp.exp(g_last[h] - gcs[i]) for h, i in enumerate(idx)])
        put(qk_ref, [qkms[i] for i in idx])
        dec_ref[g] = jnp.concatenate([jnp.broadcast_to(jnp.exp(gl), (SUBLANES, DV_A)) for gl in g_last], axis=-1)


def _delta_prep(qkv_pre3, hist8, ab3, conv_w, a_log, dt_bias, chunk, g_chunks):
    n_seq, length, w = qkv_pre3.shape
    c = chunk
    chunks_are_seqs = (length == c)
    nc = length // c
    if chunks_are_seqs:
        steps = n_seq // g_chunks
        blk = (g_chunks, c, w)
        x_map = lambda i: (i, 0, 0)
        halo_map = lambda i: (i, 0, 0)
        hist_spec = pl.BlockSpec((g_chunks, SUBLANES, w), lambda i: (i, 0, 0))
        ab_spec = pl.BlockSpec((g_chunks, c, LANES), lambda i: (i, 0, 0))
        out_map = lambda i: (i, 0, 0)
        out_rows = (g_chunks, c)
    else:
        rows = g_chunks * c
        spb = nc // g_chunks
        steps = n_seq * spb
        blk = (1, rows, w)
        x_map = lambda i: (i // spb, i % spb, 0)
        halo_map = lambda i: (i // spb, jnp.maximum((i % spb) * (rows // SUBLANES) - 1, 0), 0)
        hist_spec = pl.BlockSpec((1, SUBLANES, w), lambda i: (i // spb, 0, 0))
        ab_spec = pl.BlockSpec((1, rows, LANES), lambda i: (i // spb, i % spb, 0))
        out_map = x_map
        out_rows = (1, rows)
    op_dtype = BF16 if c % (2 * SUBLANES) == 0 else F32
    lt = jnp.tril(jnp.ones((c, c), F32))
    sel = jnp.eye(SUBLANES, LANES, dtype=F32)
    hv = H_A * DV_A
    al = jnp.zeros((1, LANES), F32).at[0, :H_A].set(a_log)
    dtb = jnp.zeros((1, LANES), F32).at[0, :H_A].set(dt_bias)
    kern = functools.partial(_delta_prep_kernel, g_chunks=g_chunks, chunk=c,
                             chunks_are_seqs=chunks_are_seqs, chunks_per_seq=nc)
    big = lambda width: pl.BlockSpec(out_rows + (width,), out_map)
    return pl.pallas_call(
        kern,
        grid=(steps,),
        in_specs=[pl.BlockSpec(blk, x_map),
                  pl.BlockSpec((1, SUBLANES, w), halo_map) if not chunks_are_seqs
                  else pl.BlockSpec((g_chunks, SUBLANES, w), halo_map),
                  hist_spec, ab_spec,
                  pl.BlockSpec((CONV_A, w), lambda i: (0, 0)),
                  pl.BlockSpec((1, LANES), lambda i: (0, 0)),
                  pl.BlockSpec((1, LANES), lambda i: (0, 0)),
                  pl.BlockSpec((c, c), lambda i: (0, 0)),
                  pl.BlockSpec((SUBLANES, LANES), lambda i: (0, 0))],
        out_specs=[big(hv), big(hv), big(hv), big(hv), big(H_A * c),
                   pl.BlockSpec((g_chunks, SUBLANES, hv), lambda i: (i, 0, 0))],
        out_shape=[jax.ShapeDtypeStruct((n_seq, length, hv), F32)]
        + [jax.ShapeDtypeStruct((n_seq, length, hv), op_dtype)] * 3
        + [jax.ShapeDtypeStruct((n_seq, length, H_A * c), op_dtype),
           jax.ShapeDtypeStruct((n_seq * nc, SUBLANES, hv), F32)],
        compiler_params=_cparams(("arbitrary",), VMEM_LIMIT),
        name="delta_prep",
    )(qkv_pre3, hist8 if chunks_are_seqs else qkv_pre3, hist8, ab3, conv_w, al, dtb, lt, sel)


def _delta_scan_kernel(ub_ref, wm_ref, qd_ref, kd_ref, qk_ref, dec_ref, z_ref, s0_ref, og_ref,
                       o_ref, sn_ref, s_ref, *, nb, chunk):
    c = chunk
    j = pl.program_id(1)

    @pl.when(j == 0)
    def _():
        s_ref[...] = s0_ref[...]

    og = og_ref[...]
    chains = [(b, h) for b in range(nb) for h in range(H_A)]
    sl = lambda h: slice(h * DV_A, (h + 1) * DV_A)
    states = [s_ref[b, h] for b, h in chains]
    ws = [_bdot(jnp.concatenate([wm_ref[b, :, sl(h)], qd_ref[b, :, sl(h)]], axis=0), s)
          for (b, h), s in zip(chains, states)]
    us = [ub_ref[b, :, sl(h)] - x[0:c] for (b, h), x in zip(chains, ws)]
    os_ = [x[c:2 * c] + _bdot(qk_ref[b, :, h * c:(h + 1) * c], u) for (b, h), x, u in zip(chains, ws, us)]
    for (b, h), s, u in zip(chains, states, us):
        s_ref[b, h] = s * dec_ref[b, 0, 0:1, sl(h)] + _bdot_tn(kd_ref[b, :, sl(h)], u)
    for b in range(nb):
        outs = []
        for h in range(H_A):
            o = os_[b * H_A + h]
            on = o * lax.rsqrt(jnp.mean(o * o, axis=-1, keepdims=True) + EPS) * og
            outs.append(on * _silu(z_ref[b, :, sl(h)]))
        o_ref[b] = jnp.concatenate(outs, axis=-1).astype(o_ref.dtype)
    sn_ref[...] = s_ref[...]


def _delta_scan(ub, wm, qd, kd, qk, dec, z3, s0, o_gain, chunk):
    n_seq, length, hv = ub.shape
    c = chunk
    nc = length // c
    nb = SEQ_BLOCK_SCAN
    dec4 = dec.reshape(n_seq, nc, SUBLANES, hv)
    tok = lambda width: pl.BlockSpec((nb, c, width), lambda b, j: (b, j, 0))
    st = pl.BlockSpec((nb, H_A, DK_A, DV_A), lambda b, j: (b, 0, 0, 0))
    return pl.pallas_call(
        functools.partial(_delta_scan_kernel, nb=nb, chunk=c),
        grid=(n_seq // nb, nc),
        in_specs=[tok(hv), tok(hv), tok(hv), tok(hv), tok(H_A * c),
                  pl.BlockSpec((nb, 1, SUBLANES, hv), lambda b, j: (b, j, 0, 0)),
                  tok(hv), st, pl.BlockSpec((1, DV_A), lambda b, j: (0, 0))],
        out_specs=[tok(hv), st],
        out_shape=[jax.ShapeDtypeStruct((n_seq, length, hv), BF16 if c % (2 * SUBLANES) == 0 else F32),
                   jax.ShapeDtypeStruct((n_seq, H_A, DK_A, DV_A), F32)],
        scratch_shapes=[pltpu.VMEM((nb, H_A, DK_A, DV_A), F32)],
        compiler_params=_cparams(("arbitrary", "arbitrary"), VMEM_LIMIT),
        name="delta_scan",
    )(ub, wm, qd, kd, qk, dec4, z3, s0, o_gain.reshape(1, DV_A))


def _even_out_kernel(gcx_ref, halo_ref, hist_ref, oa_ref, x_ref, g1_ref, cw_ref, w_ref,
                     xo_ref, tail_ref, ext_ref, *, sb, lb, nl):
    step = pl.program_id(0)
    gcx = gcx_ref[...]
    b_gate = gcx[:, :, 0:D_B]
    bx = gcx[:, :, D_B:2 * D_B] * gcx[:, :, 2 * D_B:3 * D_B]
    if nl == 1:
        prev = hist_ref[...]
    else:
        hb = halo_ref[...]
        first = (step % nl) == 0
        prev = jnp.where(first, hist_ref[...], hb[:, :, D_B:2 * D_B] * hb[:, :, 2 * D_B:3 * D_B])
    ext_ref[:, 0:SUBLANES, :] = prev
    ext_ref[:, SUBLANES:SUBLANES + lb, :] = bx
    cw = cw_ref[...]
    conv = ext_ref[:, pl.ds(SUBLANES - (CONV_B - 1), lb), :] * cw[0:1, :]
    for j in range(1, CONV_B):
        conv = conv + ext_ref[:, pl.ds(SUBLANES - (CONV_B - 1) + j, lb), :] * cw[j:j + 1, :]
    out_b = (b_gate * conv).reshape(sb * lb, D_B)
    hv = H_A * DV_A
    y = _bdot(oa_ref[...], w_ref[0:hv, :]) + _bdot(out_b, w_ref[hv:hv + D_B, :])
    d = y.shape[-1]
    xo_ref[...] = x_ref[...] + g1_ref[...] * y.reshape(sb, lb, d)
    tail_ref[...] = bx[:, lb - SUBLANES:lb, :]


def _even_out(gcx3, hist8, out_a2, x3, mod3, k_gate, conv_w, w_out_bf16):
    n_seq, length, d = x3.shape
    sb, lb, steps, nl = _tok_tiling(n_seq, length, TOK_TILE)
    rows = sb * lb
    w3 = gcx3.shape[-1]
    hv = H_A * DV_A
    halo_map = lambda i: (i // nl, jnp.maximum((i % nl) * (lb // SUBLANES) - 1, 0), 0)
    return pl.pallas_call(
        functools.partial(_even_out_kernel, sb=sb, lb=lb, nl=nl),
        grid=(steps,),
        in_specs=[pl.BlockSpec((sb, lb, w3), lambda i: (i // nl, i % nl, 0)),
                  pl.BlockSpec((sb, SUBLANES, w3), halo_map if nl > 1 else (lambda i: (i, 0, 0))),
                  pl.BlockSpec((sb, SUBLANES, D_B), lambda i: (i // nl, 0, 0)),
                  pl.BlockSpec((rows, hv), lambda i: (i, 0)),
                  pl.BlockSpec((sb, lb, d), lambda i: (i // nl, i % nl, 0)),
                  pl.BlockSpec((sb, 1, d), lambda i: (i // nl, 0, k_gate)),
                  pl.BlockSpec((CONV_B, D_B), lambda i: (0, 0)),
                  pl.BlockSpec((hv + D_B, d), lambda i: (0, 0))],
        out_specs=[pl.BlockSpec((sb, lb, d), lambda i: (i // nl, i % nl, 0)),
                   pl.BlockSpec((sb, SUBLANES, D_B), lambda i: (i // nl, 0, 0))],
        out_shape=[jax.ShapeDtypeStruct((n_seq, length, d), F32),
                   jax.ShapeDtypeStruct((n_seq, SUBLANES, D_B), F32)],
        scratch_shapes=[pltpu.VMEM((sb, SUBLANES + lb, D_B), F32)],
        compiler_params=_cparams(("arbitrary",), VMEM_LIMIT),
        name="even_out",
    )(gcx3, gcx3, hist8, out_a2, x3, mod3, conv_w, w_out_bf16)


def _proj_res_kernel(a_ref, x_ref, g_ref, w_ref, xo_ref, *, sb, lb):
    y = _bdot(a_ref[...], w_ref[...])
    xo_ref[...] = x_ref[...] + g_ref[...] * y.reshape(sb, lb, y.shape[-1])


def _proj_residual(a2, x3, mod3, k_gate, w_bf16):
    n_seq, length, d = x3.shape
    sb, lb, steps, nl = _tok_tiling(n_seq, length, TOK_TILE)
    rows = sb * lb
    ka = a2.shape[-1]
    return pl.pallas_call(
        functools.partial(_proj_res_kernel, sb=sb, lb=lb),
        grid=(steps,),
        in_specs=[pl.BlockSpec((rows, ka), lambda i: (i, 0)),
                  pl.BlockSpec((sb, lb, d), lambda i: (i // nl, i % nl, 0)),
                  pl.BlockSpec((sb, 1, d), lambda i: (i // nl, 0, k_gate)),
                  pl.BlockSpec((ka, d), lambda i: (0, 0))],
        out_specs=pl.BlockSpec((sb, lb, d), lambda i: (i // nl, i % nl, 0)),
        out_shape=jax.ShapeDtypeStruct((n_seq, length, d), F32),
        compiler_params=_cparams(("arbitrary",), VMEM_LIMIT),
        name="proj_residual",
    )(a2, x3, mod3, w_bf16)


def _router_kernel(x_ref, sh_ref, sc_ref, gain_ref, wrt_ref, bcol_ref, utri_ref,
                   mi_ref, gcol_ref, cnt_ref, *rest):
    carry_ref = rest[-1]
    i = pl.program_id(0)

    @pl.when(i == 0)
    def _():
        carry_ref[...] = jnp.zeros_like(carry_ref)

    if len(rest) == 2:
        rest[0][...] = jnp.zeros_like(rest[0])

    h2 = _norm_mod(x_ref[...], gain_ref[...], sc_ref[...], sh_ref[...])
    tm = h2.shape[0]
    logits = _bdot_nt(wrt_ref[...], h2) + bcol_ref[...]
    lg = logits[0:SUBLANES]
    e = jnp.exp(lg - jnp.max(lg, axis=0, keepdims=True))
    pg = e / jnp.sum(e, axis=0, keepdims=True)
    p_top = jnp.max(pg, axis=0, keepdims=True)
    rid8 = lax.broadcasted_iota(I32, (SUBLANES, tm), 0)
    g_top = jnp.min(jnp.where(pg == p_top, rid8, SUBLANES), axis=0, keepdims=True)
    le = logits[SUBLANES:SUBLANES + N_EXPERTS]
    sel = jnp.zeros((E_PER_GROUP, tm), F32)
    for gi in range(N_GROUPS):
        sel = sel + jnp.where(g_top == gi, le[gi * E_PER_GROUP:(gi + 1) * E_PER_GROUP], 0.0)
    e2 = jnp.exp(sel - jnp.max(sel, axis=0, keepdims=True))
    p_in = e2 / jnp.sum(e2, axis=0, keepdims=True)
    w_a = jnp.max(p_in, axis=0, keepdims=True)
    i_a = jnp.min(jnp.where(p_in == w_a, rid8, SUBLANES), axis=0, keepdims=True)
    p_rest = jnp.where(rid8 == i_a, -1.0, p_in)
    w_b = jnp.max(p_rest, axis=0, keepdims=True)
    i_b = jnp.min(jnp.where(p_rest == w_b, rid8, SUBLANES), axis=0, keepdims=True)
    den = w_a + w_b
    gate1 = p_top * (w_a / den)
    gate2 = p_top * (w_b / den)
    ex1 = g_top * E_PER_GROUP + i_a
    ex2 = g_top * E_PER_GROUP + i_b

    rid32 = lax.broadcasted_iota(I32, (N_EXPERTS, tm), 0)
    oh1 = rid32 == ex1
    oh2 = rid32 == ex2
    ohc = jnp.where(oh1, 1.0, jnp.where(oh2, 1.0, 0.0))
    cum = _bdot(ohc, utri_ref[...])
    carry = carry_ref[...]
    base = cum - ohc + carry[:, 0:1]
    r1 = jnp.sum(jnp.where(oh1, base, 0.0), axis=0, keepdims=True)
    r2 = jnp.sum(jnp.where(oh2, base, 0.0), axis=0, keepdims=True)
    new_carry = carry + cum[:, tm - 1:tm]
    carry_ref[...] = new_carry
    cnt_ref[...] = new_carry

    bro = lambda v: jnp.broadcast_to(v, (SUBLANES, tm))
    mi_ref[...] = jnp.where(rid8 == 0, bro(ex1),
                            jnp.where(rid8 == 1, bro(ex2),
                                      jnp.where(rid8 == 2, bro(r1.astype(I32)),
                                                jnp.where(rid8 == 3, bro(r2.astype(I32)), 0))))
    rid128 = lax.broadcasted_iota(I32, (LANES, tm), 0)
    g128 = jnp.where(rid128 == 0, jnp.broadcast_to(gate1, (LANES, tm)),
                     jnp.where(rid128 == 1, jnp.broadcast_to(gate2, (LANES, tm)), 0.0))
    gcol_ref[...] = g128.T


def _moe_router(x3, mod3, k_sh, k_sc, gain, wrt_bf16, bcol, xs_rows):
    n_seq, length, d = x3.shape
    sb, lb, steps, nl = _tok_tiling(n_seq, length, TOK_TILE)
    tm = sb * lb
    t = n_seq * length
    utri = jnp.triu(jnp.ones((tm, tm), F32)).astype(BF16)
    r_rows = wrt_bf16.shape[0]
    assert xs_rows % (steps * SUBLANES) == 0
    zrows = xs_rows // steps
    z_specs = [pl.BlockSpec((zrows, LANES), lambda i: (i, 0))] if xs_rows else []
    z_shapes = [jax.ShapeDtypeStruct((xs_rows, LANES), F32)] if xs_rows else []
    return pl.pallas_call(
        _router_kernel,
        grid=(steps,),
        in_specs=[pl.BlockSpec((sb, lb, d), lambda i: (i // nl, i % nl, 0)),
                  pl.BlockSpec((sb, 1, d), lambda i: (i // nl, 0, k_sh)),
                  pl.BlockSpec((sb, 1, d), lambda i: (i // nl, 0, k_sc)),
                  pl.BlockSpec((1, d), lambda i: (0, 0)),
                  pl.BlockSpec((r_rows, d), lambda i: (0, 0)),
                  pl.BlockSpec((r_rows, 1), lambda i: (0, 0)),
                  pl.BlockSpec((tm, tm), lambda i: (0, 0))],
        out_specs=[pl.BlockSpec((SUBLANES, tm), lambda i: (0, i)),
                   pl.BlockSpec((tm, LANES), lambda i: (i, 0)),
                   pl.BlockSpec((N_EXPERTS, LANES), lambda i: (0, 0))] + z_specs,
        out_shape=[jax.ShapeDtypeStruct((SUBLANES, t), I32),
                   jax.ShapeDtypeStruct((t, LANES), F32),
                   jax.ShapeDtypeStruct((N_EXPERTS, LANES), F32)] + z_shapes,
        scratch_shapes=[pltpu.VMEM((N_EXPERTS, LANES), F32)],
        compiler_params=_cparams(("arbitrary",), VMEM_LIMIT),
        name="moe_router",
    )(x3, mod3, mod3, gain.reshape(1, d), wrt_bf16, bcol, utri)


def _plan_kernel(mi_ref, cnt_ref, cnt_other_ref, ltri_ref, pos_ref, tinfo_ref, *, n_tiles_pad, row_tile, second):
    cnt = cnt_ref[...] + cnt_other_ref[...]
    nt = jnp.floor((cnt + (row_tile - 1)) * (1.0 / row_tile))
    tstart = _bdot(ltri_ref[...], nt)
    mi = mi_ref[...]
    tm = mi.shape[1]
    rid32 = lax.broadcasted_iota(I32, (N_EXPERTS, tm), 0)
    ts_col = tstart[:, 0:1]
    base_col = cnt_other_ref[:, 0:1] if second else jnp.zeros((N_EXPERTS, 1), F32)

    def pos_of(ex, rank):
        hit = rid32 == ex
        start = jnp.sum(jnp.where(hit, ts_col, 0.0), axis=0, keepdims=True)
        base = jnp.sum(jnp.where(hit, base_col, 0.0), axis=0, keepdims=True)
        return start.astype(I32) * row_tile + base.astype(I32) + rank

    p1 = pos_of(mi[0:1], mi[2:3])
    p2 = pos_of(mi[1:2], mi[3:4])
    rid8 = lax.broadcasted_iota(I32, (SUBLANES, tm), 0)
    pos_ref[0] = jnp.where(rid8 == 0, jnp.broadcast_to(p1, (SUBLANES, tm)),
                           jnp.where(rid8 == 1, jnp.broadcast_to(p2, (SUBLANES, tm)), 0))
    tend_col = ts_col + nt[:, 0:1]
    jt = lax.broadcasted_iota(I32, (N_EXPERTS, n_tiles_pad), 1).astype(F32)
    te = jnp.sum(jnp.where(tend_col <= jt, 1.0, 0.0), axis=0, keepdims=True)
    te = jnp.minimum(te, N_EXPERTS - 1.0).astype(I32)
    total = jnp.sum(nt[:, 0:1], axis=0, keepdims=True).astype(I32)
    rid8t = lax.broadcasted_iota(I32, (SUBLANES, n_tiles_pad), 0)
    tinfo_ref[...] = jnp.where(rid8t == 0, jnp.broadcast_to(te, (SUBLANES, n_tiles_pad)),
                               jnp.broadcast_to(total, (SUBLANES, n_tiles_pad)))


def _moe_plan(meta_i, counts, counts_other, n_tiles_pad, row_tile, second):
    t = meta_i.shape[1]
    tm = MOE_TOK_TILE
    steps = t // tm
    ltri = jnp.tril(jnp.ones((N_EXPERTS, N_EXPERTS), F32), k=-1).astype(BF16)
    return pl.pallas_call(
        functools.partial(_plan_kernel, n_tiles_pad=n_tiles_pad, row_tile=row_tile, second=second),
        grid=(steps,),
        in_specs=[pl.BlockSpec((SUBLANES, tm), lambda i: (0, i)),
                  pl.BlockSpec((N_EXPERTS, LANES), lambda i: (0, 0)),
                  pl.BlockSpec((N_EXPERTS, LANES), lambda i: (0, 0)),
                  pl.BlockSpec((N_EXPERTS, N_EXPERTS), lambda i: (0, 0))],
        out_specs=[pl.BlockSpec((1, SUBLANES, tm), lambda i: (i, 0, 0)),
                   pl.BlockSpec((SUBLANES, n_tiles_pad), lambda i: (0, 0))],
        out_shape=[jax.ShapeDtypeStruct((steps, SUBLANES, tm), I32),
                   jax.ShapeDtypeStruct((SUBLANES, n_tiles_pad), I32)],
        compiler_params=_cparams(("arbitrary",)),
        name="moe_plan",
    )(meta_i, counts, counts_other, ltri)


def _to_token_tiles(ref, val):
    tm, d = val.shape
    for s in range(d // LANES):
        ref[pl.ds(s, tm, stride=SUBLANES), :] = val[:, s * LANES:(s + 1) * LANES]


def _from_token_tiles(ref, tm):
    return jnp.concatenate([ref[pl.ds(s, tm, stride=SUBLANES), :] for s in range(SUBLANES)], axis=-1)


def _tile_rows(idx):
    return pl.ds(pl.multiple_of(idx * SUBLANES, SUBLANES), SUBLANES)


def _row_copy_out(buf_ref, slot, r, dst_hbm, p, sem):
    return pltpu.make_async_copy(buf_ref.at[slot, _tile_rows(r), :], dst_hbm.at[_tile_rows(p), :], sem.at[slot])


def _dispatch_kernel(x_ref, sh_ref, sc_ref, gain_ref, pos_ref, xs_in_ref, xs_ref,
                     buf_ref, idx_ref, sem_ref, isem_ref, *, tm, steps):
    del xs_in_ref
    i = pl.program_id(0)
    slot = i % 2

    def drain(s):
        for _ in range(2):
            pltpu.make_async_copy(buf_ref.at[s], xs_ref.at[pl.ds(0, tm * SUBLANES), :], sem_ref.at[s]).wait()

    @pl.when(i >= 2)
    def _():
        drain(slot)

    _to_token_tiles(buf_ref.at[slot], _norm_mod(x_ref[...], gain_ref[...], sc_ref[...], sh_ref[...]))
    cp = pltpu.make_async_copy(pos_ref.at[0], idx_ref, isem_ref)
    cp.start()
    cp.wait()

    def issue(r, carry):
        _row_copy_out(buf_ref, slot, r, xs_ref, idx_ref[0, r], sem_ref).start()
        _row_copy_out(buf_ref, slot, r, xs_ref, idx_ref[1, r], sem_ref).start()
        return carry
    lax.fori_loop(0, tm, issue, 0, unroll=8)

    @pl.when(i == steps - 1)
    def _():
        drain(slot)
        if steps > 1:
            drain(1 - slot)


def _moe_dispatch(x3, mod3, k_sh, k_sc, gain, pos3, xs0):
    n_seq, length, d = x3.shape
    assert d == SUBLANES * LANES
    sb, lb, steps, nl = _tok_tiling(n_seq, length, MOE_TOK_TILE)
    tm = sb * lb
    return pl.pallas_call(
        functools.partial(_dispatch_kernel, tm=tm, steps=steps),
        grid=(steps,),
        in_specs=[pl.BlockSpec((sb, lb, d), lambda i: (i // nl, i % nl, 0)),
                  pl.BlockSpec((sb, 1, d), lambda i: (i // nl, 0, k_sh)),
                  pl.BlockSpec((sb, 1, d), lambda i: (i // nl, 0, k_sc)),
                  pl.BlockSpec((1, d), lambda i: (0, 0)),
                  pl.BlockSpec((1, SUBLANES, tm), lambda i: (i, 0, 0)),
                  pl.BlockSpec(memory_space=pl.ANY)],
        out_specs=pl.BlockSpec(memory_space=pl.ANY),
        out_shape=jax.ShapeDtypeStruct(xs0.shape, F32),
        scratch_shapes=[pltpu.VMEM((2, tm * SUBLANES, LANES), F32),
                        pltpu.SMEM((SUBLANES, tm), I32),
                        pltpu.SemaphoreType.DMA((2,)),
                        pltpu.SemaphoreType.DMA(())],
        input_output_aliases={5: 0},
        compiler_params=_cparams(("arbitrary",), VMEM_LIMIT),
        name="moe_dispatch",
    )(x3, mod3, mod3, gain.reshape(1, d), pos3, xs0)


def _expert_kernel(te_ref, nt_ref, xs_ref, wg_ref, wu_ref, wd_ref, ys_ref, *, row_tile):
    i = pl.program_id(0)

    @pl.when(i < nt_ref[0])
    def _():
        x = _from_token_tiles(xs_ref, row_tile).astype(BF16)
        hid = _silu(_bdot(x, wg_ref[...])) * _bdot(x, wu_ref[...])
        _to_token_tiles(ys_ref, _bdot(hid, wd_ref[...]))

    @pl.when(i >= nt_ref[0])
    def _():
        ys_ref[...] = jnp.zeros_like(ys_ref)


def _moe_experts(xs, tile_expert, n_tiles, wg, wu, wd, layer, row_tile):
    d, f = wg.shape[-2], wg.shape[-1]
    steps = xs.shape[0] // (row_tile * SUBLANES)
    x_map = lambda i, te, nt: (jnp.minimum(i, nt[0] - 1), 0)
    w_map = lambda i, te, nt: (layer, te[i] // E_PER_GROUP, te[i] % E_PER_GROUP, 0, 0)
    grid_spec = pltpu.PrefetchScalarGridSpec(
        num_scalar_prefetch=2,
        grid=(steps,),
        in_specs=[pl.BlockSpec((row_tile * SUBLANES, LANES), x_map),
                  pl.BlockSpec((None, None, None, d, f), w_map),
                  pl.BlockSpec((None, None, None, d, f), w_map),
                  pl.BlockSpec((None, None, None, f, d), w_map)],
        out_specs=pl.BlockSpec((row_tile * SUBLANES, LANES), lambda i, te, nt: (i, 0)),
    )
    return pl.pallas_call(
        functools.partial(_expert_kernel, row_tile=row_tile),
        grid_spec=grid_spec,
        out_shape=jax.ShapeDtypeStruct(xs.shape, F32),
        compiler_params=_cparams(("arbitrary",), VMEM_LIMIT),
        name="moe_experts",
    )(tile_expert, n_tiles, xs, wg, wu, wd)


def _row_copy_in(src_hbm, p, buf_ref, slot, which, r, sem):
    return pltpu.make_async_copy(src_hbm.at[_tile_rows(p), :], buf_ref.at[slot, which, _tile_rows(r), :],
                                 sem.at[slot])


def _combine_kernel(x_ref, g_ref, gcol_ref, pos_ref, posn_ref, ys_ref, gainf_ref, xo_ref,
                    buf_ref, idx_ref, sem_ref, isem_ref, *, tm, steps, sb, lb, final_norm):
    i = pl.program_id(0)
    slot = i % 2

    def issue(pref, s):
        cp = pltpu.make_async_copy(pref.at[0], idx_ref, isem_ref)
        cp.start()
        cp.wait()

        def body(r, carry):
            _row_copy_in(ys_ref, idx_ref[0, r], buf_ref, s, 0, r, sem_ref).start()
            _row_copy_in(ys_ref, idx_ref[1, r], buf_ref, s, 1, r, sem_ref).start()
            return carry
        lax.fori_loop(0, tm, body, 0, unroll=8)

    @pl.when(i == 0)
    def _():
        issue(pos_ref, slot)

    @pl.when(i + 1 < steps)
    def _():
        issue(posn_ref, 1 - slot)

    for which in range(2):
        pltpu.make_async_copy(ys_ref.at[pl.ds(0, tm * SUBLANES), :], buf_ref.at[slot, which],
                              sem_ref.at[slot]).wait()

    gc = gcol_ref[...]
    y = gc[:, 0:1] * _from_token_tiles(buf_ref.at[slot, 0], tm) \
        + gc[:, 1:2] * _from_token_tiles(buf_ref.at[slot, 1], tm)
    d = y.shape[-1]
    xn = x_ref[...] + g_ref[...] * y.reshape(sb, lb, d)
    if final_norm:
        xn = xn * lax.rsqrt(jnp.mean(xn * xn, axis=-1, keepdims=True) + EPS) * gainf_ref[...]
    xo_ref[...] = xn


def _moe_combine(x3, mod3, k_gate, gates_col, pos3, ys, gain_f, final_norm):
    n_seq, length, d = x3.shape
    sb, lb, steps, nl = _tok_tiling(n_seq, length, MOE_TOK_TILE)
    tm = sb * lb
    return pl.pallas_call(
        functools.partial(_combine_kernel, tm=tm, steps=steps, sb=sb, lb=lb, final_norm=final_norm),
        grid=(steps,),
        in_specs=[pl.BlockSpec((sb, lb, d), lambda i: (i // nl, i % nl, 0)),
                  pl.BlockSpec((sb, 1, d), lambda i: (i // nl, 0, k_gate)),
                  pl.BlockSpec((tm, LANES), lambda i: (i, 0)),
                  pl.BlockSpec((1, SUBLANES, tm), lambda i: (i, 0, 0)),
                  pl.BlockSpec((1, SUBLANES, tm), lambda i: (jnp.minimum(i + 1, steps - 1), 0, 0)),
                  pl.BlockSpec(memory_space=pl.ANY),
                  pl.BlockSpec((1, d), lambda i: (0, 0))],
        out_specs=pl.BlockSpec((sb, lb, d), lambda i: (i // nl, i % nl, 0)),
        out_shape=jax.ShapeDtypeStruct((n_seq, length, d), F32),
        scratch_shapes=[pltpu.VMEM((2, 2, tm * SUBLANES, LANES), F32),
                        pltpu.SMEM((SUBLANES, tm), I32),
                        pltpu.SemaphoreType.DMA((2,)),
                        pltpu.SemaphoreType.DMA(())],
        compiler_params=_cparams(("arbitrary",), VMEM_LIMIT),
        name="moe_combine",
    )(x3, mod3, gates_col, pos3, pos3, ys, gain_f.reshape(1, d))


def _hier_moe_residual(groups, k_sh, k_sc, k_gate, gain, wrt_bf16, bcol, wg, wu, wd, layer, gain_f, final_norm):
    assert len(groups) == 2
    t_all = sum(x3.shape[0] * x3.shape[1] for x3, _ in groups)
    row_tile = MOE_ROW_TILE_BIG if (2 * t_all) // N_EXPERTS >= MOE_ROW_TILE_BIG else MOE_ROW_TILE_SMALL
    max_tiles = -(-(2 * t_all) // row_tile) + N_EXPERTS
    n_tiles_pad = -(-max_tiles // LANES) * LANES
    xs_rows = max_tiles * row_tile * SUBLANES
    routed = [_moe_router(x3, mod3, k_sh, k_sc, gain, wrt_bf16, bcol, xs_rows if gi == 0 else 0)
              for gi, (x3, mod3) in enumerate(groups)]
    xs = routed[0][3]
    plans = [_moe_plan(routed[gi][0], routed[gi][2], routed[1 - gi][2], n_tiles_pad, row_tile, gi == 1)
             for gi in range(2)]
    for (x3, mod3), (pos3, _) in zip(groups, plans):
        xs = _moe_dispatch(x3, mod3, k_sh, k_sc, gain, pos3, xs)
    tinfo = plans[0][1]
    ys = _moe_experts(xs, tinfo[0, :max_tiles], tinfo[1, :1], wg, wu, wd, layer, row_tile)
    return [_moe_combine(x3, mod3, k_gate, r[1], pos3, ys, gain_f, final_norm)
            for (x3, mod3), r, (pos3, _) in zip(groups, routed, plans)]


def _multiplicity(dist):
    m = jnp.zeros(dist.shape, F32)
    for window, dil in BRANCHES:
        m = m + ((dist >= 0) & (dist <= window) & (dist % dil == 0)).astype(F32)
    return m


def _log_mult(dist):
    m = _multiplicity(dist)
    return jnp.where(m > 0, jnp.log(jnp.maximum(m, 1.0)), NEG_BIG)


def _attn_prompt_kernel(q_ref, k_ref, v_ref, nbias_ref, fbias_ref, o_ref, mf_ref, lf_ref, af_ref, *, length):
    blk = ATT_BLK
    nq = length // blk
    near = ATT_NEAR_BLOCKS
    ncls = BRANCHES[-1][1]
    csz = length // ncls
    has_far = nq > near
    scale = (HD_C ** -0.5) * LOG2_E
    nt = lambda a, b: lax.dot_general(a, b, (((1,), (1,)), ((), ())), preferred_element_type=F32)
    mm = lambda a, b: jnp.dot(a, b, preferred_element_type=F32)

    def split_heads(q):
        head0 = lax.broadcasted_iota(I32, q.shape, 1) < HD_C
        return head0, jnp.where(head0, q, 0.0).astype(BF16), jnp.where(head0, 0.0, q).astype(BF16)

    if has_far:
        fbias = fbias_ref[...]
        group = 4
        for r0 in range(0, ncls, group):
            rs = list(range(r0, r0 + group))
            rows = [pl.ds(r, csz, stride=ncls) for r in rs]
            hq = [split_heads(q_ref[0, rw, :] * scale) for rw in rows]
            kr = [k_ref[0, rw, :].astype(BF16) for rw in rows]
            vr = [v_ref[0, rw, :].astype(BF16) for rw in rows]
            s0 = [nt(h[1], k) + fbias for h, k in zip(hq, kr)]
            s1 = [nt(h[2], k) + fbias for h, k in zip(hq, kr)]
            m0 = [jnp.max(s, axis=-1, keepdims=True) for s in s0]
            m1 = [jnp.max(s, axis=-1, keepdims=True) for s in s1]
            p0 = [jnp.exp2(s - m) for s, m in zip(s0, m0)]
            p1 = [jnp.exp2(s - m) for s, m in zip(s1, m1)]
            a0 = [mm(p.astype(BF16), v) for p, v in zip(p0, vr)]
            a1 = [mm(p.astype(BF16), v) for p, v in zip(p1, vr)]
            for i, rw in enumerate(rows):
                head0 = hq[i][0]
                af_ref[rw, :] = jnp.where(head0, a0[i], a1[i])
                mf_ref[rw, :] = jnp.where(head0, m0[i], m1[i])
                lf_ref[rw, :] = jnp.where(head0, jnp.sum(p0[i], axis=-1, keepdims=True),
                                          jnp.sum(p1[i], axis=-1, keepdims=True))

    for qi in range(nq):
        rq = slice(qi * blk, (qi + 1) * blk)
        head0, q0, q1 = split_heads(q_ref[0, rq, :] * scale)
        js = list(range(max(0, qi - near + 1), qi + 1))
        ks = [k_ref[0, kj * blk:(kj + 1) * blk, :].astype(BF16) for kj in js]
        vs = [v_ref[0, kj * blk:(kj + 1) * blk, :].astype(BF16) for kj in js]
        far = has_far and qi >= near

        def fold(x, op):
            parts = [x[:, c * LANES:(c + 1) * LANES] for c in range(blk // LANES)]
            return functools.reduce(op, parts)

        q0d, q1d = q0 * 2, q1 * 2
        mr0 = functools.reduce(jnp.maximum, [fold(nt(q0d, k) * 0.5 + nbias_ref[qi - kj], jnp.maximum)
                                             for k, kj in zip(ks, js)])
        mr1 = functools.reduce(jnp.maximum, [fold(nt(q1d, k) * 0.5 + nbias_ref[qi - kj], jnp.maximum)
                                             for k, kj in zip(ks, js)])
        m0 = jnp.max(mr0, axis=-1, keepdims=True)
        m1 = jnp.max(mr1, axis=-1, keepdims=True)
        if far:
            mf = mf_ref[rq, :]
            mf0, mf1 = mf[:, 0:1], mf[:, HD_C:HD_C + 1]
            m0 = jnp.maximum(m0, mf0)
            m1 = jnp.maximum(m1, mf1)
        lr0 = lr1 = a0 = a1 = None
        for k, v, kj in zip(ks, vs, js):
            bias = nbias_ref[qi - kj]
            p0 = jnp.exp2(nt(q0, k) + bias - m0)
            p1 = jnp.exp2(nt(q1, k) + bias - m1)
            f0, f1 = fold(p0, jnp.add), fold(p1, jnp.add)
            t0, t1 = mm(p0.astype(BF16), v), mm(p1.astype(BF16), v)
            lr0, lr1 = (f0, f1) if lr0 is None else (lr0 + f0, lr1 + f1)
            a0, a1 = (t0, t1) if a0 is None else (a0 + t0, a1 + t1)
        l0 = jnp.sum(lr0, axis=-1, keepdims=True)
        l1 = jnp.sum(lr1, axis=-1, keepdims=True)
        acc = jnp.where(head0, a0, a1)
        den = jnp.where(head0, l0, l1)
        if far:
            w_far = jnp.where(head0, jnp.exp2(mf0 - m0), jnp.exp2(mf1 - m1))
            acc = acc + af_ref[rq, :] * w_far
            den = den + lf_ref[rq, :] * w_far
        o_ref[0, rq, :] = (acc / den).astype(o_ref.dtype)


def _attn_prompt(q3, k3, v3):
    n_seq, length, d = q3.shape
    pairs = d // LANES
    blk = ATT_BLK
    near = ATT_NEAR_BLOCKS
    ncls = BRANCHES[-1][1]
    csz = length // ncls
    per = blk // ncls
    assert (near - 1) * blk >= BRANCHES[-2][0] and length % blk == 0 and blk % ncls == 0
    a = jnp.arange(blk)
    dist = (jnp.arange(near)[:, None, None] * blk) + a[None, :, None] - a[None, None, :]
    nbias = _log_mult(dist) * jnp.where(_multiplicity(dist) > 0, LOG2_E, 1.0)
    m = jnp.arange(csz)
    far_ok = ((m[:, None] // per - m[None, :] // per) >= near) & \
        ((m[:, None] - m[None, :]) * ncls <= BRANCHES[-1][0])
    fbias = jnp.where(far_ok, 0.0, NEG_BIG).astype(F32)
    return pl.pallas_call(
        functools.partial(_attn_prompt_kernel, length=length),
        grid=(n_seq, pairs),
        in_specs=[pl.BlockSpec((1, length, LANES), lambda n, p: (n, 0, p)),
                  pl.BlockSpec((1, length, LANES), lambda n, p: (n, 0, p)),
                  pl.BlockSpec((1, length, LANES), lambda n, p: (n, 0, p)),
                  pl.BlockSpec((near, blk, blk), lambda n, p: (0, 0, 0)),
                  pl.BlockSpec((csz, csz), lambda n, p: (0, 0))],
        out_specs=pl.BlockSpec((1, length, LANES), lambda n, p: (n, 0, p)),
        out_shape=jax.ShapeDtypeStruct((n_seq, length, d), BF16),
        scratch_shapes=[pltpu.VMEM((length, LANES), F32)] * 3,
        compiler_params=_cparams(("arbitrary", "arbitrary"), VMEM_LIMIT),
        name="attn_prompt",
    )(q3, k3, v3, nbias, fbias)


def _attn_sample_kernel(q_ref, kn_ref, vn_ref, kt_ref, vt_ref, bias_ref, biasn_ref, o_ref, *, heads):
    q = q_ref[0] * (HD_C ** -0.5)
    kn = kn_ref[0]
    vn = vn_ref[0]
    bias = bias_ref[...]
    biasn = biasn_ref[...]
    outs = []
    for h in range(heads):
        lo, hi = h * HD_C, (h + 1) * HD_C
        qh = q[:, lo:hi].astype(BF16)
        s = jnp.dot(qh, kt_ref[0, h].astype(BF16), preferred_element_type=F32) + bias
        sn = _bdot_nt(qh, kn[:, lo:hi]) + biasn
        m = jnp.maximum(jnp.max(s, axis=-1, keepdims=True), jnp.max(sn, axis=-1, keepdims=True))
        p = jnp.exp(s - m)
        pn = jnp.exp(sn - m)
        den = jnp.sum(p, axis=-1, keepdims=True) + jnp.sum(pn, axis=-1, keepdims=True)
        o = _bdot_nt(p, vt_ref[0, h]) + _bdot(pn, vn[:, lo:hi])
        outs.append(o / den)
    o_ref[0] = jnp.concatenate(outs, axis=-1)


def _attn_sample(q3, k3, v3, k_cache, v_cache):
    n_seq, t_len, d = q3.shape
    w_buf = k_cache.shape[1]
    kt = jnp.transpose(k_cache, (0, 2, 3, 1))
    vt = jnp.transpose(v_cache, (0, 2, 3, 1))
    heads = SAMPLE_HEADS_PER_STEP
    hw = heads * HD_C
    nj = H_C // heads
    tpos = w_buf + jnp.arange(t_len)
    bias = _log_mult(tpos[:, None] - jnp.arange(w_buf)[None, :])
    biasn = _log_mult(jnp.arange(t_len)[:, None] - jnp.arange(t_len)[None, :])
    return pl.pallas_call(
        functools.partial(_attn_sample_kernel, heads=heads),
        grid=(n_seq, nj),
        in_specs=[pl.BlockSpec((1, t_len, hw), lambda n, j: (n, 0, j)),
                  pl.BlockSpec((1, t_len, hw), lambda n, j: (n, 0, j)),
                  pl.BlockSpec((1, t_len, hw), lambda n, j: (n, 0, j)),
                  pl.BlockSpec((1, heads, HD_C, w_buf), lambda n, j: (n, j, 0, 0)),
                  pl.BlockSpec((1, heads, HD_C, w_buf), lambda n, j: (n, j, 0, 0)),
                  pl.BlockSpec((t_len, w_buf), lambda n, j: (0, 0)),
                  pl.BlockSpec((t_len, t_len), lambda n, j: (0, 0))],
        out_specs=pl.BlockSpec((1, t_len, hw), lambda n, j: (n, 0, j)),
        out_shape=jax.ShapeDtypeStruct((n_seq, t_len, d), F32),
        compiler_params=_cparams(("arbitrary", "arbitrary"), VMEM_LIMIT),
        name="attn_sample",
    )(q3, k3, v3, kt, vt, bias, biasn)


def _pad_hist(hist, n_seq, width):
    k1 = hist.shape[1]
    return jnp.concatenate([jnp.zeros((n_seq, SUBLANES - k1, width), F32), hist.astype(F32)], axis=1)


def _prep_weights(p):
    d = p['w_in_even'].shape[1]
    out = {}
    w_in = p['w_in_even'][0]
    hv = H_A * DV_A
    o_z = QKV_A
    o_a = o_z + hv
    o_g = o_a + 2 * H_A
    ab_pad = jnp.zeros((d, LANES - 2 * H_A), F32)
    out['w_in'] = jnp.concatenate([w_in[:, :o_a], w_in[:, o_a:o_g], ab_pad, w_in[:, o_g:]], axis=1).astype(BF16)
    out['w_out_even'] = p['w_out_even'][0].astype(BF16)
    out['w_qkv'] = p['w_qkv_odd'][0].astype(BF16)
    out['w_out_odd'] = p['w_out_odd'][0].astype(BF16)
    depth = p['w_router_group'].shape[0]
    wrt, bcol = [], []
    for l in range(depth):
        pad_w = jnp.zeros((SUBLANES - N_GROUPS, d), F32)
        wrt.append(jnp.concatenate([p['w_router_group'][l].T, pad_w, p['w_router_expert'][l].T], axis=0).astype(BF16))
        pad_b = jnp.full((SUBLANES - N_GROUPS,), NEG_BIG, F32)
        bcol.append(jnp.concatenate([p['b_router_group'][l], pad_b, p['b_router_expert'][l]]).reshape(-1, 1))
    out.update(wrt=wrt, bcol=bcol, wg=p['w_exp_gate'], wu=p['w_exp_up'], wd=p['w_exp_down'])
    return out


def _layer0_mixers(x3, mod3, a_ssm, a_conv, b_conv, p, w):
    n_seq, length, d = x3.shape
    t = n_seq * length
    hv = H_A * DV_A
    qkv_pre, z, ab, gcx = _norm_mod_matmul(x3, mod3, 0, 1, p['norm1'][0], w['w_in'],
                                           (QKV_A, hv, LANES, 3 * D_B))
    chunk = min(CHUNK_PROMPT, length)
    g_chunks = DELTA_CHUNKS_PER_STEP if length > chunk else DELTA_SEQS_PER_STEP
    qkv_pre3 = qkv_pre.reshape(n_seq, length, QKV_A)
    ub, wm, qd, kd, qk, dec = _delta_prep(qkv_pre3, _pad_hist(a_conv[0], n_seq, QKV_A),
                                          ab.reshape(n_seq, length, LANES), p['conv_a_w'][0],
                                          p['a_log'][0], p['dt_bias'][0], chunk, g_chunks)
    out_a, s_new = _delta_scan(ub, wm, qd, kd, qk, dec, z.reshape(n_seq, length, hv),
                               a_ssm[0].astype(F32), p['o_gain_a'][0], chunk)
    x3, bx_tail = _even_out(gcx.reshape(n_seq, length, 3 * D_B), _pad_hist(b_conv[0], n_seq, D_B),
                            out_a.reshape(t, hv), x3, mod3, 2, p['conv_b_w'][0], w['w_out_even'])
    new_aconv = jnp.concatenate([a_conv[0].astype(F32), qkv_pre3], axis=1)[:, -(CONV_A - 1):]
    new_bconv = jnp.concatenate([b_conv[0].astype(F32), bx_tail], axis=1)[:, -(CONV_B - 1):]
    return x3, s_new[None], new_aconv[None], new_bconv[None]


def _layer1_attention(x3, mod3, kv_k, kv_v, p, w, sample):
    n_seq, length, d = x3.shape
    t = n_seq * length
    if sample:
        q_new, k_new, v_new = (a.reshape(n_seq, length, d) for a in
                               _norm_mod_matmul(x3, mod3, 0, 1, p['norm1'][1], w['w_qkv'], (d, d, d)))
        attn = _attn_sample(q_new, k_new, v_new, kv_k[0], kv_v[0])
        new_k = k_new.reshape(n_seq, length, H_C, HD_C)
        new_v = v_new.reshape(n_seq, length, H_C, HD_C)
    else:
        q_new, k_new, v_new, k_t, v_t = _norm_mod_matmul(x3, mod3, 0, 1, p['norm1'][1], w['w_qkv'],
                                                         (d, d, d), transposed=(1, 2))
        q_new, k_new, v_new = (a.reshape(n_seq, length, d) for a in (q_new, k_new, v_new))
        attn = _attn_prompt(q_new, k_new, v_new)
        keep = min(BRANCHES[-1][0], length)
        new_k = jnp.transpose(k_t[:, :, length - keep:].reshape(n_seq, H_C, HD_C, keep), (0, 3, 1, 2))
        new_v = jnp.transpose(v_t[:, :, length - keep:].reshape(n_seq, H_C, HD_C, keep), (0, 3, 1, 2))
    x3 = _proj_residual(attn.reshape(t, d), x3, mod3, 2, w['w_out_odd'])
    return x3, new_k[None], new_v[None]


def kernel(x_prompt, x_sample, state_a_ssm, state_a_conv, state_b_conv, cache_c_k, cache_c_v, c_prompt, c_sample, w_ada, b_ada, norm1, norm2, norm_f, w_in_even, conv_a_w, a_log, dt_bias, o_gain_a, conv_b_w, w_out_even, w_qkv_odd, w_out_odd, w_router_group, b_router_group, w_router_expert, b_router_expert, w_exp_gate, w_exp_up, w_exp_down):
    p = dict(norm1=norm1, norm2=norm2, norm_f=norm_f, w_in_even=w_in_even, conv_a_w=conv_a_w,
             a_log=a_log, dt_bias=dt_bias, o_gain_a=o_gain_a, conv_b_w=conv_b_w, w_out_even=w_out_even,
             w_qkv_odd=w_qkv_odd, w_out_odd=w_out_odd, w_router_group=w_router_group,
             b_router_group=b_router_group, w_router_expert=w_router_expert,
             b_router_expert=b_router_expert, w_exp_gate=w_exp_gate, w_exp_up=w_exp_up,
             w_exp_down=w_exp_down)
    w = _prep_weights(p)
    nb = x_prompt.shape[0]
    ns = x_sample.shape[0]
    mod_all = _ada_mod(jnp.concatenate([c_prompt, c_sample], axis=0), w_ada, b_ada)
    depth = w_ada.shape[0]
    mod_p = [mod_all[l, :nb] for l in range(depth)]
    mod_s = [mod_all[l, nb:] for l in range(depth)]
    n_even = state_a_ssm.shape[0]
    zero_ssm = jnp.zeros((n_even, nb, H_A, DK_A, DV_A), F32)
    zero_aconv = jnp.zeros((n_even, nb, CONV_A - 1, QKV_A), F32)
    zero_bconv = jnp.zeros((n_even, nb, CONV_B - 1, D_B), F32)
    d = x_prompt.shape[-1]
    m3_p = [m.reshape(nb, 1, 6 * d) for m in mod_p]
    m3_s = [m.reshape(ns, 1, 6 * d) for m in mod_s]

    def moe(xp, xs_, layer, final_norm):
        return _hier_moe_residual([(xp, m3_p[layer]), (xs_, m3_s[layer])], 3, 4, 5, norm2[layer],
                                  w['wrt'][layer], w['bcol'][layer], w['wg'], w['wu'], w['wd'], layer,
                                  norm_f, final_norm)

    xp, ssm_p, aconv_p, bconv_p = _layer0_mixers(x_prompt, m3_p[0], zero_ssm, zero_aconv, zero_bconv, p, w)
    xs_, ssm_s, aconv_s, bconv_s = _layer0_mixers(x_sample, m3_s[0], state_a_ssm, state_a_conv,
                                                  state_b_conv, p, w)
    xp, xs_ = moe(xp, xs_, 0, False)
    xp, k_p, v_p = _layer1_attention(xp, m3_p[1], None, None, p, w, False)
    xs_, k_s, v_s = _layer1_attention(xs_, m3_s[1], cache_c_k, cache_c_v, p, w, True)
    y_p, y_s = moe(xp, xs_, 1, True)
    return (y_p, y_s, ssm_p, ssm_s, aconv_p, aconv_s, bconv_p, bconv_s, k_p, k_s, v_p, v_s)
```

```python
import functools
import math

import jax
import jax.numpy as jnp
from jax import lax
from jax.experimental import pallas as pl
from jax.experimental.pallas import tpu as pltpu

F32 = jnp.float32
BF16 = jnp.bfloat16
I32 = jnp.int32

EPS = 1e-6
NEG_BIG = -1e30

LANES = 128
SUBLANES = 8
VMEM_BYTES_V7X = 64 * 1024 * 1024
VMEM_LIMIT = 56 * 1024 * 1024

H_A, DK_A, DV_A = 4, 128, 128
CONV_A, CONV_B = 4, 3
QKV_A = H_A * (2 * DK_A + DV_A)
D_B = 512
H_C, HD_C = 16, 64
BRANCHES = ((128, 1), (512, 4), (2048, 16))
N_GROUPS, E_PER_GROUP = 4, 8
N_EXPERTS = N_GROUPS * E_PER_GROUP
CHUNK_PROMPT = 64
DELTA_CHUNKS_PER_STEP = 4
DELTA_SEQS_PER_STEP = 8

TOK_TILE = 512
MOE_TOK_TILE = 1024
MOE_ROW_TILE_BIG = 512
MOE_ROW_TILE_SMALL = 128
ATT_BLK = 256
ATT_NEAR_BLOCKS = 3
LOG2_E = math.log2(math.e)
SEQ_BLOCK_SCAN = 8
SAMPLE_HEADS_PER_STEP = 16


def _cparams(sem, vmem=None):
    return pltpu.CompilerParams(dimension_semantics=sem, vmem_limit_bytes=vmem)


def _silu(x):
    return x * jax.nn.sigmoid(x)


def _bdot(a, b):
    return jnp.dot(a.astype(BF16), b.astype(BF16), preferred_element_type=F32)


def _bdot_nt(a, b):
    return lax.dot_general(a.astype(BF16), b.astype(BF16), (((1,), (1,)), ((), ())),
                           preferred_element_type=F32)


def _bdot_tn(a, b):
    return lax.dot_general(a.astype(BF16), b.astype(BF16), (((0,), (0,)), ((), ())),
                           preferred_element_type=F32)


def _split_bf16(x):
    hi = x.astype(BF16)
    return hi, (x - hi.astype(F32)).astype(BF16)


def _dot3(a, b):
    ah, al = _split_bf16(a)
    bh, bl = _split_bf16(b)
    dot = functools.partial(jnp.dot, preferred_element_type=F32)
    return dot(ah, bh) + (dot(ah, bl) + dot(al, bh))


def _fdot(a, b):
    return jnp.dot(a, b, preferred_element_type=F32, precision=lax.Precision.HIGHEST)


def _fdot_nt(a, b):
    return lax.dot_general(a, b, (((1,), (1,)), ((), ())), preferred_element_type=F32,
                           precision=lax.Precision.HIGHEST)


def _tok_tiling(n_seq, length, tile):
    assert (n_seq * length) % tile == 0 and (length % tile == 0 or tile % length == 0)
    if length >= tile:
        sb, lb = 1, tile
    else:
        sb, lb = tile // length, length
    nl = length // lb
    return sb, lb, (n_seq // sb) * nl, nl


def _norm_mod(x, gain, sc, sh):
    sb, lb, d = x.shape
    y = x * lax.rsqrt(jnp.mean(x * x, axis=-1, keepdims=True) + EPS) * gain
    h = y * (1.0 + sc) + sh
    return h.reshape(sb * lb, d)


def _ada_kernel(c_ref, w_ref, b_ref, o_ref):
    c = _silu(c_ref[...])
    o_ref[0] = _bdot(c, w_ref[0]) + b_ref[0]


def _ada_mod(c_all, w_ada, b_ada):
    r, d = c_all.shape
    depth, _, n6 = w_ada.shape
    tn = 1536
    return pl.pallas_call(
        _ada_kernel,
        grid=(depth, n6 // tn),
        in_specs=[pl.BlockSpec((r, d), lambda l, j: (0, 0)),
                  pl.BlockSpec((1, d, tn), lambda l, j: (l, 0, j)),
                  pl.BlockSpec((1, 1, tn), lambda l, j: (l, 0, j))],
        out_specs=pl.BlockSpec((1, r, tn), lambda l, j: (l, 0, j)),
        out_shape=jax.ShapeDtypeStruct((depth, r, n6), F32),
        compiler_params=_cparams(("arbitrary", "arbitrary"), VMEM_LIMIT),
        name="ada_mod",
    )(c_all, w_ada, b_ada.reshape(depth, 1, n6))


def _nmm_kernel(x_ref, sh_ref, sc_ref, gain_ref, w_ref, *o_refs, splits, transposed):
    h = _norm_mod(x_ref[...], gain_ref[...], sc_ref[...], sh_ref[...]).astype(BF16)
    off = 0
    vals = []
    for o_ref, n in zip(o_refs, splits):
        vals.append(jnp.dot(h, w_ref[:, off:off + n], preferred_element_type=F32))
        o_ref[...] = vals[-1]
        off += n
    for o_ref, j in zip(o_refs[len(splits):], transposed):
        o_ref[0] = vals[j].T


def _norm_mod_matmul(x3, mod3, k_sh, k_sc, gain, w_bf16, splits, transposed=()):
    n_seq, length, d = x3.shape
    sb, lb, steps, nl = _tok_tiling(n_seq, length, TOK_TILE)
    rows = sb * lb
    t = n_seq * length
    n_out = w_bf16.shape[1]
    assert sum(splits) == n_out and (not transposed or sb == 1)
    t_specs = [pl.BlockSpec((1, splits[j], lb), lambda i: (i // nl, 0, i % nl)) for j in transposed]
    t_shapes = [jax.ShapeDtypeStruct((n_seq, splits[j], length), F32) for j in transposed]
    return pl.pallas_call(
        functools.partial(_nmm_kernel, splits=splits, transposed=transposed),
        grid=(steps,),
        in_specs=[pl.BlockSpec((sb, lb, d), lambda i: (i // nl, i % nl, 0)),
                  pl.BlockSpec((sb, 1, d), lambda i: (i // nl, 0, k_sh)),
                  pl.BlockSpec((sb, 1, d), lambda i: (i // nl, 0, k_sc)),
                  pl.BlockSpec((1, d), lambda i: (0, 0)),
                  pl.BlockSpec((d, n_out), lambda i: (0, 0))],
        out_specs=[pl.BlockSpec((rows, n), lambda i: (i, 0)) for n in splits] + t_specs,
        out_shape=[jax.ShapeDtypeStruct((t, n), F32) for n in splits] + t_shapes,
        compiler_params=_cparams(("arbitrary",), VMEM_LIMIT),
        name="norm_mod_matmul",
    )(x3, mod3, mod3, gain.reshape(1, d), w_bf16)


def _delta_prep_kernel(qkv_ref, halo_ref, hist_ref, ab_ref, cw_ref, al_ref, dtb_ref, lt_ref, sel_ref,
                       ub_ref, wm_ref, qd_ref, kd_ref, qk_ref, dec_ref,
                       *, g_chunks, chunk, chunks_are_seqs, chunks_per_seq):
    c = chunk
    step = pl.program_id(0)
    cw = cw_ref[...]
    row = lax.broadcasted_iota(I32, (c, c), 0)
    col = lax.broadcasted_iota(I32, (c, c), 1)
    lower_incl = row >= col
    strict = row > col
    eye = (row == col).astype(F32)
    n_fac = max(int(math.ceil(math.log2(c))), 1)

    qkvs, abs_ = [], []
    for g in range(g_chunks):
        cur = qkv_ref[g] if chunks_are_seqs else qkv_ref[0, g * c:(g + 1) * c, :]
        if chunks_are_seqs:
            prev = hist_ref[g]
        elif g == 0:
            first = (step % (chunks_per_seq // g_chunks)) == 0
            prev = jnp.where(first, hist_ref[0], halo_ref[0])
        else:
            prev = qkv_ref[0, g * c - SUBLANES:g * c, :]
        row8 = lax.broadcasted_iota(I32, (SUBLANES, cur.shape[1]), 0)
        conv = cur * cw[CONV_A - 1:CONV_A, :]
        for dback in range(1, CONV_A):
            rolled = pltpu.roll(cur, dback, 0)
            top = jnp.where(row8 < dback, pltpu.roll(prev, dback, 0), rolled[0:SUBLANES])
            shifted = top if c == SUBLANES else jnp.concatenate([top, rolled[SUBLANES:]], axis=0)
            conv = conv + shifted * cw[CONV_A - 1 - dback:CONV_A - dback, :]
        qkvs.append(_silu(conv))
        abs_.append(ab_ref[g] if chunks_are_seqs else ab_ref[0, g * c:(g + 1) * c, :])

    g_alls = [-jnp.exp(al_ref[...]) * jax.nn.softplus(ab + dtb_ref[...]) for ab in abs_]
    sigs = [jax.nn.sigmoid(ab) for ab in abs_]
    gcum_alls = [_fdot(lt_ref[...], ga) for ga in g_alls]
    gcum_rows = [_fdot_nt(sel_ref[...], gc) for gc in gcum_alls]

    chains = [(g, h) for g in range(g_chunks) for h in range(H_A)]
    qs, ks, vs, betas, gcs, gammas, egs = [], [], [], [], [], [], []
    for g, h in chains:
        qkv = qkvs[g]
        q = qkv[:, h * DK_A:(h + 1) * DK_A]
        k = qkv[:, H_A * DK_A + h * DK_A:H_A * DK_A + (h + 1) * DK_A]
        v = qkv[:, 2 * H_A * DK_A + h * DV_A:2 * H_A * DK_A + (h + 1) * DV_A]
        qs.append(q * lax.rsqrt(jnp.sum(q * q, axis=-1, keepdims=True) + EPS) * (DK_A ** -0.5))
        ks.append(k * lax.rsqrt(jnp.sum(k * k, axis=-1, keepdims=True) + EPS))
        vs.append(v)
        betas.append(sigs[g][:, H_A + h:H_A + h + 1])
        gc = gcum_alls[g][:, h:h + 1]
        gr = gcum_rows[g][h:h + 1, :]
        gcs.append(gc)
        gammas.append(jnp.exp(jnp.where(lower_incl, gc - gr, NEG_BIG)))
        egs.append(jnp.exp(gc))

    kq = [_bdot_nt(jnp.concatenate([k, q], axis=0), k) for k, q in zip(ks, qs)]
    a_mats = [jnp.where(strict, b * x[0:c] * gm, 0.0) for b, x, gm in zip(betas, kq, gammas)]
    qkms = [x[c:2 * c] * gm for x, gm in zip(kq, gammas)]
    m_pows = [-a for a in a_mats]
    t_invs = [eye + m for m in m_pows]
    for _ in range(n_fac - 1):
        m_pows = [_bdot(m, m) for m in m_pows]
        t_invs = [t + _bdot(t, m) for t, m in zip(t_invs, m_pows)]
    resids = [eye - t - _dot3(a, t) for a, t in zip(a_mats, t_invs)]
    t_invs = [t + _bdot(t, r) for t, r in zip(t_invs, resids)]
    rhss = [jnp.concatenate([v * b, k * (b * eg)], axis=-1) for v, k, b, eg in zip(vs, ks, betas, egs)]
    sols = [rhs + _dot3(t - eye, rhs) for t, rhs in zip(t_invs, rhss)]

    for g in range(g_chunks):
        idx = [i for i, (gg, _) in enumerate(chains) if gg == g]
        g_last = [gcum_alls[g][c - 1:c, h:h + 1] for h in range(H_A)]

        def put(ref, parts):
            val = jnp.concatenate(parts, axis=-1).astype(ref.dtype)
            if chunks_are_seqs:
                ref[g] = val
            else:
                ref[0, g * c:(g + 1) * c, :] = val

        put(ub_ref, [sols[i][:, :DV_A] for i in idx])
        put(wm_ref, [sols[i][:, DV_A:] for i in idx])
        put(qd_ref, [qs[i] * egs[i] for i in idx])
        put(kd_ref, [ks[i] * jnp.exp(g_last[h] - gcs[i]) for h, i in enumerate(idx)])
        put(qk_ref, [qkms[i] for i in idx])
        dec_ref[g] = jnp.concatenate([jnp.broadcast_to(jnp.exp(gl), (SUBLANES, DV_A)) for gl in g_last], axis=-1)


def _delta_prep(qkv_pre3, hist8, ab3, conv_w, a_log, dt_bias, chunk, g_chunks):
    n_seq, length, w = qkv_pre3.shape
    c = chunk
    chunks_are_seqs = (length == c)
    nc = length // c
    if chunks_are_seqs:
        steps = n_seq // g_chunks
        blk = (g_chunks, c, w)
        x_map = lambda i: (i, 0, 0)
        halo_map = lambda i: (i, 0, 0)
        hist_spec = pl.BlockSpec((g_chunks, SUBLANES, w), lambda i: (i, 0, 0))
        ab_spec = pl.BlockSpec((g_chunks, c, LANES), lambda i: (i, 0, 0))
        out_map = lambda i: (i, 0, 0)
        out_rows = (g_chunks, c)
    else:
        rows = g_chunks * c
        spb = nc // g_chunks
        steps = n_seq * spb
        blk = (1, rows, w)
        x_map = lambda i: (i // spb, i % spb, 0)
        halo_map = lambda i: (i // spb, jnp.maximum((i % spb) * (rows // SUBLANES) - 1, 0), 0)
        hist_spec = pl.BlockSpec((1, SUBLANES, w), lambda i: (i // spb, 0, 0))
        ab_spec = pl.BlockSpec((1, rows, LANES), lambda i: (i // spb, i % spb, 0))
        out_map = x_map
        out_rows = (1, rows)
    op_dtype = BF16 if c % (2 * SUBLANES) == 0 else F32
    lt = jnp.tril(jnp.ones((c, c), F32))
    sel = jnp.eye(SUBLANES, LANES, dtype=F32)
    hv = H_A * DV_A
    al = jnp.zeros((1, LANES), F32).at[0, :H_A].set(a_log)
    dtb = jnp.zeros((1, LANES), F32).at[0, :H_A].set(dt_bias)
    kern = functools.partial(_delta_prep_kernel, g_chunks=g_chunks, chunk=c,
                             chunks_are_seqs=chunks_are_seqs, chunks_per_seq=nc)
    big = lambda width: pl.BlockSpec(out_rows + (width,), out_map)
    return pl.pallas_call(
        kern,
        grid=(steps,),
        in_specs=[pl.BlockSpec(blk, x_map),
                  pl.BlockSpec((1, SUBLANES, w), halo_map) if not chunks_are_seqs
                  else pl.BlockSpec((g_chunks, SUBLANES, w), halo_map),
                  hist_spec, ab_spec,
                  pl.BlockSpec((CONV_A, w), lambda i: (0, 0)),
                  pl.BlockSpec((1, LANES), lambda i: (0, 0)),
                  pl.BlockSpec((1, LANES), lambda i: (0, 0)),
                  pl.BlockSpec((c, c), lambda i: (0, 0)),
                  pl.BlockSpec((SUBLANES, LANES), lambda i: (0, 0))],
        out_specs=[big(hv), big(hv), big(hv), big(hv), big(H_A * c),
                   pl.BlockSpec((g_chunks, SUBLANES, hv), lambda i: (i, 0, 0))],
        out_shape=[jax.ShapeDtypeStruct((n_seq, length, hv), F32)]
        + [jax.ShapeDtypeStruct((n_seq, length, hv), op_dtype)] * 3
        + [jax.ShapeDtypeStruct((n_seq, length, H_A * c), op_dtype),
           jax.ShapeDtypeStruct((n_seq * nc, SUBLANES, hv), F32)],
        compiler_params=_cparams(("arbitrary",), VMEM_LIMIT),
        name="delta_prep",
    )(qkv_pre3, hist8 if chunks_are_seqs else qkv_pre3, hist8, ab3, conv_w, al, dtb, lt, sel)


def _delta_scan_kernel(ub_ref, wm_ref, qd_ref, kd_ref, qk_ref, dec_ref, z_ref, s0_ref, og_ref,
                       o_ref, sn_ref, s_ref, *, nb, chunk):
    c = chunk
    j = pl.program_id(1)

    @pl.when(j == 0)
    def _():
        s_ref[...] = s0_ref[...]

    og = og_ref[...]
    chains = [(b, h) for b in range(nb) for h in range(H_A)]
    sl = lambda h: slice(h * DV_A, (h + 1) * DV_A)
    states = [s_ref[b, h] for b, h in chains]
    ws = [_bdot(jnp.concatenate([wm_ref[b, :, sl(h)], qd_ref[b, :, sl(h)]], axis=0), s)
          for (b, h), s in zip(chains, states)]
    us = [ub_ref[b, :, sl(h)] - x[0:c] for (b, h), x in zip(chains, ws)]
    os_ = [x[c:2 * c] + _bdot(qk_ref[b, :, h * c:(h + 1) * c], u) for (b, h), x, u in zip(chains, ws, us)]
    for (b, h), s, u in zip(chains, states, us):
        s_ref[b, h] = s * dec_ref[b, 0, 0:1, sl(h)] + _bdot_tn(kd_ref[b, :, sl(h)], u)
    for b in range(nb):
        outs = []
        for h in range(H_A):
            o = os_[b * H_A + h]
            on = o * lax.rsqrt(jnp.mean(o * o, axis=-1, keepdims=True) + EPS) * og
            outs.append(on * _silu(z_ref[b, :, sl(h)]))
        o_ref[b] = jnp.concatenate(outs, axis=-1).astype(o_ref.dtype)
    sn_ref[...] = s_ref[...]


def _delta_scan(ub, wm, qd, kd, qk, dec, z3, s0, o_gain, chunk):
    n_seq, length, hv = ub.shape
    c = chunk
    nc = length // c
    nb = SEQ_BLOCK_SCAN
    dec4 = dec.reshape(n_seq, nc, SUBLANES, hv)
    tok = lambda width: pl.BlockSpec((nb, c, width), lambda b, j: (b, j, 0))
    st = pl.BlockSpec((nb, H_A, DK_A, DV_A), lambda b, j: (b, 0, 0, 0))
    return pl.pallas_call(
        functools.partial(_delta_scan_kernel, nb=nb, chunk=c),
        grid=(n_seq // nb, nc),
        in_specs=[tok(hv), tok(hv), tok(hv), tok(hv), tok(H_A * c),
                  pl.BlockSpec((nb, 1, SUBLANES, hv), lambda b, j: (b, j, 0, 0)),
                  tok(hv), st, pl.BlockSpec((1, DV_A), lambda b, j: (0, 0))],
        out_specs=[tok(hv), st],
        out_shape=[jax.ShapeDtypeStruct((n_seq, length, hv), BF16 if c % (2 * SUBLANES) == 0 else F32),
                   jax.ShapeDtypeStruct((n_seq, H_A, DK_A, DV_A), F32)],
        scratch_shapes=[pltpu.VMEM((nb, H_A, DK_A, DV_A), F32)],
        compiler_params=_cparams(("arbitrary", "arbitrary"), VMEM_LIMIT),
        name="delta_scan",
    )(ub, wm, qd, kd, qk, dec4, z3, s0, o_gain.reshape(1, DV_A))


def _even_out_kernel(gcx_ref, halo_ref, hist_ref, oa_ref, x_ref, g1_ref, cw_ref, w_ref,
                     xo_ref, tail_ref, ext_ref, *, sb, lb, nl):
    step = pl.program_id(0)
    gcx = gcx_ref[...]
    b_gate = gcx[:, :, 0:D_B]
    bx = gcx[:, :, D_B:2 * D_B] * gcx[:, :, 2 * D_B:3 * D_B]
    if nl == 1:
        prev = hist_ref[...]
    else:
        hb = halo_ref[...]
        first = (step % nl) == 0
        prev = jnp.where(first, hist_ref[...], hb[:, :, D_B:2 * D_B] * hb[:, :, 2 * D_B:3 * D_B])
    ext_ref[:, 0:SUBLANES, :] = prev
    ext_ref[:, SUBLANES:SUBLANES + lb, :] = bx
    cw = cw_ref[...]
    conv = ext_ref[:, pl.ds(SUBLANES - (CONV_B - 1), lb), :] * cw[0:1, :]
    for j in range(1, CONV_B):
        conv = conv + ext_ref[:, pl.ds(SUBLANES - (CONV_B - 1) + j, lb), :] * cw[j:j + 1, :]
    out_b = (b_gate * conv).reshape(sb * lb, D_B)
    hv = H_A * DV_A
    y = _bdot(oa_ref[...], w_ref[0:hv, :]) + _bdot(out_b, w_ref[hv:hv + D_B, :])
    d = y.shape[-1]
    xo_ref[...] = x_ref[...] + g1_ref[...] * y.reshape(sb, lb, d)
    tail_ref[...] = bx[:, lb - SUBLANES:lb, :]


def _even_out(gcx3, hist8, out_a2, x3, mod3, k_gate, conv_w, w_out_bf16):
    n_seq, length, d = x3.shape
    sb, lb, steps, nl = _tok_tiling(n_seq, length, TOK_TILE)
    rows = sb * lb
    w3 = gcx3.shape[-1]
    hv = H_A * DV_A
    halo_map = lambda i: (i // nl, jnp.maximum((i % nl) * (lb // SUBLANES) - 1, 0), 0)
    return pl.pallas_call(
        functools.partial(_even_out_kernel, sb=sb, lb=lb, nl=nl),
        grid=(steps,),
        in_specs=[pl.BlockSpec((sb, lb, w3), lambda i: (i // nl, i % nl, 0)),
                  pl.BlockSpec((sb, SUBLANES, w3), halo_map if nl > 1 else (lambda i: (i, 0, 0))),
                  pl.BlockSpec((sb, SUBLANES, D_B), lambda i: (i // nl, 0, 0)),
                  pl.BlockSpec((rows, hv), lambda i: (i, 0)),
                  pl.BlockSpec((sb, lb, d), lambda i: (i // nl, i % nl, 0)),
                  pl.BlockSpec((sb, 1, d), lambda i: (i // nl, 0, k_gate)),
                  pl.BlockSpec((CONV_B, D_B), lambda i: (0, 0)),
                  pl.BlockSpec((hv + D_B, d), lambda i: (0, 0))],
        out_specs=[pl.BlockSpec((sb, lb, d), lambda i: (i // nl, i % nl, 0)),
                   pl.BlockSpec((sb, SUBLANES, D_B), lambda i: (i // nl, 0, 0))],
        out_shape=[jax.ShapeDtypeStruct((n_seq, length, d), F32),
                   jax.ShapeDtypeStruct((n_seq, SUBLANES, D_B), F32)],
        scratch_shapes=[pltpu.VMEM((sb, SUBLANES + lb, D_B), F32)],
        compiler_params=_cparams(("arbitrary",), VMEM_LIMIT),
        name="even_out",
    )(gcx3, gcx3, hist8, out_a2, x3, mod3, conv_w, w_out_bf16)


def _proj_res_kernel(a_ref, x_ref, g_ref, w_ref, xo_ref, *, sb, lb):
    y = _bdot(a_ref[...], w_ref[...])
    xo_ref[...] = x_ref[...] + g_ref[...] * y.reshape(sb, lb, y.shape[-1])


def _proj_residual(a2, x3, mod3, k_gate, w_bf16):
    n_seq, length, d = x3.shape
    sb, lb, steps, nl = _tok_tiling(n_seq, length, TOK_TILE)
    rows = sb * lb
    ka = a2.shape[-1]
    return pl.pallas_call(
        functools.partial(_proj_res_kernel, sb=sb, lb=lb),
        grid=(steps,),
        in_specs=[pl.BlockSpec((rows, ka), lambda i: (i, 0)),
                  pl.BlockSpec((sb, lb, d), lambda i: (i // nl, i % nl, 0)),
                  pl.BlockSpec((sb, 1, d), lambda i: (i // nl, 0, k_gate)),
                  pl.BlockSpec((ka, d), lambda i: (0, 0))],
        out_specs=pl.BlockSpec((sb, lb, d), lambda i: (i // nl, i % nl, 0)),
        out_shape=jax.ShapeDtypeStruct((n_seq, length, d), F32),
        compiler_params=_cparams(("arbitrary",), VMEM_LIMIT),
        name="proj_residual",
    )(a2, x3, mod3, w_bf16)


def _router_kernel(x_ref, sh_ref, sc_ref, gain_ref, wrt_ref, bcol_ref, utri_ref,
                   mi_ref, gcol_ref, cnt_ref, *rest):
    carry_ref = rest[-1]
    i = pl.program_id(0)

    @pl.when(i == 0)
    def _():
        carry_ref[...] = jnp.zeros_like(carry_ref)

    if len(rest) == 2:
        rest[0][...] = jnp.zeros_like(rest[0])

    h2 = _norm_mod(x_ref[...], gain_ref[...], sc_ref[...], sh_ref[...])
    tm = h2.shape[0]
    logits = _bdot_nt(wrt_ref[...], h2) + bcol_ref[...]
    lg = logits[0:SUBLANES]
    e = jnp.exp(lg - jnp.max(lg, axis=0, keepdims=True))
    pg = e / jnp.sum(e, axis=0, keepdims=True)
    p_top = jnp.max(pg, axis=0, keepdims=True)
    rid8 = lax.broadcasted_iota(I32, (SUBLANES, tm), 0)
    g_top = jnp.min(jnp.where(pg == p_top, rid8, SUBLANES), axis=0, keepdims=True)
    le = logits[SUBLANES:SUBLANES + N_EXPERTS]
    sel = jnp.zeros((E_PER_GROUP, tm), F32)
    for gi in range(N_GROUPS):
        sel = sel + jnp.where(g_top == gi, le[gi * E_PER_GROUP:(gi + 1) * E_PER_GROUP], 0.0)
    e2 = jnp.exp(sel - jnp.max(sel, axis=0, keepdims=True))
    p_in = e2 / jnp.sum(e2, axis=0, keepdims=True)
    w_a = jnp.max(p_in, axis=0, keepdims=True)
    i_a = jnp.min(jnp.where(p_in == w_a, rid8, SUBLANES), axis=0, keepdims=True)
    p_rest = jnp.where(rid8 == i_a, -1.0, p_in)
    w_b = jnp.max(p_rest, axis=0, keepdims=True)
    i_b = jnp.min(jnp.where(p_rest == w_b, rid8, SUBLANES), axis=0, keepdims=True)
    den = w_a + w_b
    gate1 = p_top * (w_a / den)
    gate2 = p_top * (w_b / den)
    ex1 = g_top * E_PER_GROUP + i_a
    ex2 = g_top * E_PER_GROUP + i_b

    rid32 = lax.broadcasted_iota(I32, (N_EXPERTS, tm), 0)
    oh1 = rid32 == ex1
    oh2 = rid32 == ex2
    ohc = jnp.where(oh1, 1.0, jnp.where(oh2, 1.0, 0.0))
    cum = _bdot(ohc, utri_ref[...])
    carry = carry_ref[...]
    base = cum - ohc + carry[:, 0:1]
    r1 = jnp.sum(jnp.where(oh1, base, 0.0), axis=0, keepdims=True)
    r2 = jnp.sum(jnp.where(oh2, base, 0.0), axis=0, keepdims=True)
    new_carry = carry + cum[:, tm - 1:tm]
    carry_ref[...] = new_carry
    cnt_ref[...] = new_carry

    bro = lambda v: jnp.broadcast_to(v, (SUBLANES, tm))
    mi_ref[...] = jnp.where(rid8 == 0, bro(ex1),
                            jnp.where(rid8 == 1, bro(ex2),
                                      jnp.where(rid8 == 2, bro(r1.astype(I32)),
                                                jnp.where(rid8 == 3, bro(r2.astype(I32)), 0))))
    rid128 = lax.broadcasted_iota(I32, (LANES, tm), 0)
    g128 = jnp.where(rid128 == 0, jnp.broadcast_to(gate1, (LANES, tm)),
                     jnp.where(rid128 == 1, jnp.broadcast_to(gate2, (LANES, tm)), 0.0))
    gcol_ref[...] = g128.T


def _moe_router(x3, mod3, k_sh, k_sc, gain, wrt_bf16, bcol, xs_rows):
    n_seq, length, d = x3.shape
    sb, lb, steps, nl = _tok_tiling(n_seq, length, TOK_TILE)
    tm = sb * lb
    t = n_seq * length
    utri = jnp.triu(jnp.ones((tm, tm), F32)).astype(BF16)
    r_rows = wrt_bf16.shape[0]
    assert xs_rows % (steps * SUBLANES) == 0
    zrows = xs_rows // steps
    z_specs = [pl.BlockSpec((zrows, LANES), lambda i: (i, 0))] if xs_rows else []
    z_shapes = [jax.ShapeDtypeStruct((xs_rows, LANES), F32)] if xs_rows else []
    return pl.pallas_call(
        _router_kernel,
        grid=(steps,),
        in_specs=[pl.BlockSpec((sb, lb, d), lambda i: (i // nl, i % nl, 0)),
                  pl.BlockSpec((sb, 1, d), lambda i: (i // nl, 0, k_sh)),
                  pl.BlockSpec((sb, 1, d), lambda i: (i // nl, 0, k_sc)),
                  pl.BlockSpec((1, d), lambda i: (0, 0)),
                  pl.BlockSpec((r_rows, d), lambda i: (0, 0)),
                  pl.BlockSpec((r_rows, 1), lambda i: (0, 0)),
                  pl.BlockSpec((tm, tm), lambda i: (0, 0))],
        out_specs=[pl.BlockSpec((SUBLANES, tm), lambda i: (0, i)),
                   pl.BlockSpec((tm, LANES), lambda i: (i, 0)),
                   pl.BlockSpec((N_EXPERTS, LANES), lambda i: (0, 0))] + z_specs,
        out_shape=[jax.ShapeDtypeStruct((SUBLANES, t), I32),
                   jax.ShapeDtypeStruct((t, LANES), F32),
                   jax.ShapeDtypeStruct((N_EXPERTS, LANES), F32)] + z_shapes,
        scratch_shapes=[pltpu.VMEM((N_EXPERTS, LANES), F32)],
        compiler_params=_cparams(("arbitrary",), VMEM_LIMIT),
        name="moe_router",
    )(x3, mod3, mod3, gain.reshape(1, d), wrt_bf16, bcol, utri)


def _plan_kernel(mi_ref, cnt_ref, cnt_other_ref, ltri_ref, pos_ref, tinfo_ref, *, n_tiles_pad, row_tile, second):
    cnt = cnt_ref[...] + cnt_other_ref[...]
    nt = jnp.floor((cnt + (row_tile - 1)) * (1.0 / row_tile))
    tstart = _bdot(ltri_ref[...], nt)
    mi = mi_ref[...]
    tm = mi.shape[1]
    rid32 = lax.broadcasted_iota(I32, (N_EXPERTS, tm), 0)
    ts_col = tstart[:, 0:1]
    base_col = cnt_other_ref[:, 0:1] if second else jnp.zeros((N_EXPERTS, 1), F32)

    def pos_of(ex, rank):
        hit = rid32 == ex
        start = jnp.sum(jnp.where(hit, ts_col, 0.0), axis=0, keepdims=True)
        base = jnp.sum(jnp.where(hit, base_col, 0.0), axis=0, keepdims=True)
        return start.astype(I32) * row_tile + base.astype(I32) + rank

    p1 = pos_of(mi[0:1], mi[2:3])
    p2 = pos_of(mi[1:2], mi[3:4])
    rid8 = lax.broadcasted_iota(I32, (SUBLANES, tm), 0)
    pos_ref[0] = jnp.where(rid8 == 0, jnp.broadcast_to(p1, (SUBLANES, tm)),
                           jnp.where(rid8 == 1, jnp.broadcast_to(p2, (SUBLANES, tm)), 0))
    tend_col = ts_col + nt[:, 0:1]
    jt = lax.broadcasted_iota(I32, (N_EXPERTS, n_tiles_pad), 1).astype(F32)
    te = jnp.sum(jnp.where(tend_col <= jt, 1.0, 0.0), axis=0, keepdims=True)
    te = jnp.minimum(te, N_EXPERTS - 1.0).astype(I32)
    total = jnp.sum(nt[:, 0:1], axis=0, keepdims=True).astype(I32)
    rid8t = lax.broadcasted_iota(I32, (SUBLANES, n_tiles_pad), 0)
    tinfo_ref[...] = jnp.where(rid8t == 0, jnp.broadcast_to(te, (SUBLANES, n_tiles_pad)),
                               jnp.broadcast_to(total, (SUBLANES, n_tiles_pad)))


def _moe_plan(meta_i, counts, counts_other, n_tiles_pad, row_tile, second):
    t = meta_i.shape[1]
    tm = MOE_TOK_TILE
    steps = t // tm
    ltri = jnp.tril(jnp.ones((N_EXPERTS, N_EXPERTS), F32), k=-1).astype(BF16)
    return pl.pallas_call(
        functools.partial(_plan_kernel, n_tiles_pad=n_tiles_pad, row_tile=row_tile, second=second),
        grid=(steps,),
        in_specs=[pl.BlockSpec((SUBLANES, tm), lambda i: (0, i)),
                  pl.BlockSpec((N_EXPERTS, LANES), lambda i: (0, 0)),
                  pl.BlockSpec((N_EXPERTS, LANES), lambda i: (0, 0)),
                  pl.BlockSpec((N_EXPERTS, N_EXPERTS), lambda i: (0, 0))],
        out_specs=[pl.BlockSpec((1, SUBLANES, tm), lambda i: (i, 0, 0)),
                   pl.BlockSpec((SUBLANES, n_tiles_pad), lambda i: (0, 0))],
        out_shape=[jax.ShapeDtypeStruct((steps, SUBLANES, tm), I32),
                   jax.ShapeDtypeStruct((SUBLANES, n_tiles_pad), I32)],
        compiler_params=_cparams(("arbitrary",)),
        name="moe_plan",
    )(meta_i, counts, counts_other, ltri)


def _to_token_tiles(ref, val):
    tm, d = val.shape
    for s in range(d // LANES):
        ref[pl.ds(s, tm, stride=SUBLANES), :] = val[:, s * LANES:(s + 1) * LANES]


def _from_token_tiles(ref, tm):
    return jnp.concatenate([ref[pl.ds(s, tm, stride=SUBLANES), :] for s in range(SUBLANES)], axis=-1)


def _tile_rows(idx):
    return pl.ds(pl.multiple_of(idx * SUBLANES, SUBLANES), SUBLANES)


def _row_copy_out(buf_ref, slot, r, dst_hbm, p, sem):
    return pltpu.make_async_copy(buf_ref.at[slot, _tile_rows(r), :], dst_hbm.at[_tile_rows(p), :], sem.at[slot])


def _dispatch_kernel(x_ref, sh_ref, sc_ref, gain_ref, pos_ref, xs_in_ref, xs_ref,
                     buf_ref, idx_ref, sem_ref, isem_ref, *, tm, steps):
    del xs_in_ref
    i = pl.program_id(0)
    slot = i % 2

    def drain(s):
        for _ in range(2):
            pltpu.make_async_copy(buf_ref.at[s], xs_ref.at[pl.ds(0, tm * SUBLANES), :], sem_ref.at[s]).wait()

    @pl.when(i >= 2)
    def _():
        drain(slot)

    _to_token_tiles(buf_ref.at[slot], _norm_mod(x_ref[...], gain_ref[...], sc_ref[...], sh_ref[...]))
    cp = pltpu.make_async_copy(pos_ref.at[0], idx_ref, isem_ref)
    cp.start()
    cp.wait()

    def issue(r, carry):
        _row_copy_out(buf_ref, slot, r, xs_ref, idx_ref[0, r], sem_ref).start(priority=0)
        _row_copy_out(buf_ref, slot, r, xs_ref, idx_ref[1, r], sem_ref).start(priority=1)
        return carry
    lax.fori_loop(0, tm, issue, 0, unroll=8)

    @pl.when(i == steps - 1)
    def _():
        drain(slot)
        if steps > 1:
            drain(1 - slot)


def _moe_dispatch(x3, mod3, k_sh, k_sc, gain, pos3, xs0):
    n_seq, length, d = x3.shape
    assert d == SUBLANES * LANES
    sb, lb, steps, nl = _tok_tiling(n_seq, length, MOE_TOK_TILE)
    tm = sb * lb
    return pl.pallas_call(
        functools.partial(_dispatch_kernel, tm=tm, steps=steps),
        grid=(steps,),
        in_specs=[pl.BlockSpec((sb, lb, d), lambda i: (i // nl, i % nl, 0)),
                  pl.BlockSpec((sb, 1, d), lambda i: (i // nl, 0, k_sh)),
                  pl.BlockSpec((sb, 1, d), lambda i: (i // nl, 0, k_sc)),
                  pl.BlockSpec((1, d), lambda i: (0, 0)),
                  pl.BlockSpec((1, SUBLANES, tm), lambda i: (i, 0, 0)),
                  pl.BlockSpec(memory_space=pl.ANY)],
        out_specs=pl.BlockSpec(memory_space=pl.ANY),
        out_shape=jax.ShapeDtypeStruct(xs0.shape, F32),
        scratch_shapes=[pltpu.VMEM((2, tm * SUBLANES, LANES), F32),
                        pltpu.SMEM((SUBLANES, tm), I32),
                        pltpu.SemaphoreType.DMA((2,)),
                        pltpu.SemaphoreType.DMA(())],
        input_output_aliases={5: 0},
        compiler_params=_cparams(("arbitrary",), VMEM_LIMIT),
        name="moe_dispatch",
    )(x3, mod3, mod3, gain.reshape(1, d), pos3, xs0)


def _expert_kernel(te_ref, nt_ref, xs_ref, wg_ref, wu_ref, wd_ref, ys_ref, *, row_tile):
    i = pl.program_id(0)

    @pl.when(i < nt_ref[0])
    def _():
        x = _from_token_tiles(xs_ref, row_tile).astype(BF16)
        hid = _silu(_bdot(x, wg_ref[...])) * _bdot(x, wu_ref[...])
        _to_token_tiles(ys_ref, _bdot(hid, wd_ref[...]))

    @pl.when(i >= nt_ref[0])
    def _():
        ys_ref[...] = jnp.zeros_like(ys_ref)


def _moe_experts(xs, tile_expert, n_tiles, wg, wu, wd, layer, row_tile):
    d, f = wg.shape[-2], wg.shape[-1]
    steps = xs.shape[0] // (row_tile * SUBLANES)
    x_map = lambda i, te, nt: (jnp.minimum(i, nt[0] - 1), 0)
    w_map = lambda i, te, nt: (layer, te[i] // E_PER_GROUP, te[i] % E_PER_GROUP, 0, 0)
    grid_spec = pltpu.PrefetchScalarGridSpec(
        num_scalar_prefetch=2,
        grid=(steps,),
        in_specs=[pl.BlockSpec((row_tile * SUBLANES, LANES), x_map),
                  pl.BlockSpec((None, None, None, d, f), w_map),
                  pl.BlockSpec((None, None, None, d, f), w_map),
                  pl.BlockSpec((None, None, None, f, d), w_map)],
        out_specs=pl.BlockSpec((row_tile * SUBLANES, LANES), lambda i, te, nt: (i, 0)),
    )
    return pl.pallas_call(
        functools.partial(_expert_kernel, row_tile=row_tile),
        grid_spec=grid_spec,
        out_shape=jax.ShapeDtypeStruct(xs.shape, F32),
        compiler_params=_cparams(("arbitrary",), VMEM_LIMIT),
        name="moe_experts",
    )(tile_expert, n_tiles, xs, wg, wu, wd)


def _row_copy_in(src_hbm, p, buf_ref, slot, which, r, sem):
    return pltpu.make_async_copy(src_hbm.at[_tile_rows(p), :], buf_ref.at[slot, which, _tile_rows(r), :],
                                 sem.at[slot])


def _combine_kernel(x_ref, g_ref, gcol_ref, pos_ref, posn_ref, ys_ref, gainf_ref, xo_ref,
                    buf_ref, idx_ref, sem_ref, isem_ref, *, tm, steps, sb, lb, final_norm):
    i = pl.program_id(0)
    slot = i % 2

    def issue(pref, s):
        cp = pltpu.make_async_copy(pref.at[0], idx_ref, isem_ref)
        cp.start()
        cp.wait()

        def body(r, carry):
            _row_copy_in(ys_ref, idx_ref[0, r], buf_ref, s, 0, r, sem_ref).start(priority=0)
            _row_copy_in(ys_ref, idx_ref[1, r], buf_ref, s, 1, r, sem_ref).start(priority=1)
            return carry
        lax.fori_loop(0, tm, body, 0, unroll=8)

    @pl.when(i == 0)
    def _():
        issue(pos_ref, slot)

    @pl.when(i + 1 < steps)
    def _():
        issue(posn_ref, 1 - slot)

    for which in range(2):
        pltpu.make_async_copy(ys_ref.at[pl.ds(0, tm * SUBLANES), :], buf_ref.at[slot, which],
                              sem_ref.at[slot]).wait()

    gc = gcol_ref[...]
    y = gc[:, 0:1] * _from_token_tiles(buf_ref.at[slot, 0], tm) \
        + gc[:, 1:2] * _from_token_tiles(buf_ref.at[slot, 1], tm)
    d = y.shape[-1]
    xn = x_ref[...] + g_ref[...] * y.reshape(sb, lb, d)
    if final_norm:
        xn = xn * lax.rsqrt(jnp.mean(xn * xn, axis=-1, keepdims=True) + EPS) * gainf_ref[...]
    xo_ref[...] = xn


def _moe_combine(x3, mod3, k_gate, gates_col, pos3, ys, gain_f, final_norm):
    n_seq, length, d = x3.shape
    sb, lb, steps, nl = _tok_tiling(n_seq, length, MOE_TOK_TILE)
    tm = sb * lb
    return pl.pallas_call(
        functools.partial(_combine_kernel, tm=tm, steps=steps, sb=sb, lb=lb, final_norm=final_norm),
        grid=(steps,),
        in_specs=[pl.BlockSpec((sb, lb, d), lambda i: (i // nl, i % nl, 0)),
                  pl.BlockSpec((sb, 1, d), lambda i: (i // nl, 0, k_gate)),
                  pl.BlockSpec((tm, LANES), lambda i: (i, 0)),
                  pl.BlockSpec((1, SUBLANES, tm), lambda i: (i, 0, 0)),
                  pl.BlockSpec((1, SUBLANES, tm), lambda i: (jnp.minimum(i + 1, steps - 1), 0, 0)),
                  pl.BlockSpec(memory_space=pl.ANY),
                  pl.BlockSpec((1, d), lambda i: (0, 0))],
        out_specs=pl.BlockSpec((sb, lb, d), lambda i: (i // nl, i % nl, 0)),
        out_shape=jax.ShapeDtypeStruct((n_seq, length, d), F32),
        scratch_shapes=[pltpu.VMEM((2, 2, tm * SUBLANES, LANES), F32),
                        pltpu.SMEM((SUBLANES, tm), I32),
                        pltpu.SemaphoreType.DMA((2,)),
                        pltpu.SemaphoreType.DMA(())],
        compiler_params=_cparams(("arbitrary",), VMEM_LIMIT),
        name="moe_combine",
    )(x3, mod3, gates_col, pos3, pos3, ys, gain_f.reshape(1, d))


def _hier_moe_residual(groups, k_sh, k_sc, k_gate, gain, wrt_bf16, bcol, wg, wu, wd, layer, gain_f, final_norm):
    assert len(groups) == 2
    t_all = sum(x3.shape[0] * x3.shape[1] for x3, _ in groups)
    row_tile = MOE_ROW_TILE_BIG if (2 * t_all) // N_EXPERTS >= MOE_ROW_TILE_BIG else MOE_ROW_TILE_SMALL
    max_tiles = -(-(2 * t_all) // row_tile) + N_EXPERTS
    n_tiles_pad = -(-max_tiles // LANES) * LANES
    xs_rows = max_tiles * row_tile * SUBLANES
    routed = [_moe_router(x3, mod3, k_sh, k_sc, gain, wrt_bf16, bcol, xs_rows if gi == 0 else 0)
              for gi, (x3, mod3) in enumerate(groups)]
    xs = routed[0][3]
    plans = [_moe_plan(routed[gi][0], routed[gi][2], routed[1 - gi][2], n_tiles_pad, row_tile, gi == 1)
             for gi in range(2)]
    for (x3, mod3), (pos3, _) in zip(groups, plans):
        xs = _moe_dispatch(x3, mod3, k_sh, k_sc, gain, pos3, xs)
    tinfo = plans[0][1]
    ys = _moe_experts(xs, tinfo[0, :max_tiles], tinfo[1, :1], wg, wu, wd, layer, row_tile)
    return [_moe_combine(x3, mod3, k_gate, r[1], pos3, ys, gain_f, final_norm)
            for (x3, mod3), r, (pos3, _) in zip(groups, routed, plans)]


def _multiplicity(dist):
    m = jnp.zeros(dist.shape, F32)
    for window, dil in BRANCHES:
        m = m + ((dist >= 0) & (dist <= window) & (dist % dil == 0)).astype(F32)
    return m


def _log_mult(dist):
    m = _multiplicity(dist)
    return jnp.where(m > 0, jnp.log(jnp.maximum(m, 1.0)), NEG_BIG)


def _attn_prompt_kernel(q_ref, k_ref, v_ref, nbias_ref, fbias_ref, o_ref, mf_ref, lf_ref, af_ref, *, length):
    blk = ATT_BLK
    nq = length // blk
    near = ATT_NEAR_BLOCKS
    ncls = BRANCHES[-1][1]
    csz = length // ncls
    has_far = nq > near
    scale = (HD_C ** -0.5) * LOG2_E
    nt = lambda a, b: lax.dot_general(a, b, (((1,), (1,)), ((), ())), preferred_element_type=F32)
    mm = lambda a, b: jnp.dot(a, b, preferred_element_type=F32)

    def split_heads(q):
        head0 = lax.broadcasted_iota(I32, q.shape, 1) < HD_C
        return head0, jnp.where(head0, q, 0.0).astype(BF16), jnp.where(head0, 0.0, q).astype(BF16)

    if has_far:
        fbias = fbias_ref[...]
        group = 4
        for r0 in range(0, ncls, group):
            rs = list(range(r0, r0 + group))
            rows = [pl.ds(r, csz, stride=ncls) for r in rs]
            hq = [split_heads(q_ref[0, rw, :] * scale) for rw in rows]
            kr = [k_ref[0, rw, :].astype(BF16) for rw in rows]
            vr = [v_ref[0, rw, :].astype(BF16) for rw in rows]
            s0 = [nt(h[1], k) + fbias for h, k in zip(hq, kr)]
            s1 = [nt(h[2], k) + fbias for h, k in zip(hq, kr)]
            m0 = [jnp.max(s, axis=-1, keepdims=True) for s in s0]
            m1 = [jnp.max(s, axis=-1, keepdims=True) for s in s1]
            p0 = [jnp.exp2(s - m) for s, m in zip(s0, m0)]
            p1 = [jnp.exp2(s - m) for s, m in zip(s1, m1)]
            a0 = [mm(p.astype(BF16), v) for p, v in zip(p0, vr)]
            a1 = [mm(p.astype(BF16), v) for p, v in zip(p1, vr)]
            for i, rw in enumerate(rows):
                head0 = hq[i][0]
                af_ref[rw, :] = jnp.where(head0, a0[i], a1[i])
                mf_ref[rw, :] = jnp.where(head0, m0[i], m1[i])
                lf_ref[rw, :] = jnp.where(head0, jnp.sum(p0[i], axis=-1, keepdims=True),
                                          jnp.sum(p1[i], axis=-1, keepdims=True))

    for qi in range(nq):
        rq = slice(qi * blk, (qi + 1) * blk)
        head0, q0, q1 = split_heads(q_ref[0, rq, :] * scale)
        js = list(range(max(0, qi - near + 1), qi + 1))
        ks = [k_ref[0, kj * blk:(kj + 1) * blk, :].astype(BF16) for kj in js]
        vs = [v_ref[0, kj * blk:(kj + 1) * blk, :].astype(BF16) for kj in js]
        far = has_far and qi >= near

        def fold(x, op):
            parts = [x[:, c * LANES:(c + 1) * LANES] for c in range(blk // LANES)]
            return functools.reduce(op, parts)

        q0d, q1d = q0 * 2, q1 * 2
        mr0 = functools.reduce(jnp.maximum, [fold(nt(q0d, k) * 0.5 + nbias_ref[qi - kj], jnp.maximum)
                                             for k, kj in zip(ks, js)])
        mr1 = functools.reduce(jnp.maximum, [fold(nt(q1d, k) * 0.5 + nbias_ref[qi - kj], jnp.maximum)
                                             for k, kj in zip(ks, js)])
        m0 = jnp.max(mr0, axis=-1, keepdims=True)
        m1 = jnp.max(mr1, axis=-1, keepdims=True)
        if far:
            mf = mf_ref[rq, :]
            mf0, mf1 = mf[:, 0:1], mf[:, HD_C:HD_C + 1]
            m0 = jnp.maximum(m0, mf0)
            m1 = jnp.maximum(m1, mf1)
        lr0 = lr1 = a0 = a1 = None
        for k, v, kj in zip(ks, vs, js):
            bias = nbias_ref[qi - kj]
            p0 = jnp.exp2(nt(q0, k) + bias - m0)
            p1 = jnp.exp2(nt(q1, k) + bias - m1)
            f0, f1 = fold(p0, jnp.add), fold(p1, jnp.add)
            t0, t1 = mm(p0.astype(BF16), v), mm(p1.astype(BF16), v)
            lr0, lr1 = (f0, f1) if lr0 is None else (lr0 + f0, lr1 + f1)
            a0, a1 = (t0, t1) if a0 is None else (a0 + t0, a1 + t1)
        l0 = jnp.sum(lr0, axis=-1, keepdims=True)
        l1 = jnp.sum(lr1, axis=-1, keepdims=True)
        acc = jnp.where(head0, a0, a1)
        den = jnp.where(head0, l0, l1)
        if far:
            w_far = jnp.where(head0, jnp.exp2(mf0 - m0), jnp.exp2(mf1 - m1))
            acc = acc + af_ref[rq, :] * w_far
            den = den + lf_ref[rq, :] * w_far
        o_ref[0, rq, :] = (acc / den).astype(o_ref.dtype)


def _attn_prompt(q3, k3, v3):
    n_seq, length, d = q3.shape
    pairs = d // LANES
    blk = ATT_BLK
    near = ATT_NEAR_BLOCKS
    ncls = BRANCHES[-1][1]
    csz = length // ncls
    per = blk // ncls
    assert (near - 1) * blk >= BRANCHES[-2][0] and length % blk == 0 and blk % ncls == 0
    a = jnp.arange(blk)
    dist = (jnp.arange(near)[:, None, None] * blk) + a[None, :, None] - a[None, None, :]
    nbias = _log_mult(dist) * jnp.where(_multiplicity(dist) > 0, LOG2_E, 1.0)
    m = jnp.arange(csz)
    far_ok = ((m[:, None] // per - m[None, :] // per) >= near) & \
        ((m[:, None] - m[None, :]) * ncls <= BRANCHES[-1][0])
    fbias = jnp.where(far_ok, 0.0, NEG_BIG).astype(F32)
    return pl.pallas_call(
        functools.partial(_attn_prompt_kernel, length=length),
        grid=(n_seq, pairs),
        in_specs=[pl.BlockSpec((1, length, LANES), lambda n, p: (n, 0, p)),
                  pl.BlockSpec((1, length, LANES), lambda n, p: (n, 0, p)),
                  pl.BlockSpec((1, length, LANES), lambda n, p: (n, 0, p)),
                  pl.BlockSpec((near, blk, blk), lambda n, p: (0, 0, 0)),
                  pl.BlockSpec((csz, csz), lambda n, p: (0, 0))],
        out_specs=pl.BlockSpec((1, length, LANES), lambda n, p: (n, 0, p)),
        out_shape=jax.ShapeDtypeStruct((n_seq, length, d), BF16),
        scratch_shapes=[pltpu.VMEM((length, LANES), F32)] * 3,
        compiler_params=_cparams(("arbitrary", "arbitrary"), VMEM_LIMIT),
        name="attn_prompt",
    )(q3, k3, v3, nbias, fbias)


def _attn_sample_kernel(q_ref, kn_ref, vn_ref, kt_ref, vt_ref, bias_ref, biasn_ref, o_ref, *, heads):
    q = q_ref[0] * (HD_C ** -0.5)
    kn = kn_ref[0]
    vn = vn_ref[0]
    bias = bias_ref[...]
    biasn = biasn_ref[...]
    outs = []
    for h in range(heads):
        lo, hi = h * HD_C, (h + 1) * HD_C
        qh = q[:, lo:hi].astype(BF16)
        s = jnp.dot(qh, kt_ref[0, h].astype(BF16), preferred_element_type=F32) + bias
        sn = _bdot_nt(qh, kn[:, lo:hi]) + biasn
        m = jnp.maximum(jnp.max(s, axis=-1, keepdims=True), jnp.max(sn, axis=-1, keepdims=True))
        p = jnp.exp(s - m)
        pn = jnp.exp(sn - m)
        den = jnp.sum(p, axis=-1, keepdims=True) + jnp.sum(pn, axis=-1, keepdims=True)
        o = _bdot_nt(p, vt_ref[0, h]) + _bdot(pn, vn[:, lo:hi])
        outs.append(o / den)
    o_ref[0] = jnp.concatenate(outs, axis=-1)


def _attn_sample(q3, k3, v3, k_cache, v_cache):
    n_seq, t_len, d = q3.shape
    w_buf = k_cache.shape[1]
    kt = jnp.transpose(k_cache, (0, 2, 3, 1))
    vt = jnp.transpose(v_cache, (0, 2, 3, 1))
    heads = SAMPLE_HEADS_PER_STEP
    hw = heads * HD_C
    nj = H_C // heads
    tpos = w_buf + jnp.arange(t_len)
    bias = _log_mult(tpos[:, None] - jnp.arange(w_buf)[None, :])
    biasn = _log_mult(jnp.arange(t_len)[:, None] - jnp.arange(t_len)[None, :])
    return pl.pallas_call(
        functools.partial(_attn_sample_kernel, heads=heads),
        grid=(n_seq, nj),
        in_specs=[pl.BlockSpec((1, t_len, hw), lambda n, j: (n, 0, j)),
                  pl.BlockSpec((1, t_len, hw), lambda n, j: (n, 0, j)),
                  pl.BlockSpec((1, t_len, hw), lambda n, j: (n, 0, j)),
                  pl.BlockSpec((1, heads, HD_C, w_buf), lambda n, j: (n, j, 0, 0)),
                  pl.BlockSpec((1, heads, HD_C, w_buf), lambda n, j: (n, j, 0, 0)),
                  pl.BlockSpec((t_len, w_buf), lambda n, j: (0, 0)),
                  pl.BlockSpec((t_len, t_len), lambda n, j: (0, 0))],
        out_specs=pl.BlockSpec((1, t_len, hw), lambda n, j: (n, 0, j)),
        out_shape=jax.ShapeDtypeStruct((n_seq, t_len, d), F32),
        compiler_params=_cparams(("arbitrary", "arbitrary"), VMEM_LIMIT),
        name="attn_sample",
    )(q3, k3, v3, kt, vt, bias, biasn)


def _pad_hist(hist, n_seq, width):
    k1 = hist.shape[1]
    return jnp.concatenate([jnp.zeros((n_seq, SUBLANES - k1, width), F32), hist.astype(F32)], axis=1)


def _prep_weights(p):
    d = p['w_in_even'].shape[1]
    out = {}
    w_in = p['w_in_even'][0]
    hv = H_A * DV_A
    o_z = QKV_A
    o_a = o_z + hv
    o_g = o_a + 2 * H_A
    ab_pad = jnp.zeros((d, LANES - 2 * H_A), F32)
    out['w_in'] = jnp.concatenate([w_in[:, :o_a], w_in[:, o_a:o_g], ab_pad, w_in[:, o_g:]], axis=1).astype(BF16)
    out['w_out_even'] = p['w_out_even'][0].astype(BF16)
    out['w_qkv'] = p['w_qkv_odd'][0].astype(BF16)
    out['w_out_odd'] = p['w_out_odd'][0].astype(BF16)
    depth = p['w_router_group'].shape[0]
    wrt, bcol = [], []
    for l in range(depth):
        pad_w = jnp.zeros((SUBLANES - N_GROUPS, d), F32)
        wrt.append(jnp.concatenate([p['w_router_group'][l].T, pad_w, p['w_router_expert'][l].T], axis=0).astype(BF16))
        pad_b = jnp.full((SUBLANES - N_GROUPS,), NEG_BIG, F32)
        bcol.append(jnp.concatenate([p['b_router_group'][l], pad_b, p['b_router_expert'][l]]).reshape(-1, 1))
    out.update(wrt=wrt, bcol=bcol, wg=p['w_exp_gate'], wu=p['w_exp_up'], wd=p['w_exp_down'])
    return out


def _layer0_mixers(x3, mod3, a_ssm, a_conv, b_conv, p, w):
    n_seq, length, d = x3.shape
    t = n_seq * length
    hv = H_A * DV_A
    qkv_pre, z, ab, gcx = _norm_mod_matmul(x3, mod3, 0, 1, p['norm1'][0], w['w_in'],
                                           (QKV_A, hv, LANES, 3 * D_B))
    chunk = min(CHUNK_PROMPT, length)
    g_chunks = DELTA_CHUNKS_PER_STEP if length > chunk else DELTA_SEQS_PER_STEP
    qkv_pre3 = qkv_pre.reshape(n_seq, length, QKV_A)
    ub, wm, qd, kd, qk, dec = _delta_prep(qkv_pre3, _pad_hist(a_conv[0], n_seq, QKV_A),
                                          ab.reshape(n_seq, length, LANES), p['conv_a_w'][0],
                                          p['a_log'][0], p['dt_bias'][0], chunk, g_chunks)
    out_a, s_new = _delta_scan(ub, wm, qd, kd, qk, dec, z.reshape(n_seq, length, hv),
                               a_ssm[0].astype(F32), p['o_gain_a'][0], chunk)
    x3, bx_tail = _even_out(gcx.reshape(n_seq, length, 3 * D_B), _pad_hist(b_conv[0], n_seq, D_B),
                            out_a.reshape(t, hv), x3, mod3, 2, p['conv_b_w'][0], w['w_out_even'])
    new_aconv = jnp.concatenate([a_conv[0].astype(F32), qkv_pre3], axis=1)[:, -(CONV_A - 1):]
    new_bconv = jnp.concatenate([b_conv[0].astype(F32), bx_tail], axis=1)[:, -(CONV_B - 1):]
    return x3, s_new[None], new_aconv[None], new_bconv[None]


def _layer1_attention(x3, mod3, kv_k, kv_v, p, w, sample):
    n_seq, length, d = x3.shape
    t = n_seq * length
    if sample:
        q_new, k_new, v_new = (a.reshape(n_seq, length, d) for a in
                               _norm_mod_matmul(x3, mod3, 0, 1, p['norm1'][1], w['w_qkv'], (d, d, d)))
        attn = _attn_sample(q_new, k_new, v_new, kv_k[0], kv_v[0])
        new_k = k_new.reshape(n_seq, length, H_C, HD_C)
        new_v = v_new.reshape(n_seq, length, H_C, HD_C)
    else:
        q_new, k_new, v_new, k_t, v_t = _norm_mod_matmul(x3, mod3, 0, 1, p['norm1'][1], w['w_qkv'],
                                                         (d, d, d), transposed=(1, 2))
        q_new, k_new, v_new = (a.reshape(n_seq, length, d) for a in (q_new, k_new, v_new))
        attn = _attn_prompt(q_new, k_new, v_new)
        keep = min(BRANCHES[-1][0], length)
        new_k = jnp.transpose(k_t[:, :, length - keep:].reshape(n_seq, H_C, HD_C, keep), (0, 3, 1, 2))
        new_v = jnp.transpose(v_t[:, :, length - keep:].reshape(n_seq, H_C, HD_C, keep), (0, 3, 1, 2))
    x3 = _proj_residual(attn.reshape(t, d), x3, mod3, 2, w['w_out_odd'])
    return x3, new_k[None], new_v[None]


def kernel(x_prompt, x_sample, state_a_ssm, state_a_conv, state_b_conv, cache_c_k, cache_c_v, c_prompt, c_sample, w_ada, b_ada, norm1, norm2, norm_f, w_in_even, conv_a_w, a_log, dt_bias, o_gain_a, conv_b_w, w_out_even, w_qkv_odd, w_out_odd, w_router_group, b_router_group, w_router_expert, b_router_expert, w_exp_gate, w_exp_up, w_exp_down):
    p = dict(norm1=norm1, norm2=norm2, norm_f=norm_f, w_in_even=w_in_even, conv_a_w=conv_a_w,
             a_log=a_log, dt_bias=dt_bias, o_gain_a=o_gain_a, conv_b_w=conv_b_w, w_out_even=w_out_even,
             w_qkv_odd=w_qkv_odd, w_out_odd=w_out_odd, w_router_group=w_router_group,
             b_router_group=b_router_group, w_router_expert=w_router_expert,
             b_router_expert=b_router_expert, w_exp_gate=w_exp_gate, w_exp_up=w_exp_up,
             w_exp_down=w_exp_down)
    w = _prep_weights(p)
    nb = x_prompt.shape[0]
    ns = x_sample.shape[0]
    mod_all = _ada_mod(jnp.concatenate([c_prompt, c_sample], axis=0), w_ada, b_ada)
    depth = w_ada.shape[0]
    mod_p = [mod_all[l, :nb] for l in range(depth)]
    mod_s = [mod_all[l, nb:] for l in range(depth)]
    n_even = state_a_ssm.shape[0]
    zero_ssm = jnp.zeros((n_even, nb, H_A, DK_A, DV_A), F32)
    zero_aconv = jnp.zeros((n_even, nb, CONV_A - 1, QKV_A), F32)
    zero_bconv = jnp.zeros((n_even, nb, CONV_B - 1, D_B), F32)
    d = x_prompt.shape[-1]
    m3_p = [m.reshape(nb, 1, 6 * d) for m in mod_p]
    m3_s = [m.reshape(ns, 1, 6 * d) for m in mod_s]

    def moe(xp, xs_, layer, final_norm):
        return _hier_moe_residual([(xp, m3_p[layer]), (xs_, m3_s[layer])], 3, 4, 5, norm2[layer],
                                  w['wrt'][layer], w['bcol'][layer], w['wg'], w['wu'], w['wd'], layer,
                                  norm_f, final_norm)

    xp, ssm_p, aconv_p, bconv_p = _layer0_mixers(x_prompt, m3_p[0], zero_ssm, zero_aconv, zero_bconv, p, w)
    xs_, ssm_s, aconv_s, bconv_s = _layer0_mixers(x_sample, m3_s[0], state_a_ssm, state_a_conv,
                                                  state_b_conv, p, w)
    xp, xs_ = moe(xp, xs_, 0, False)
    xp, k_p, v_p = _layer1_attention(xp, m3_p[1], None, None, p, w, False)
    xs_, k_s, v_s = _layer1_attention(xs_, m3_s[1], cache_c_k, cache_c_v, p, w, True)
    y_p, y_s = moe(xp, xs_, 1, True)
    return (y_p, y_s, ssm_p, ssm_s, aconv_p, aconv_s, bconv_p, bconv_s, k_p, k_s, v_p, v_s)
```
